```python
import math
import jax, jax.numpy as jnp
from jax import lax
import numpy as np

D_MODEL = 1024
BATCH = 4
SEQ = 8192
DEPTH = 4

BLOCK = 128
MLA_HEADS = 6
MLA_Q_RANK = 192
MLA_KV_RANK = 128
MLA_NOPE = 64
MLA_ROPE = 32
MLA_V = 64
DIFF_HEADS = 4
DIFF_QK = 32
DIFF_V = 64
SWA_HEADS = 6
SWA_KV_HEADS = 2
SWA_DIM = 64
WINDOW = 128
W_MLA = MLA_HEADS * MLA_V
W_DIFF = DIFF_HEADS * DIFF_V
W_SWA = SWA_HEADS * SWA_DIM
D_MIX = W_MLA + W_DIFF + W_SWA
IN_SIZES = (MLA_Q_RANK, MLA_KV_RANK, MLA_ROPE,
            DIFF_HEADS * 2 * DIFF_QK, DIFF_HEADS * 2 * DIFF_QK, W_DIFF,
            W_SWA, SWA_KV_HEADS * SWA_DIM, SWA_KV_HEADS * SWA_DIM,
            D_MIX)
D_IN = sum(IN_SIZES)
REL_BUCKETS = 32
REL_MAX_DIST = 128
BIAS_HEADS = DIFF_HEADS + SWA_HEADS
ROPE_THETA = 10000.0
DEEPNORM_ALPHA = (2 * DEPTH) ** 0.25
DEEPNORM_BETA = (8 * DEPTH) ** -0.25

kernel_name = "hymba_style_mla_diff_swa_encoder"


def rms_norm(x, g, eps=1e-6):
    xf = x.astype(jnp.float32)
    y = xf * lax.rsqrt(jnp.mean(xf * xf, axis=-1, keepdims=True) + eps)
    return (y * g.astype(jnp.float32)).astype(x.dtype)


def layer_norm(x, g, b, eps=1e-5):
    xf = x.astype(jnp.float32)
    mu = jnp.mean(xf, axis=-1, keepdims=True)
    var = jnp.mean(jnp.square(xf - mu), axis=-1, keepdims=True)
    y = (xf - mu) * lax.rsqrt(var + eps) * g.astype(jnp.float32) + b.astype(jnp.float32)
    return y.astype(x.dtype)


def t5_bucket(rel):
    half = REL_BUCKETS // 2
    max_exact = half // 2
    ret = jnp.where(rel > 0, half, 0)
    n = jnp.abs(rel)
    nf = jnp.maximum(n, 1).astype(jnp.float32)
    large = max_exact + (jnp.log(nf / max_exact) / math.log(REL_MAX_DIST / max_exact)
                         * (half - max_exact)).astype(jnp.int32)
    large = jnp.minimum(large, half - 1)
    return ret + jnp.where(n < max_exact, n, large)


def rope(x, pos):
    half = x.shape[-1] // 2
    freqs = ROPE_THETA ** (-jnp.arange(half, dtype=jnp.float32) / half)
    ang = pos.astype(jnp.float32)[:, None] * freqs[None, :]
    cos = jnp.cos(ang)[None, :, None, :]
    sin = jnp.sin(ang)[None, :, None, :]
    xf = x.astype(jnp.float32)
    x1, x2 = xf[..., :half], xf[..., half:]
    return jnp.concatenate([x1 * cos - x2 * sin, x2 * cos + x1 * sin], axis=-1).astype(x.dtype)


def to_blocks(t):
    b, s = t.shape[:2]
    return jnp.moveaxis(t.reshape(b, s // BLOCK, BLOCK, *t.shape[2:]), 1, 0)


def from_blocks(t):
    nb, b = t.shape[:2]
    return jnp.moveaxis(t, 0, 1).reshape(b, nb * BLOCK, *t.shape[3:])


def mla_attention(c_q, c_kv, k_rope_in, q_norm, kv_norm, w_uq, w_ukv, pos):
    b, s, _ = c_q.shape
    q = (rms_norm(c_q, q_norm) @ w_uq).reshape(b, s, MLA_HEADS, MLA_NOPE + MLA_ROPE)
    q_nope = q[..., :MLA_NOPE]
    q_rope = rope(q[..., MLA_NOPE:], pos)
    kv = (rms_norm(c_kv, kv_norm) @ w_ukv).reshape(b, s, MLA_HEADS, MLA_NOPE + MLA_V)
    k_nope, v = kv[..., :MLA_NOPE], kv[..., MLA_NOPE:]
    k_rope = rope(k_rope_in[:, :, None, :], pos)[:, :, 0, :]
    scale = 1.0 / math.sqrt(MLA_NOPE + MLA_ROPE)

    def block(args):
        qn, qr = args
        sc = (jnp.einsum('bqhd,bkhd->bhqk', qn, k_nope)
              + jnp.einsum('bqhr,bkr->bhqk', qr, k_rope))
        p = jax.nn.softmax(sc.astype(jnp.float32) * scale, axis=-1)
        return jnp.einsum('bhqk,bkhd->bqhd', p.astype(v.dtype), v)

    o = lax.map(block, (to_blocks(q_nope), to_blocks(q_rope)))
    return from_blocks(o).reshape(b, s, W_MLA)


def diff_attention(q, k, v, lam_vecs, subln, rel_bias, lam_init):
    b, s, _ = q.shape
    nb = s // BLOCK
    q = q.reshape(b, s, DIFF_HEADS, 2, DIFF_QK)
    k = k.reshape(b, s, DIFF_HEADS, 2, DIFF_QK)
    v = v.reshape(b, s, DIFF_HEADS, DIFF_V)
    lv = lam_vecs.astype(jnp.float32)
    lam = jnp.exp(jnp.sum(lv[0] * lv[1])) - jnp.exp(jnp.sum(lv[2] * lv[3])) + lam_init
    table = rel_bias[:, :DIFF_HEADS].T.astype(jnp.float32)
    kpos = jnp.arange(s, dtype=jnp.int32)
    scale = 1.0 / math.sqrt(DIFF_QK)

    def block(args):
        qb, bi = args
        qpos = bi * BLOCK + jnp.arange(BLOCK, dtype=jnp.int32)
        bias = table[:, t5_bucket(kpos[None, :] - qpos[:, None])]
        sc = jnp.einsum('bqhmd,bkhmd->bmhqk', qb, k).astype(jnp.float32) * scale + bias[None, None]
        p = jax.nn.softmax(sc, axis=-1)
        a = p[:, 0] - lam * p[:, 1]
        return jnp.einsum('bhqk,bkhd->bqhd', a.astype(v.dtype), v)

    o = from_blocks(lax.map(block, (to_blocks(q), jnp.arange(nb, dtype=jnp.int32))))
    o = rms_norm(o, subln) * (1.0 - lam_init)
    return o.reshape(b, s, W_DIFF)


def swa_attention(q, k, v, sink, rel_bias):
    b, s, _ = q.shape
    nb = s // BLOCK
    grp = SWA_HEADS // SWA_KV_HEADS
    q = q.reshape(b, nb, BLOCK, SWA_KV_HEADS, grp, SWA_DIM)
    pad = ((0, 0), (BLOCK, BLOCK), (0, 0), (0, 0))

    def windows(t):
        tb = jnp.pad(t.reshape(b, s, SWA_KV_HEADS, SWA_DIM), pad).reshape(
            b, nb + 2, BLOCK, SWA_KV_HEADS, SWA_DIM)
        return jnp.concatenate([tb[:, :-2], tb[:, 1:-1], tb[:, 2:]], axis=2)

    kw, vw = windows(k), windows(v)
    rel = (jnp.arange(3 * BLOCK, dtype=jnp.int32) - BLOCK)[None, :] - jnp.arange(BLOCK, dtype=jnp.int32)[:, None]
    table = rel_bias[:, DIFF_HEADS:].T.astype(jnp.float32)
    bias = table[:, t5_bucket(rel)].reshape(SWA_KV_HEADS, grp, BLOCK, 3 * BLOCK)
    kpos = (jnp.arange(nb, dtype=jnp.int32)[:, None] * BLOCK - BLOCK
            + jnp.arange(3 * BLOCK, dtype=jnp.int32)[None, :])
    mask = (jnp.abs(rel) <= WINDOW)[None] & ((kpos >= 0) & (kpos < s))[:, None, :]
    scale = 1.0 / math.sqrt(SWA_DIM)
    sc = jnp.einsum('bnqgrd,bnkgd->bngrqk', q, kw).astype(jnp.float32) * scale + bias[None, None]
    sc = jnp.where(mask[None, :, None, None], sc, -jnp.inf)
    sink_l = sink.astype(jnp.float32).reshape(SWA_KV_HEADS, grp)[None, None, :, :, None, None]
    m = jnp.maximum(jnp.max(sc, axis=-1, keepdims=True), sink_l)
    p = jnp.exp(sc - m)
    p = p / (jnp.sum(p, axis=-1, keepdims=True) + jnp.exp(sink_l - m))
    o = jnp.einsum('bngrqk,bnkgd->bnqgrd', p.astype(vw.dtype), vw)
    return o.reshape(b, s, W_SWA)


def hybrid_layer(x, w_in, q_norm, kv_norm, w_uq, w_ukv, lam_vecs, subln, sink,
                 rel_bias, w_out, ln_g, ln_b, pos, lam_init):
    h = x @ w_in
    split_points = np.cumsum(IN_SIZES)[:-1].tolist()
    c_q, c_kv, k_r, dq, dk, dv, sq, sk, sv, gate = jnp.split(h, split_points, axis=-1)
    o = jnp.concatenate([
        mla_attention(c_q, c_kv, k_r, q_norm, kv_norm, w_uq, w_ukv, pos),
        diff_attention(dq, dk, dv, lam_vecs, subln, rel_bias, lam_init),
        swa_attention(sq, sk, sv, sink, rel_bias),
    ], axis=-1)
    y = (o * jax.nn.silu(gate)) @ w_out
    return layer_norm(DEEPNORM_ALPHA * x + y, ln_g, ln_b)


def setup_inputs(seed: int = 0) -> dict:
    key = jax.random.key(seed)
    ks = jax.random.split(key, 14)
    f32 = jnp.float32
    nrm = lambda k, shape, sc: jax.random.normal(k, shape, f32) * sc
    return {
        "x": nrm(ks[0], (BATCH, SEQ, D_MODEL), 1.0),
        "w_in": nrm(ks[1], (DEPTH, D_MODEL, D_IN), D_MODEL ** -0.5),
        "mla_q_norm": 1.0 + nrm(ks[2], (DEPTH, MLA_Q_RANK), 0.02),
        "mla_kv_norm": 1.0 + nrm(ks[3], (DEPTH, MLA_KV_RANK), 0.02),
        "mla_w_uq": nrm(ks[4], (DEPTH, MLA_Q_RANK, MLA_HEADS * (MLA_NOPE + MLA_ROPE)), MLA_Q_RANK ** -0.5),
        "mla_w_ukv": nrm(ks[5], (DEPTH, MLA_KV_RANK, MLA_HEADS * (MLA_NOPE + MLA_V)), MLA_KV_RANK ** -0.5),
        "diff_lambda": nrm(ks[6], (DEPTH, 4, DIFF_QK), 0.1),
        "diff_subln": 1.0 + nrm(ks[7], (DEPTH, DIFF_V), 0.02),
        "swa_sink": nrm(ks[8], (DEPTH, SWA_HEADS), 0.5),
        "rel_bias": nrm(ks[9], (REL_BUCKETS, BIAS_HEADS), 0.3),
        "w_out": nrm(ks[10], (DEPTH, D_MIX, D_MODEL), DEEPNORM_BETA * D_MIX ** -0.5),
        "ln_g": 1.0 + nrm(ks[11], (DEPTH, D_MODEL), 0.02),
        "ln_b": nrm(ks[12], (DEPTH, D_MODEL), 0.02),
    }


def reference(x, w_in, mla_q_norm, mla_kv_norm, mla_w_uq, mla_w_ukv, diff_lambda,
              diff_subln, swa_sink, rel_bias, w_out, ln_g, ln_b):
    pos = jnp.arange(x.shape[1], dtype=jnp.int32)
    for l in range(DEPTH):
        lam_init = 0.8 - 0.6 * math.exp(-0.3 * l)
        x = hybrid_layer(x, w_in[l], mla_q_norm[l], mla_kv_norm[l], mla_w_uq[l], mla_w_ukv[l],
                         diff_lambda[l], diff_subln[l], swa_sink[l], rel_bias,
                         w_out[l], ln_g[l], ln_b[l], pos, lam_init)
    return x
```

```python
import functools
import math

import numpy as np
import jax
import jax.numpy as jnp
from jax import lax
from jax.experimental import pallas as pl
from jax.experimental.pallas import tpu as pltpu

D_MODEL = 1024
MLA_HEADS, MLA_Q_RANK, MLA_KV_RANK, MLA_NOPE, MLA_ROPE, MLA_V = 6, 192, 128, 64, 32, 64
DIFF_HEADS, DIFF_QK, DIFF_V = 4, 32, 64
SWA_HEADS, SWA_KV_HEADS, SWA_DIM, WINDOW = 6, 2, 64, 128
W_MLA, W_DIFF, W_SWA = MLA_HEADS * MLA_V, DIFF_HEADS * DIFF_V, SWA_HEADS * SWA_DIM
D_MIX = W_MLA + W_DIFF + W_SWA
IN_SIZES = (MLA_Q_RANK, MLA_KV_RANK, MLA_ROPE, DIFF_HEADS * 2 * DIFF_QK, DIFF_HEADS * 2 * DIFF_QK,
            W_DIFF, W_SWA, SWA_KV_HEADS * SWA_DIM, SWA_KV_HEADS * SWA_DIM, D_MIX)
REL_BUCKETS, REL_MAX_DIST = 32, 128
ROPE_THETA = 10000.0
RMS_EPS, LN_EPS = 1e-6, 1e-5

LOG2E = math.log2(math.e)
LANES = 128
ONES_ROWS = 16
VMEM_LIMIT = 56 * 1024 * 1024

F32, BF16 = jnp.float32, jnp.bfloat16
NEG_INF = float("-inf")


def _cparams(sem):
    return pltpu.CompilerParams(dimension_semantics=sem, vmem_limit_bytes=VMEM_LIMIT)


def _nt(a, b):
    return lax.dot_general(a, b, (((1,), (1,)), ((), ())), preferred_element_type=F32)


def _nn(a, b):
    return jnp.dot(a, b, preferred_element_type=F32)


def _rope_tables_kernel(fcol_ref, frow_ref, cos_t_ref, sin_t_ref, cos_p_ref, sin_p_ref, *, tile):
    p0 = pl.program_id(0) * tile
    pos_l = (p0 + lax.broadcasted_iota(jnp.int32, (MLA_ROPE // 2, tile), 1)).astype(F32)
    ang_t = pos_l * fcol_ref[...]
    cos_t_ref[...] = jnp.cos(ang_t)
    sin_t_ref[...] = jnp.sin(ang_t)
    pos_s = (p0 + lax.broadcasted_iota(jnp.int32, (tile, LANES), 0)).astype(F32)
    ang_p = pos_s * frow_ref[...]
    lane = lax.broadcasted_iota(jnp.int32, (tile, LANES), 1)
    live = (lane >= MLA_NOPE) & (lane < MLA_NOPE + MLA_ROPE)
    cos_p_ref[...] = jnp.where(live, jnp.cos(ang_p), 0.0)
    sin_p_ref[...] = jnp.where(live, jnp.sin(ang_p), 0.0)


def _rope_tables(seq):
    half = MLA_ROPE // 2
    freqs = (ROPE_THETA ** (-np.arange(half, dtype=np.float64) / half)).astype(np.float32)
    fcol = jnp.asarray(freqs.reshape(half, 1))
    frow_np = np.zeros((1, LANES), np.float32)
    frow_np[0, MLA_NOPE:MLA_NOPE + half] = freqs
    frow_np[0, MLA_NOPE + half:MLA_NOPE + MLA_ROPE] = freqs
    frow = jnp.asarray(frow_np)
    tile = min(seq, 1024)
    return pl.pallas_call(
        functools.partial(_rope_tables_kernel, tile=tile),
        grid=(seq // tile,),
        in_specs=[pl.BlockSpec((half, 1), lambda i: (0, 0)), pl.BlockSpec((1, LANES), lambda i: (0, 0))],
        out_specs=[pl.BlockSpec((half, tile), lambda i: (0, i)), pl.BlockSpec((half, tile), lambda i: (0, i)),
                   pl.BlockSpec((tile, LANES), lambda i: (i, 0)), pl.BlockSpec((tile, LANES), lambda i: (i, 0))],
        out_shape=[jax.ShapeDtypeStruct((half, seq), F32), jax.ShapeDtypeStruct((half, seq), F32),
                   jax.ShapeDtypeStruct((seq, LANES), F32), jax.ShapeDtypeStruct((seq, LANES), F32)],
        compiler_params=_cparams(("arbitrary",)),
        name="rope_tables",
    )(fcol, frow)


def _t5_bucket(rel):
    half = REL_BUCKETS // 2
    max_exact = half // 2
    n = jnp.abs(rel)
    large = jnp.full(rel.shape, max_exact, jnp.int32)
    for k in range(1, half - max_exact):
        thr = max_exact * (REL_MAX_DIST / max_exact) ** (k / (half - max_exact))
        thr_i = int(round(thr)) if abs(thr - round(thr)) < 1e-9 else int(math.ceil(thr))
        large = large + (n >= thr_i).astype(jnp.int32)
    return jnp.where(rel > 0, half, 0) + jnp.where(n < max_exact, n, large)


def _bias_lookup(tab_ref, bucket, head):
    out = jnp.zeros(bucket.shape, F32)
    for b in range(REL_BUCKETS):
        out = jnp.where(bucket == b, tab_ref[b, head], out)
    return out


def _diff_bias_kernel(tab_ref, out_ref, *, tile):
    h = pl.program_id(0)
    d = (pl.program_id(1) - 1) * tile
    ki = lax.broadcasted_iota(jnp.int32, (tile, tile), 0)
    qj = lax.broadcasted_iota(jnp.int32, (tile, tile), 1)
    out_ref[0, 0] = _bias_lookup(tab_ref, _t5_bucket(d + ki - qj), h) * LOG2E


def _diff_bias_tiles(rel_bias, tile):
    return pl.pallas_call(
        functools.partial(_diff_bias_kernel, tile=tile),
        grid=(DIFF_HEADS, 3),
        in_specs=[pl.BlockSpec(memory_space=pltpu.SMEM)],
        out_specs=pl.BlockSpec((1, 1, tile, tile), lambda h, d: (h, d, 0, 0)),
        out_shape=jax.ShapeDtypeStruct((DIFF_HEADS, 3, tile, tile), F32),
        compiler_params=_cparams(("arbitrary", "arbitrary")),
        name="diff_bias_tiles",
    )(rel_bias)


def _swa_bias_kernel(tab_ref, out_ref, *, tq):
    h = pl.program_id(0)
    wk = tq + 2 * WINDOW
    ki = lax.broadcasted_iota(jnp.int32, (wk, tq), 0)
    qj = lax.broadcasted_iota(jnp.int32, (wk, tq), 1)
    rel = ki - WINDOW - qj
    bias = _bias_lookup(tab_ref, _t5_bucket(rel), DIFF_HEADS + h) * LOG2E
    out_ref[0] = jnp.where(jnp.abs(rel) <= WINDOW, bias, NEG_INF)


def _swa_bias_tiles(rel_bias, tq):
    wk = tq + 2 * WINDOW
    return pl.pallas_call(
        functools.partial(_swa_bias_kernel, tq=tq),
        grid=(SWA_HEADS,),
        in_specs=[pl.BlockSpec(memory_space=pltpu.SMEM)],
        out_specs=pl.BlockSpec((1, wk, tq), lambda h: (h, 0, 0)),
        out_shape=jax.ShapeDtypeStruct((SWA_HEADS, wk, tq), F32),
        compiler_params=_cparams(("arbitrary",)),
        name="swa_bias_tiles",
    )(rel_bias)


_T_CQ, _T_CKV, _T_DQ, _T_DV, _T_SQ, _T_SV, _T_GATE = 0, 192, 320, 576, 832, 1216, 1344
_T_ROWS = 2368
_R_CKV, _R_KR, _R_KRROT, _R_DK, _R_SK = 0, 128, 256, 384, 896
_R_COLS = 1024


def _rms_t(v, g_col):
    ms = jnp.mean(v * v, axis=0, keepdims=True)
    return v * lax.rsqrt(ms + RMS_EPS) * g_col


def _in_proj_kernel(x_ref, wt_ref, wr_ref, wuq_ref, wkp_ref, wvt_ref, gq_ref, gkv_col_ref, gkv_row_ref,
                    cos_t_ref, sin_t_ref, cos_p_ref, sin_p_ref, ones_ref,
                    qm_ref, km_ref, vm_ref, qd_ref, kd_ref, vd_ref, sq_ref, sk_ref, sv_ref, g_ref):
    xb = x_ref[0].astype(BF16)

    def proj_t(lo, hi):
        return _nt(wt_ref[lo:hi, :], xb)

    cqn = _rms_t(proj_t(_T_CQ, _T_CKV), gq_ref[...]).astype(BF16)
    q_all = _nn(wuq_ref[...], cqn)
    cos_t, sin_t = cos_t_ref[...], sin_t_ref[...]
    q_scale = LOG2E / math.sqrt(MLA_NOPE + MLA_ROPE)
    half = MLA_ROPE // 2
    for h in range(MLA_HEADS):
        blk = q_all[h * LANES:(h + 1) * LANES]
        x1 = blk[MLA_NOPE:MLA_NOPE + half]
        x2 = blk[MLA_NOPE + half:MLA_NOPE + MLA_ROPE]
        roped = jnp.concatenate([blk[:MLA_NOPE], x1 * cos_t - x2 * sin_t, x2 * cos_t + x1 * sin_t,
                                 blk[MLA_NOPE + MLA_ROPE:]], axis=0)
        qm_ref[0, h] = (roped * q_scale).astype(BF16)

    ckvn_t = _rms_t(proj_t(_T_CKV, _T_DQ), gkv_col_ref[...]).astype(BF16)
    v_all = _nn(wvt_ref[...], ckvn_t).astype(BF16)
    for h in range(MLA_HEADS):
        vm_ref[0, h] = v_all[h * MLA_V:(h + 1) * MLA_V]

    r = _nn(xb, wr_ref[...])
    ckv = r[:, _R_CKV:_R_KR]
    ms = jnp.mean(ckv * ckv, axis=1, keepdims=True)
    ckvn = (ckv * lax.rsqrt(ms + RMS_EPS) * gkv_row_ref[...]).astype(BF16)
    k_nope = _nn(ckvn, wkp_ref[...])
    k_rope = r[:, _R_KR:_R_KRROT] * cos_p_ref[...] + r[:, _R_KRROT:_R_DK] * sin_p_ref[...]
    for h in range(MLA_HEADS):
        km_ref[0, h] = (k_nope[:, h * LANES:(h + 1) * LANES] + k_rope).astype(BF16)
    ones_row = ones_ref[...]
    for h in range(DIFF_HEADS):
        kd_ref[0, h] = (r[:, _R_DK + h * LANES:_R_DK + (h + 1) * LANES] + ones_row).astype(BF16)
    sk_ref[0] = r[:, _R_SK:_R_COLS].astype(BF16)

    qd_ref[0] = (proj_t(_T_DQ, _T_DV) * (LOG2E / math.sqrt(DIFF_QK))).astype(BF16)
    vd_ref[0] = proj_t(_T_DV, _T_SQ).astype(BF16)
    sq_ref[0] = (proj_t(_T_SQ, _T_SV) * (LOG2E / math.sqrt(SWA_DIM))).astype(BF16)
    sv_ref[0] = proj_t(_T_SV, _T_GATE).astype(BF16)
    gate = proj_t(_T_GATE, _T_ROWS)
    g_ref[0] = (gate * jax.nn.sigmoid(gate)).astype(BF16)


def _in_proj(x, lw, tabs, tm):
    b, s, d = x.shape
    cos_t, sin_t, cos_p, sin_p = tabs
    full = lambda a: pl.BlockSpec(a.shape, lambda bi, i: (0,) * a.ndim)
    ones_np = np.zeros((1, LANES), np.float32)
    ones_np[0, 2 * DIFF_QK:2 * DIFF_QK + 2] = 1.0
    ones_row = jnp.asarray(ones_np)
    half = MLA_ROPE // 2
    ins = [x, lw["w_t"], lw["w_r"], lw["w_uq_t"], lw["w_kp"], lw["w_v_t"], lw["gq_col"], lw["gkv_col"],
           lw["gkv_row"], cos_t, sin_t, cos_p, sin_p, ones_row]
    in_specs = [pl.BlockSpec((1, tm, d), lambda bi, i: (bi, i, 0))]
    in_specs += [full(a) for a in ins[1:9]]
    in_specs += [pl.BlockSpec((half, tm), lambda bi, i: (0, i)), pl.BlockSpec((half, tm), lambda bi, i: (0, i)),
                 pl.BlockSpec((tm, LANES), lambda bi, i: (i, 0)), pl.BlockSpec((tm, LANES), lambda bi, i: (i, 0)),
                 full(ones_row)]
    sds = jax.ShapeDtypeStruct
    out_shape = [
        sds((b, MLA_HEADS, LANES, s), BF16),
        sds((b, MLA_HEADS, s, LANES), BF16),
        sds((b, MLA_HEADS, MLA_V, s), BF16),
        sds((b, DIFF_HEADS * 2 * DIFF_QK, s), BF16),
        sds((b, DIFF_HEADS, s, LANES), BF16),
        sds((b, W_DIFF, s), BF16),
        sds((b, W_SWA, s), BF16),
        sds((b, s, LANES), BF16),
        sds((b, SWA_KV_HEADS * SWA_DIM, s), BF16),
        sds((b, D_MIX, s), BF16),
    ]
    out_specs = [
        pl.BlockSpec((1, MLA_HEADS, LANES, tm), lambda bi, i: (bi, 0, 0, i)),
        pl.BlockSpec((1, MLA_HEADS, tm, LANES), lambda bi, i: (bi, 0, i, 0)),
        pl.BlockSpec((1, MLA_HEADS, MLA_V, tm), lambda bi, i: (bi, 0, 0, i)),
        pl.BlockSpec((1, DIFF_HEADS * 2 * DIFF_QK, tm), lambda bi, i: (bi, 0, i)),
        pl.BlockSpec((1, DIFF_HEADS, tm, LANES), lambda bi, i: (bi, 0, i, 0)),
        pl.BlockSpec((1, W_DIFF, tm), lambda bi, i: (bi, 0, i)),
        pl.BlockSpec((1, W_SWA, tm), lambda bi, i: (bi, 0, i)),
        pl.BlockSpec((1, tm, LANES), lambda bi, i: (bi, i, 0)),
        pl.BlockSpec((1, SWA_KV_HEADS * SWA_DIM, tm), lambda bi, i: (bi, 0, i)),
        pl.BlockSpec((1, D_MIX, tm), lambda bi, i: (bi, 0, i)),
    ]
    return pl.pallas_call(
        _in_proj_kernel,
        grid=(b, s // tm),
        in_specs=in_specs,
        out_specs=out_specs,
        out_shape=out_shape,
        compiler_params=_cparams(("parallel", "parallel")),
        name="in_proj",
    )(*ins)


def _with_ones(v_t):
    tk = v_t.shape[1]
    row = lax.broadcasted_iota(jnp.int32, (ONES_ROWS, tk), 0)
    return jnp.concatenate([v_t, jnp.where(row == 0, 1.0, 0.0).astype(v_t.dtype)], axis=0)


def _online_step(s_t, v_ext, m, acc):
    m_new = jnp.maximum(m, jnp.max(s_t, axis=0, keepdims=True))
    p = jnp.exp2(s_t - m_new).astype(BF16)
    alpha = jnp.exp2(m - m_new)
    return m_new, alpha * acc + _nn(v_ext, p)


def _mla_attn_kernel(q_ref, k_ref, v_ref, o_ref, *, tk, nk):
    q_t = q_ref[0, 0]
    tq = q_t.shape[1]

    def body(j, carry):
        m, acc = carry
        off = pl.multiple_of(j * tk, tk)
        s_t = _nn(k_ref[0, 0, pl.ds(off, tk), :], q_t)
        return _online_step(s_t, _with_ones(v_ref[0, 0, :, pl.ds(off, tk)]), m, acc)

    m0 = jnp.full((1, tq), NEG_INF, F32)
    acc0 = jnp.zeros((MLA_V + ONES_ROWS, tq), F32)
    _, acc = lax.fori_loop(0, nk, body, (m0, acc0))
    o_ref[0] = (acc[:MLA_V] / acc[MLA_V:MLA_V + 1]).astype(o_ref.dtype)


def _mla_attention(qm, km, vm, tq, tk):
    b, h, _, s = qm.shape
    return pl.pallas_call(
        functools.partial(_mla_attn_kernel, tk=tk, nk=s // tk),
        grid=(b, h, s // tq),
        in_specs=[pl.BlockSpec((1, 1, LANES, tq), lambda bi, hi, qi: (bi, hi, 0, qi)),
                  pl.BlockSpec((1, 1, s, LANES), lambda bi, hi, qi: (bi, hi, 0, 0)),
                  pl.BlockSpec((1, 1, MLA_V, s), lambda bi, hi, qi: (bi, hi, 0, 0))],
        out_specs=pl.BlockSpec((1, MLA_V, tq), lambda bi, hi, qi: (bi, hi, qi)),
        out_shape=jax.ShapeDtypeStruct((b, W_MLA, s), BF16),
        compiler_params=_cparams(("parallel", "parallel", "arbitrary")),
        name="mla_attention",
    )(qm, km, vm)


def _diff_attn_kernel(far_ref, lam_ref, q_ref, k_ref, v_ref, bias_ref, subln_ref, o_ref, *, t, nk, lam_init):
    h = pl.program_id(1)
    qi = pl.program_id(2)
    q_t = q_ref[0]
    r16 = lax.broadcasted_iota(jnp.int32, (ONES_ROWS, t), 0)
    zero_map = jnp.zeros((DIFF_QK, t), BF16)
    zero_pad = jnp.zeros((LANES - 2 * DIFF_QK - ONES_ROWS, t), BF16)

    def q_aug(mp, side):
        own = q_t[mp * DIFF_QK:(mp + 1) * DIFF_QK]
        own = [own, zero_map] if mp == 0 else [zero_map, own]
        if side is None:
            aug = jnp.zeros((ONES_ROWS, t), BF16)
        else:
            c = jnp.full((ONES_ROWS, t), far_ref[side, h], F32)
            c_hi = c.astype(BF16).astype(F32)
            aug = jnp.where(r16 == 0, c_hi, jnp.where(r16 == 1, c - c_hi, 0.0)).astype(BF16)
        return jnp.concatenate(own + [aug, zero_pad], axis=0)

    def tile_step(j, carry, qs, bias):
        off = pl.multiple_of(j * t, t)
        k = k_ref[0, 0, pl.ds(off, t), :]
        v_ext = _with_ones(v_ref[0, :, pl.ds(off, t)])
        out = []
        for mp in range(2):
            s_t = _nn(k, qs[mp])
            if bias is not None:
                s_t = s_t + bias
            out.extend(_online_step(s_t, v_ext, carry[2 * mp], carry[2 * mp + 1]))
        return tuple(out)

    m0 = jnp.full((1, t), NEG_INF, F32)
    acc0 = jnp.zeros((DIFF_V + ONES_ROWS, t), F32)
    carry = (m0, acc0, m0, acc0)
    q_left = [q_aug(0, 0), q_aug(1, 0)]
    carry = lax.fori_loop(0, jnp.maximum(qi - 1, 0), lambda j, c: tile_step(j, c, q_left, None), carry)
    q_near = [q_aug(0, None), q_aug(1, None)]
    near_lo = jnp.maximum(qi - 1, 0)
    near_hi = jnp.minimum(qi + 2, nk)
    carry = lax.fori_loop(near_lo, near_hi,
                          lambda j, c: tile_step(j, c, q_near, bias_ref[0, j - qi + 1]), carry)
    q_right = [q_aug(0, 1), q_aug(1, 1)]
    carry = lax.fori_loop(near_hi, nk, lambda j, c: tile_step(j, c, q_right, None), carry)

    _, a0, _, a1 = carry
    lam = lam_ref[0, 0]
    o = a0[:DIFF_V] / a0[DIFF_V:DIFF_V + 1] - lam * (a1[:DIFF_V] / a1[DIFF_V:DIFF_V + 1])
    ms = jnp.mean(o * o, axis=0, keepdims=True)
    o = o * lax.rsqrt(ms + RMS_EPS) * subln_ref[...] * (1.0 - lam_init)
    o_ref[0] = o.astype(o_ref.dtype)


def _diff_attention(qd, kd, vd, bias_tiles, far, lam, subln_col, t, lam_init):
    b, _, s = qd.shape
    smem = pl.BlockSpec(memory_space=pltpu.SMEM)
    return pl.pallas_call(
        functools.partial(_diff_attn_kernel, t=t, nk=s // t, lam_init=lam_init),
        grid=(b, DIFF_HEADS, s // t),
        in_specs=[smem, smem,
                  pl.BlockSpec((1, 2 * DIFF_QK, t), lambda bi, hi, qi: (bi, hi, qi)),
                  pl.BlockSpec((1, 1, s, LANES), lambda bi, hi, qi: (bi, hi, 0, 0)),
                  pl.BlockSpec((1, DIFF_V, s), lambda bi, hi, qi: (bi, hi, 0)),
                  pl.BlockSpec((1, 3, t, t), lambda bi, hi, qi: (hi, 0, 0, 0)),
                  pl.BlockSpec((DIFF_V, 1), lambda bi, hi, qi: (0, 0))],
        out_specs=pl.BlockSpec((1, DIFF_V, t), lambda bi, hi, qi: (bi, hi, qi)),
        out_shape=jax.ShapeDtypeStruct((b, W_DIFF, s), BF16),
        compiler_params=_cparams(("parallel", "parallel", "arbitrary")),
        name="diff_attention",
    )(far, lam, qd, kd, vd, bias_tiles, subln_col)


def _swa_attn_kernel(sink_ref, q_ref, k0_ref, k1_ref, k2_ref, k3_ref, v0_ref, v1_ref, v2_ref, v3_ref,
                     bias_ref, o_ref, *, tq, seq):
    q0 = pl.program_id(1) * tq
    k_win = jnp.concatenate([k0_ref[0], k1_ref[0], k2_ref[0], k3_ref[0]], axis=0)
    v_win = jnp.concatenate([v0_ref[0], v1_ref[0], v2_ref[0], v3_ref[0]], axis=1)
    wk = tq + 2 * WINDOW
    kpos = q0 - WINDOW + lax.broadcasted_iota(jnp.int32, (wk, tq), 0)
    valid = (kpos >= 0) & (kpos < seq)
    grp = SWA_HEADS // SWA_KV_HEADS
    zeros = jnp.zeros((SWA_DIM, tq), BF16)
    for hq in range(SWA_HEADS):
        g = hq // grp
        q_t = q_ref[0, hq * SWA_DIM:(hq + 1) * SWA_DIM, :]
        q_pad = jnp.concatenate([q_t, zeros] if g == 0 else [zeros, q_t], axis=0)
        s_t = jnp.where(valid, _nn(k_win, q_pad) + bias_ref[hq], NEG_INF)
        sink = sink_ref[hq] * LOG2E
        m = jnp.maximum(jnp.max(s_t, axis=0, keepdims=True), sink)
        p = jnp.exp2(s_t - m)
        den = jnp.sum(p, axis=0, keepdims=True) + jnp.exp2(sink - m)
        o = _nn(v_win[g * SWA_DIM:(g + 1) * SWA_DIM], p.astype(BF16)) / den
        o_ref[0, hq * SWA_DIM:(hq + 1) * SWA_DIM, :] = o.astype(o_ref.dtype)


def _swa_attention(sq, sk, sv, bias_tiles, sink, tq):
    b, _, s = sq.shape
    assert tq % WINDOW == 0
    r = tq // WINDOW
    nblk = s // WINDOW
    smem = pl.BlockSpec(memory_space=pltpu.SMEM)
    wk = tq + 2 * WINDOW
    nw = wk // WINDOW

    def kspec(c):
        return pl.BlockSpec((1, WINDOW, LANES),
                            lambda bi, qi: (bi, jnp.clip(qi * r - 1 + c, 0, nblk - 1), 0))

    def vspec(c):
        return pl.BlockSpec((1, SWA_KV_HEADS * SWA_DIM, WINDOW),
                            lambda bi, qi: (bi, 0, jnp.clip(qi * r - 1 + c, 0, nblk - 1)))

    assert nw == 4
    return pl.pallas_call(
        functools.partial(_swa_attn_kernel, tq=tq, seq=s),
        grid=(b, s // tq),
        in_specs=[smem, pl.BlockSpec((1, W_SWA, tq), lambda bi, qi: (bi, 0, qi))]
        + [kspec(c) for c in range(nw)] + [vspec(c) for c in range(nw)]
        + [pl.BlockSpec((SWA_HEADS, wk, tq), lambda bi, qi: (0, 0, 0))],
        out_specs=pl.BlockSpec((1, W_SWA, tq), lambda bi, qi: (bi, 0, qi)),
        out_shape=jax.ShapeDtypeStruct((b, W_SWA, s), BF16),
        compiler_params=_cparams(("parallel", "parallel")),
        name="swa_attention",
    )(sink, sq, sk, sk, sk, sk, sv, sv, sv, sv, bias_tiles)


def _out_proj_kernel(x_ref, om_ref, od_ref, os_ref, g_ref, w_ref, lng_ref, lnb_ref, o_ref, *, alpha):
    o_t = jnp.concatenate([om_ref[0], od_ref[0], os_ref[0]], axis=0)
    og_t = o_t * g_ref[0]
    y = lax.dot_general(og_t, w_ref[...], (((0,), (0,)), ((), ())), preferred_element_type=F32)
    z = alpha * x_ref[0] + y
    mu = jnp.mean(z, axis=1, keepdims=True)
    zc = z - mu
    var = jnp.mean(zc * zc, axis=1, keepdims=True)
    o_ref[0] = zc * lax.rsqrt(var + LN_EPS) * lng_ref[...] + lnb_ref[...]


def _out_proj(x, om, od, osw, g, w_out, ln_g, ln_b, tm, alpha):
    b, s, d = x.shape
    tspec = lambda rows: pl.BlockSpec((1, rows, tm), lambda bi, i: (bi, 0, i))
    return pl.pallas_call(
        functools.partial(_out_proj_kernel, alpha=alpha),
        grid=(b, s // tm),
        in_specs=[pl.BlockSpec((1, tm, d), lambda bi, i: (bi, i, 0)),
                  tspec(W_MLA), tspec(W_DIFF), tspec(W_SWA), tspec(D_MIX),
                  pl.BlockSpec((D_MIX, d), lambda bi, i: (0, 0)),
                  pl.BlockSpec((1, d), lambda bi, i: (0, 0)), pl.BlockSpec((1, d), lambda bi, i: (0, 0))],
        out_specs=pl.BlockSpec((1, tm, d), lambda bi, i: (bi, i, 0)),
        out_shape=jax.ShapeDtypeStruct((b, s, d), F32),
        compiler_params=_cparams(("parallel", "parallel")),
        name="out_proj",
    )(x, om, od, osw, g, w_out, ln_g, ln_b)


def _layer_weights(w_in, q_norm, kv_norm, w_uq, w_ukv):
    cq, ckv, kr, dq, dk, dv, sq, sk, sv, gate = jnp.split(w_in, np.cumsum(IN_SIZES)[:-1].tolist(), axis=1)
    d = w_in.shape[0]
    half = MLA_ROPE // 2
    w_t = jnp.concatenate([cq, ckv, dq, dv, sq, sv, gate], axis=1).T.astype(BF16)
    pad = lambda a, lo, width: jnp.pad(a, ((0, 0), (lo, width - lo - a.shape[1])))
    kr_p = pad(kr, MLA_NOPE, LANES)
    kr_rot_p = pad(jnp.concatenate([-kr[:, half:], kr[:, :half]], axis=1), MLA_NOPE, LANES)
    dk_p = jnp.concatenate([pad(dk[:, h * 2 * DIFF_QK:(h + 1) * 2 * DIFF_QK], 0, LANES)
                            for h in range(DIFF_HEADS)], axis=1)
    w_r = jnp.concatenate([ckv, kr_p, kr_rot_p, dk_p, sk], axis=1).astype(BF16)
    assert w_t.shape == (_T_ROWS, d) and w_r.shape == (d, _R_COLS)
    qk = MLA_NOPE + MLA_ROPE
    w_uq_t = jnp.concatenate([jnp.pad(w_uq[:, h * qk:(h + 1) * qk].T, ((0, LANES - qk), (0, 0)))
                              for h in range(MLA_HEADS)], axis=0).astype(BF16)
    kv = MLA_NOPE + MLA_V
    w_kp = jnp.concatenate([pad(w_ukv[:, h * kv:h * kv + MLA_NOPE], 0, LANES)
                            for h in range(MLA_HEADS)], axis=1).astype(BF16)
    w_v_t = jnp.concatenate([w_ukv[:, h * kv + MLA_NOPE:(h + 1) * kv].T
                             for h in range(MLA_HEADS)], axis=0).astype(BF16)
    return dict(w_t=w_t, w_r=w_r, w_uq_t=w_uq_t, w_kp=w_kp, w_v_t=w_v_t,
                gq_col=q_norm.reshape(-1, 1), gkv_col=kv_norm.reshape(-1, 1), gkv_row=kv_norm.reshape(1, -1))


def _lambda_kernel(lv_ref, o_ref, *, lam_init):
    lv = lv_ref[...]
    a = jnp.sum(lv[0:1] * lv[1:2], axis=1, keepdims=True)
    c = jnp.sum(lv[2:3] * lv[3:4], axis=1, keepdims=True)
    o_ref[...] = jnp.exp(a) - jnp.exp(c) + lam_init


def _diff_lambda(lam_vecs, lam_init):
    return pl.pallas_call(
        functools.partial(_lambda_kernel, lam_init=lam_init),
        out_shape=jax.ShapeDtypeStruct((1, 1), F32),
        name="diff_lambda",
    )(lam_vecs)


def _tiles(seq):
    t = min(256, seq)
    return dict(tm=min(512, seq), mla_tq=t, mla_tk=t, diff_t=t, swa_tq=256)


def kernel(x, w_in, mla_q_norm, mla_kv_norm, mla_w_uq, mla_w_ukv, diff_lambda, diff_subln, swa_sink,
           rel_bias, w_out, ln_g, ln_b):
    depth = w_in.shape[0]
    seq = x.shape[1]
    cfg = _tiles(seq)
    alpha = (2 * depth) ** 0.25
    tabs = _rope_tables(seq)
    diff_bias = _diff_bias_tiles(rel_bias, cfg["diff_t"])
    swa_bias = _swa_bias_tiles(rel_bias, cfg["swa_tq"])
    half = REL_BUCKETS // 2
    far = jnp.stack([rel_bias[half - 1, :DIFF_HEADS], rel_bias[2 * half - 1, :DIFF_HEADS]]) * LOG2E
    for l in range(depth):
        lam_init = 0.8 - 0.6 * math.exp(-0.3 * l)
        lw = _layer_weights(w_in[l], mla_q_norm[l], mla_kv_norm[l], mla_w_uq[l], mla_w_ukv[l])
        qm, km, vm, qd, kd, vd, sq, sk, sv, g = _in_proj(x, lw, tabs, cfg["tm"])
        om = _mla_attention(qm, km, vm, cfg["mla_tq"], cfg["mla_tk"])
        lam = _diff_lambda(diff_lambda[l], lam_init)
        od = _diff_attention(qd, kd, vd, diff_bias, far, lam, diff_subln[l].reshape(-1, 1), cfg["diff_t"],
                             lam_init)
        osw = _swa_attention(sq, sk, sv, swa_bias, swa_sink[l], cfg["swa_tq"])
        x = _out_proj(x, om, od, osw, g, w_out[l].astype(BF16), ln_g[l].reshape(1, -1), ln_b[l].reshape(1, -1),
                      cfg["tm"], alpha)
    return x
```

```python
import functools
import math

import numpy as np
import jax
import jax.numpy as jnp
from jax import lax
from jax.experimental import pallas as pl
from jax.experimental.pallas import tpu as pltpu

D_MODEL = 1024
MLA_HEADS, MLA_Q_RANK, MLA_KV_RANK, MLA_NOPE, MLA_ROPE, MLA_V = 6, 192, 128, 64, 32, 64
DIFF_HEADS, DIFF_QK, DIFF_V = 4, 32, 64
SWA_HEADS, SWA_KV_HEADS, SWA_DIM, WINDOW = 6, 2, 64, 128
W_MLA, W_DIFF, W_SWA = MLA_HEADS * MLA_V, DIFF_HEADS * DIFF_V, SWA_HEADS * SWA_DIM
D_MIX = W_MLA + W_DIFF + W_SWA
IN_SIZES = (MLA_Q_RANK, MLA_KV_RANK, MLA_ROPE, DIFF_HEADS * 2 * DIFF_QK, DIFF_HEADS * 2 * DIFF_QK,
            W_DIFF, W_SWA, SWA_KV_HEADS * SWA_DIM, SWA_KV_HEADS * SWA_DIM, D_MIX)
REL_BUCKETS, REL_MAX_DIST = 32, 128
ROPE_THETA = 10000.0
RMS_EPS, LN_EPS = 1e-6, 1e-5

LOG2E = math.log2(math.e)
LANES = 128
ONES_ROWS = 16
VMEM_LIMIT = 56 * 1024 * 1024

F32, BF16 = jnp.float32, jnp.bfloat16
NEG_INF = float("-inf")


def _cparams(sem):
    return pltpu.CompilerParams(dimension_semantics=sem, vmem_limit_bytes=VMEM_LIMIT)


def _nt(a, b):
    return lax.dot_general(a, b, (((1,), (1,)), ((), ())), preferred_element_type=F32)


def _nn(a, b):
    return jnp.dot(a, b, preferred_element_type=F32)


def _rope_tables_kernel(fcol_ref, frow_ref, cos_t_ref, sin_t_ref, cos_p_ref, sin_p_ref, *, tile):
    p0 = pl.program_id(0) * tile
    pos_l = (p0 + lax.broadcasted_iota(jnp.int32, (MLA_ROPE // 2, tile), 1)).astype(F32)
    ang_t = pos_l * fcol_ref[...]
    cos_t_ref[...] = jnp.cos(ang_t)
    sin_t_ref[...] = jnp.sin(ang_t)
    pos_s = (p0 + lax.broadcasted_iota(jnp.int32, (tile, LANES), 0)).astype(F32)
    ang_p = pos_s * frow_ref[...]
    lane = lax.broadcasted_iota(jnp.int32, (tile, LANES), 1)
    live = (lane >= MLA_NOPE) & (lane < MLA_NOPE + MLA_ROPE)
    cos_p_ref[...] = jnp.where(live, jnp.cos(ang_p), 0.0)
    sin_p_ref[...] = jnp.where(live, jnp.sin(ang_p), 0.0)


def _rope_tables(seq):
    half = MLA_ROPE // 2
    freqs = (ROPE_THETA ** (-np.arange(half, dtype=np.float64) / half)).astype(np.float32)
    fcol = jnp.asarray(freqs.reshape(half, 1))
    frow_np = np.zeros((1, LANES), np.float32)
    frow_np[0, MLA_NOPE:MLA_NOPE + half] = freqs
    frow_np[0, MLA_NOPE + half:MLA_NOPE + MLA_ROPE] = freqs
    frow = jnp.asarray(frow_np)
    tile = min(seq, 1024)
    return pl.pallas_call(
        functools.partial(_rope_tables_kernel, tile=tile),
        grid=(seq // tile,),
        in_specs=[pl.BlockSpec((half, 1), lambda i: (0, 0)), pl.BlockSpec((1, LANES), lambda i: (0, 0))],
        out_specs=[pl.BlockSpec((half, tile), lambda i: (0, i)), pl.BlockSpec((half, tile), lambda i: (0, i)),
                   pl.BlockSpec((tile, LANES), lambda i: (i, 0)), pl.BlockSpec((tile, LANES), lambda i: (i, 0))],
        out_shape=[jax.ShapeDtypeStruct((half, seq), F32), jax.ShapeDtypeStruct((half, seq), F32),
                   jax.ShapeDtypeStruct((seq, LANES), F32), jax.ShapeDtypeStruct((seq, LANES), F32)],
        compiler_params=_cparams(("arbitrary",)),
        name="rope_tables",
    )(fcol, frow)


def _t5_bucket(rel):
    half = REL_BUCKETS // 2
    max_exact = half // 2
    n = jnp.abs(rel)
    large = jnp.full(rel.shape, max_exact, jnp.int32)
    for k in range(1, half - max_exact):
        thr = max_exact * (REL_MAX_DIST / max_exact) ** (k / (half - max_exact))
        thr_i = int(round(thr)) if abs(thr - round(thr)) < 1e-9 else int(math.ceil(thr))
        large = large + (n >= thr_i).astype(jnp.int32)
    return jnp.where(rel > 0, half, 0) + jnp.where(n < max_exact, n, large)


def _bias_lookup(tab_ref, bucket, head):
    out = jnp.zeros(bucket.shape, F32)
    for b in range(REL_BUCKETS):
        out = jnp.where(bucket == b, tab_ref[b, head], out)
    return out


def _diff_bias_kernel(tab_ref, out_ref, *, tq, tk, e_min):
    h = pl.program_id(0)
    d = (pl.program_id(1) + e_min) * tk
    ki = lax.broadcasted_iota(jnp.int32, (tk, tq), 0)
    qj = lax.broadcasted_iota(jnp.int32, (tk, tq), 1)
    out_ref[0, 0] = _bias_lookup(tab_ref, _t5_bucket(d + ki - qj), h) * LOG2E


def _diff_bias_tiles(rel_bias, tq, tk, e_min, e_max):
    n_e = e_max - e_min + 1
    return pl.pallas_call(
        functools.partial(_diff_bias_kernel, tq=tq, tk=tk, e_min=e_min),
        grid=(DIFF_HEADS, n_e),
        in_specs=[pl.BlockSpec(memory_space=pltpu.SMEM)],
        out_specs=pl.BlockSpec((1, 1, tk, tq), lambda h, d: (h, d, 0, 0)),
        out_shape=jax.ShapeDtypeStruct((DIFF_HEADS, n_e, tk, tq), F32),
        compiler_params=_cparams(("arbitrary", "arbitrary")),
        name="diff_bias_tiles",
    )(rel_bias)


def _swa_bias_kernel(tab_ref, out_ref, *, tq):
    h = pl.program_id(0)
    wk = tq + 2 * WINDOW
    ki = lax.broadcasted_iota(jnp.int32, (wk, tq), 0)
    qj = lax.broadcasted_iota(jnp.int32, (wk, tq), 1)
    rel = ki - WINDOW - qj
    bias = _bias_lookup(tab_ref, _t5_bucket(rel), DIFF_HEADS + h) * LOG2E
    out_ref[0] = jnp.where(jnp.abs(rel) <= WINDOW, bias, NEG_INF)


def _swa_bias_tiles(rel_bias, tq):
    wk = tq + 2 * WINDOW
    return pl.pallas_call(
        functools.partial(_swa_bias_kernel, tq=tq),
        grid=(SWA_HEADS,),
        in_specs=[pl.BlockSpec(memory_space=pltpu.SMEM)],
        out_specs=pl.BlockSpec((1, wk, tq), lambda h: (h, 0, 0)),
        out_shape=jax.ShapeDtypeStruct((SWA_HEADS, wk, tq), F32),
        compiler_params=_cparams(("arbitrary",)),
        name="swa_bias_tiles",
    )(rel_bias)


_T_CQ, _T_CKV, _T_DQ, _T_DV, _T_SQ, _T_SV, _T_GATE = 0, 192, 320, 576, 832, 1216, 1344
_T_ROWS = 2368
_R_CKV, _R_KR, _R_KRROT, _R_DK, _R_SK = 0, 128, 256, 384, 896
_R_COLS = 1024


def _rms_t(v, g_col):
    ms = jnp.mean(v * v, axis=0, keepdims=True)
    return v * lax.rsqrt(ms + RMS_EPS) * g_col


def _in_proj_kernel(x_ref, wt_ref, wr_ref, wuq_ref, wkp_ref, wvt_ref, gq_ref, gkv_col_ref, gkv_row_ref,
                    cos_t_ref, sin_t_ref, cos_p_ref, sin_p_ref, ones_ref,
                    qm_ref, km_ref, vm_ref, qd_ref, kd_ref, vd_ref, sq_ref, sk_ref, sv_ref, g_ref):
    xb = x_ref[0].astype(BF16)

    def proj_t(lo, hi):
        return _nt(wt_ref[lo:hi, :], xb)

    cqn = _rms_t(proj_t(_T_CQ, _T_CKV), gq_ref[...]).astype(BF16)
    q_all = _nn(wuq_ref[...], cqn)
    cos_t, sin_t = cos_t_ref[...], sin_t_ref[...]
    q_scale = LOG2E / math.sqrt(MLA_NOPE + MLA_ROPE)
    half = MLA_ROPE // 2
    for h in range(MLA_HEADS):
        blk = q_all[h * LANES:(h + 1) * LANES]
        x1 = blk[MLA_NOPE:MLA_NOPE + half]
        x2 = blk[MLA_NOPE + half:MLA_NOPE + MLA_ROPE]
        roped = jnp.concatenate([blk[:MLA_NOPE], x1 * cos_t - x2 * sin_t, x2 * cos_t + x1 * sin_t,
                                 blk[MLA_NOPE + MLA_ROPE:]], axis=0)
        qm_ref[0, h] = (roped * q_scale).astype(BF16)

    ckvn_t = _rms_t(proj_t(_T_CKV, _T_DQ), gkv_col_ref[...]).astype(BF16)
    v_all = _nn(wvt_ref[...], ckvn_t).astype(BF16)
    for h in range(MLA_HEADS):
        vm_ref[0, h] = v_all[h * MLA_V:(h + 1) * MLA_V]

    r = _nn(xb, wr_ref[...])
    ckv = r[:, _R_CKV:_R_KR]
    ms = jnp.mean(ckv * ckv, axis=1, keepdims=True)
    ckvn = (ckv * lax.rsqrt(ms + RMS_EPS) * gkv_row_ref[...]).astype(BF16)
    k_nope = _nn(ckvn, wkp_ref[...])
    k_rope = r[:, _R_KR:_R_KRROT] * cos_p_ref[...] + r[:, _R_KRROT:_R_DK] * sin_p_ref[...]
    for h in range(MLA_HEADS):
        km_ref[0, h] = (k_nope[:, h * LANES:(h + 1) * LANES] + k_rope).astype(BF16)
    ones_row = ones_ref[...]
    for h in range(DIFF_HEADS):
        kd_ref[0, h] = (r[:, _R_DK + h * LANES:_R_DK + (h + 1) * LANES] + ones_row).astype(BF16)
    sk_ref[0] = r[:, _R_SK:_R_COLS].astype(BF16)

    qd_ref[0] = (proj_t(_T_DQ, _T_DV) * (LOG2E / math.sqrt(DIFF_QK))).astype(BF16)
    vd_ref[0] = proj_t(_T_DV, _T_SQ).astype(BF16)
    sq_ref[0] = (proj_t(_T_SQ, _T_SV) * (LOG2E / math.sqrt(SWA_DIM))).astype(BF16)
    sv_ref[0] = proj_t(_T_SV, _T_GATE).astype(BF16)
    gate = proj_t(_T_GATE, _T_ROWS)
    g_ref[0] = (gate * jax.nn.sigmoid(gate)).astype(BF16)


def _in_proj(x, lw, tabs, tm):
    b, s, d = x.shape
    cos_t, sin_t, cos_p, sin_p = tabs
    full = lambda a: pl.BlockSpec(a.shape, lambda bi, i: (0,) * a.ndim)
    ones_np = np.zeros((1, LANES), np.float32)
    ones_np[0, 2 * DIFF_QK:2 * DIFF_QK + 2] = 1.0
    ones_row = jnp.asarray(ones_np)
    half = MLA_ROPE // 2
    ins = [x, lw["w_t"], lw["w_r"], lw["w_uq_t"], lw["w_kp"], lw["w_v_t"], lw["gq_col"], lw["gkv_col"],
           lw["gkv_row"], cos_t, sin_t, cos_p, sin_p, ones_row]
    in_specs = [pl.BlockSpec((1, tm, d), lambda bi, i: (bi, i, 0))]
    in_specs += [full(a) for a in ins[1:9]]
    in_specs += [pl.BlockSpec((half, tm), lambda bi, i: (0, i)), pl.BlockSpec((half, tm), lambda bi, i: (0, i)),
                 pl.BlockSpec((tm, LANES), lambda bi, i: (i, 0)), pl.BlockSpec((tm, LANES), lambda bi, i: (i, 0)),
                 full(ones_row)]
    sds = jax.ShapeDtypeStruct
    out_shape = [
        sds((b, MLA_HEADS, LANES, s), BF16),
        sds((b, MLA_HEADS, s, LANES), BF16),
        sds((b, MLA_HEADS, MLA_V, s), BF16),
        sds((b, DIFF_HEADS * 2 * DIFF_QK, s), BF16),
        sds((b, DIFF_HEADS, s, LANES), BF16),
        sds((b, W_DIFF, s), BF16),
        sds((b, W_SWA, s), BF16),
        sds((b, s, LANES), BF16),
        sds((b, SWA_KV_HEADS * SWA_DIM, s), BF16),
        sds((b, D_MIX, s), BF16),
    ]
    out_specs = [
        pl.BlockSpec((1, MLA_HEADS, LANES, tm), lambda bi, i: (bi, 0, 0, i)),
        pl.BlockSpec((1, MLA_HEADS, tm, LANES), lambda bi, i: (bi, 0, i, 0)),
        pl.BlockSpec((1, MLA_HEADS, MLA_V, tm), lambda bi, i: (bi, 0, 0, i)),
        pl.BlockSpec((1, DIFF_HEADS * 2 * DIFF_QK, tm), lambda bi, i: (bi, 0, i)),
        pl.BlockSpec((1, DIFF_HEADS, tm, LANES), lambda bi, i: (bi, 0, i, 0)),
        pl.BlockSpec((1, W_DIFF, tm), lambda bi, i: (bi, 0, i)),
        pl.BlockSpec((1, W_SWA, tm), lambda bi, i: (bi, 0, i)),
        pl.BlockSpec((1, tm, LANES), lambda bi, i: (bi, i, 0)),
        pl.BlockSpec((1, SWA_KV_HEADS * SWA_DIM, tm), lambda bi, i: (bi, 0, i)),
        pl.BlockSpec((1, D_MIX, tm), lambda bi, i: (bi, 0, i)),
    ]
    return pl.pallas_call(
        _in_proj_kernel,
        grid=(b, s // tm),
        in_specs=in_specs,
        out_specs=out_specs,
        out_shape=out_shape,
        compiler_params=_cparams(("parallel", "parallel")),
        name="in_proj",
    )(*ins)


def _with_ones(v_t):
    tk = v_t.shape[1]
    row = lax.broadcasted_iota(jnp.int32, (ONES_ROWS, tk), 0)
    return jnp.concatenate([v_t, jnp.where(row == 0, 1.0, 0.0).astype(v_t.dtype)], axis=0)


def _online_step(s_t, tile_max, v_ext, m, acc):
    m_new = jnp.maximum(m, tile_max)
    p = jnp.exp2(s_t - m_new).astype(BF16)
    alpha = jnp.exp2(m - m_new)
    return m_new, alpha * acc + _nn(v_ext, p)


def _tile_off(j, t):
    return j * t if isinstance(j, int) else pl.multiple_of(j * t, t)


PIPE_TILES = 2


def _pipelined_tiles(n_tiles, produce, consume, carry, peel_first=False):
    u_n = PIPE_TILES
    trip = 2 * u_n
    assert n_tiles % trip == 0 and n_tiles >= trip

    def stage(slot, v0, c):
        for u in range(u_n):
            c = consume(slot, u, v0 + u, c)
        return c

    def fill(slot, v0):
        for u in range(u_n):
            produce(slot, u, v0 + u)

    def body(v0, c, last):
        fill(1, v0 + u_n)
        c = stage(0, v0, c)
        if not last:
            fill(0, v0 + trip)
        return stage(1, v0 + u_n, c)

    fill(0, 0)
    n_trips = n_tiles // trip
    first = 0
    if peel_first and n_trips > 1:
        carry = body(0, carry, False)
        first = 1
    carry = lax.fori_loop(first, n_trips - 1, lambda i, c: body(i * trip, c, False), carry)
    return body((n_trips - 1) * trip, carry, True)


def _mla_attn_kernel(q_ref, k_ref, v_ref, o_ref, s_buf, mx_buf, *, tk, nk):
    q_t = q_ref[0, 0]
    tq = q_t.shape[1]

    def produce(slot, u, j):
        off = _tile_off(j, tk)
        s_t = _nn(k_ref[0, 0, pl.ds(off, tk), :], q_t)
        s_buf[slot, u] = s_t
        mx_buf[slot, u] = jnp.max(s_t, axis=0, keepdims=True)

    def consume(slot, u, j, carry):
        off = _tile_off(j, tk)
        return _online_step(s_buf[slot, u], mx_buf[slot, u], _with_ones(v_ref[0, 0, :, pl.ds(off, tk)]), *carry)

    m0 = jnp.full((1, tq), NEG_INF, F32)
    acc0 = jnp.zeros((MLA_V + ONES_ROWS, tq), F32)
    _, acc = _pipelined_tiles(nk, produce, consume, (m0, acc0))
    o_ref[0] = (acc[:MLA_V] / acc[MLA_V:MLA_V + 1]).astype(o_ref.dtype)


def _mla_attention(qm, km, vm, tq, tk):
    b, h, _, s = qm.shape
    return pl.pallas_call(
        functools.partial(_mla_attn_kernel, tk=tk, nk=s // tk),
        grid=(b, h, s // tq),
        in_specs=[pl.BlockSpec((1, 1, LANES, tq), lambda bi, hi, qi: (bi, hi, 0, qi)),
                  pl.BlockSpec((1, 1, s, LANES), lambda bi, hi, qi: (bi, hi, 0, 0)),
                  pl.BlockSpec((1, 1, MLA_V, s), lambda bi, hi, qi: (bi, hi, 0, 0))],
        out_specs=pl.BlockSpec((1, MLA_V, tq), lambda bi, hi, qi: (bi, hi, qi)),
        out_shape=jax.ShapeDtypeStruct((b, W_MLA, s), BF16),
        scratch_shapes=[pltpu.VMEM((2, PIPE_TILES, tk, tq), F32), pltpu.VMEM((2, PIPE_TILES, 1, tq), F32)],
        compiler_params=_cparams(("parallel", "parallel", "arbitrary")),
        name="mla_attention",
    )(qm, km, vm)


NEAR_TILES = 2 * PIPE_TILES


def _diff_near_offsets(r):
    return min(-1, r - NEAR_TILES), NEAR_TILES - 1


def _diff_attn_kernel(far_ref, lam_ref, q_ref, k_ref, v_ref, bias_ref, subln_ref, o_ref,
                      s_buf, mx_buf, qfar_buf, *, tq, tk, nk, lam_init):
    h = pl.program_id(1)
    qi = pl.program_id(2)
    t = tq
    r = tq // tk
    q_t = q_ref[0]
    r16 = lax.broadcasted_iota(jnp.int32, (ONES_ROWS, t), 0)
    zero_map = jnp.zeros((DIFF_QK, t), BF16)
    zero_pad = jnp.zeros((LANES - 2 * DIFF_QK - ONES_ROWS, t), BF16)

    def q_aug(mp, side):
        own = q_t[mp * DIFF_QK:(mp + 1) * DIFF_QK]
        own = [own, zero_map] if mp == 0 else [zero_map, own]
        if side is None:
            aug = jnp.zeros((ONES_ROWS, t), BF16)
        else:
            c = jnp.full((ONES_ROWS, t), far_ref[side, h], F32)
            c_hi = c.astype(BF16).astype(F32)
            aug = jnp.where(r16 == 0, c_hi, jnp.where(r16 == 1, c - c_hi, 0.0)).astype(BF16)
        return jnp.concatenate(own + [aug, zero_pad], axis=0)

    near_lo = jnp.clip(r * qi - 1, 0, nk - NEAR_TILES)
    e_min, _ = _diff_near_offsets(r)
    q_near = [q_aug(0, None), q_aug(1, None)]
    for side in range(2):
        for mp in range(2):
            qfar_buf[side, mp] = q_aug(mp, side)

    def key_tile(v):
        if isinstance(v, int) and v < NEAR_TILES:
            return near_lo + v, True, None
        idx = v - NEAR_TILES
        side = (idx >= near_lo).astype(jnp.int32)
        return idx + NEAR_TILES * side, False, side

    def produce(slot, u, v):
        j, near, side = key_tile(v)
        k = k_ref[0, 0, pl.ds(_tile_off(j, tk), tk), :]
        for mp in range(2):
            if near:
                s_t = _nn(k, q_near[mp]) + bias_ref[0, j - r * qi - e_min]
            else:
                s_t = _nn(k, qfar_buf[side, mp])
            s_buf[slot, u, mp] = s_t
            mx_buf[slot, u, mp] = jnp.max(s_t, axis=0, keepdims=True)

    def consume(slot, u, v, carry):
        j, _, _ = key_tile(v)
        v_ext = _with_ones(v_ref[0, :, pl.ds(_tile_off(j, tk), tk)])
        out = []
        for mp in range(2):
            out.extend(_online_step(s_buf[slot, u, mp], mx_buf[slot, u, mp], v_ext,
                                    carry[2 * mp], carry[2 * mp + 1]))
        return tuple(out)

    m0 = jnp.full((1, t), NEG_INF, F32)
    acc0 = jnp.zeros((DIFF_V + ONES_ROWS, t), F32)
    carry = _pipelined_tiles(nk, produce, consume, (m0, acc0, m0, acc0), peel_first=True)

    _, a0, _, a1 = carry
    lam = lam_ref[0, 0]
    o = a0[:DIFF_V] / a0[DIFF_V:DIFF_V + 1] - lam * (a1[:DIFF_V] / a1[DIFF_V:DIFF_V + 1])
    ms = jnp.mean(o * o, axis=0, keepdims=True)
    o = o * lax.rsqrt(ms + RMS_EPS) * subln_ref[...] * (1.0 - lam_init)
    o_ref[0] = o.astype(o_ref.dtype)


def _diff_attention(qd, kd, vd, bias_tiles, far, lam, subln_col, tq, tk, lam_init):
    b, _, s = qd.shape
    n_e = bias_tiles.shape[1]
    assert tq % tk == 0 and tq // tk + 2 <= NEAR_TILES and tk >= REL_MAX_DIST
    smem = pl.BlockSpec(memory_space=pltpu.SMEM)
    return pl.pallas_call(
        functools.partial(_diff_attn_kernel, tq=tq, tk=tk, nk=s // tk, lam_init=lam_init),
        grid=(b, DIFF_HEADS, s // tq),
        in_specs=[smem, smem,
                  pl.BlockSpec((1, 2 * DIFF_QK, tq), lambda bi, hi, qi: (bi, hi, qi)),
                  pl.BlockSpec((1, 1, s, LANES), lambda bi, hi, qi: (bi, hi, 0, 0)),
                  pl.BlockSpec((1, DIFF_V, s), lambda bi, hi, qi: (bi, hi, 0)),
                  pl.BlockSpec((1, n_e, tk, tq), lambda bi, hi, qi: (hi, 0, 0, 0)),
                  pl.BlockSpec((DIFF_V, 1), lambda bi, hi, qi: (0, 0))],
        out_specs=pl.BlockSpec((1, DIFF_V, tq), lambda bi, hi, qi: (bi, hi, qi)),
        out_shape=jax.ShapeDtypeStruct((b, W_DIFF, s), BF16),
        scratch_shapes=[pltpu.VMEM((2, PIPE_TILES, 2, tk, tq), F32), pltpu.VMEM((2, PIPE_TILES, 2, 1, tq), F32),
                        pltpu.VMEM((2, 2, LANES, tq), BF16)],
        compiler_params=_cparams(("parallel", "parallel", "arbitrary")),
        name="diff_attention",
    )(far, lam, qd, kd, vd, bias_tiles, subln_col)


def _swa_attn_kernel(sink_ref, q_ref, k0_ref, k1_ref, k2_ref, k3_ref, v0_ref, v1_ref, v2_ref, v3_ref,
                     bias_ref, o_ref, *, tq, seq):
    q0 = pl.program_id(1) * tq
    k_win = jnp.concatenate([k0_ref[0], k1_ref[0], k2_ref[0], k3_ref[0]], axis=0)
    v_win = jnp.concatenate([v0_ref[0], v1_ref[0], v2_ref[0], v3_ref[0]], axis=1)
    wk = tq + 2 * WINDOW
    kpos = q0 - WINDOW + lax.broadcasted_iota(jnp.int32, (wk, tq), 0)
    valid = (kpos >= 0) & (kpos < seq)
    grp = SWA_HEADS // SWA_KV_HEADS
    zeros = jnp.zeros((SWA_DIM, tq), BF16)
    for hq in range(SWA_HEADS):
        g = hq // grp
        q_t = q_ref[0, hq * SWA_DIM:(hq + 1) * SWA_DIM, :]
        q_pad = jnp.concatenate([q_t, zeros] if g == 0 else [zeros, q_t], axis=0)
        s_t = jnp.where(valid, _nn(k_win, q_pad) + bias_ref[hq], NEG_INF)
        sink = sink_ref[hq] * LOG2E
        m = jnp.maximum(jnp.max(s_t, axis=0, keepdims=True), sink)
        p = jnp.exp2(s_t - m)
        den = jnp.sum(p, axis=0, keepdims=True) + jnp.exp2(sink - m)
        o = _nn(v_win[g * SWA_DIM:(g + 1) * SWA_DIM], p.astype(BF16)) / den
        o_ref[0, hq * SWA_DIM:(hq + 1) * SWA_DIM, :] = o.astype(o_ref.dtype)


def _swa_attention(sq, sk, sv, bias_tiles, sink, tq):
    b, _, s = sq.shape
    assert tq % WINDOW == 0
    r = tq // WINDOW
    nblk = s // WINDOW
    smem = pl.BlockSpec(memory_space=pltpu.SMEM)
    wk = tq + 2 * WINDOW
    nw = wk // WINDOW

    def kspec(c):
        return pl.BlockSpec((1, WINDOW, LANES),
                            lambda bi, qi: (bi, jnp.clip(qi * r - 1 + c, 0, nblk - 1), 0))

    def vspec(c):
        return pl.BlockSpec((1, SWA_KV_HEADS * SWA_DIM, WINDOW),
                            lambda bi, qi: (bi, 0, jnp.clip(qi * r - 1 + c, 0, nblk - 1)))

    assert nw == 4
    return pl.pallas_call(
        functools.partial(_swa_attn_kernel, tq=tq, seq=s),
        grid=(b, s // tq),
        in_specs=[smem, pl.BlockSpec((1, W_SWA, tq), lambda bi, qi: (bi, 0, qi))]
        + [kspec(c) for c in range(nw)] + [vspec(c) for c in range(nw)]
        + [pl.BlockSpec((SWA_HEADS, wk, tq), lambda bi, qi: (0, 0, 0))],
        out_specs=pl.BlockSpec((1, W_SWA, tq), lambda bi, qi: (bi, 0, qi)),
        out_shape=jax.ShapeDtypeStruct((b, W_SWA, s), BF16),
        compiler_params=_cparams(("parallel", "parallel")),
        name="swa_attention",
    )(sink, sq, sk, sk, sk, sk, sv, sv, sv, sv, bias_tiles)


def _out_proj_kernel(x_ref, om_ref, od_ref, os_ref, g_ref, w_ref, lng_ref, lnb_ref, o_ref, *, alpha):
    o_t = jnp.concatenate([om_ref[0], od_ref[0], os_ref[0]], axis=0)
    og_t = o_t * g_ref[0]
    y = lax.dot_general(og_t, w_ref[...], (((0,), (0,)), ((), ())), preferred_element_type=F32)
    z = alpha * x_ref[0] + y
    mu = jnp.mean(z, axis=1, keepdims=True)
    zc = z - mu
    var = jnp.mean(zc * zc, axis=1, keepdims=True)
    o_ref[0] = zc * lax.rsqrt(var + LN_EPS) * lng_ref[...] + lnb_ref[...]


def _out_proj(x, om, od, osw, g, w_out, ln_g, ln_b, tm, alpha):
    b, s, d = x.shape
    tspec = lambda rows: pl.BlockSpec((1, rows, tm), lambda bi, i: (bi, 0, i))
    return pl.pallas_call(
        functools.partial(_out_proj_kernel, alpha=alpha),
        grid=(b, s // tm),
        in_specs=[pl.BlockSpec((1, tm, d), lambda bi, i: (bi, i, 0)),
                  tspec(W_MLA), tspec(W_DIFF), tspec(W_SWA), tspec(D_MIX),
                  pl.BlockSpec((D_MIX, d), lambda bi, i: (0, 0)),
                  pl.BlockSpec((1, d), lambda bi, i: (0, 0)), pl.BlockSpec((1, d), lambda bi, i: (0, 0))],
        out_specs=pl.BlockSpec((1, tm, d), lambda bi, i: (bi, i, 0)),
        out_shape=jax.ShapeDtypeStruct((b, s, d), F32),
        compiler_params=_cparams(("parallel", "parallel")),
        name="out_proj",
    )(x, om, od, osw, g, w_out, ln_g, ln_b)


def _layer_weights(w_in, q_norm, kv_norm, w_uq, w_ukv):
    cq, ckv, kr, dq, dk, dv, sq, sk, sv, gate = jnp.split(w_in, np.cumsum(IN_SIZES)[:-1].tolist(), axis=1)
    d = w_in.shape[0]
    half = MLA_ROPE // 2
    w_t = jnp.concatenate([cq, ckv, dq, dv, sq, sv, gate], axis=1).T.astype(BF16)
    pad = lambda a, lo, width: jnp.pad(a, ((0, 0), (lo, width - lo - a.shape[1])))
    kr_p = pad(kr, MLA_NOPE, LANES)
    kr_rot_p = pad(jnp.concatenate([-kr[:, half:], kr[:, :half]], axis=1), MLA_NOPE, LANES)
    dk_p = jnp.concatenate([pad(dk[:, h * 2 * DIFF_QK:(h + 1) * 2 * DIFF_QK], 0, LANES)
                            for h in range(DIFF_HEADS)], axis=1)
    w_r = jnp.concatenate([ckv, kr_p, kr_rot_p, dk_p, sk], axis=1).astype(BF16)
    assert w_t.shape == (_T_ROWS, d) and w_r.shape == (d, _R_COLS)
    qk = MLA_NOPE + MLA_ROPE
    w_uq_t = jnp.concatenate([jnp.pad(w_uq[:, h * qk:(h + 1) * qk].T, ((0, LANES - qk), (0, 0)))
                              for h in range(MLA_HEADS)], axis=0).astype(BF16)
    kv = MLA_NOPE + MLA_V
    w_kp = jnp.concatenate([pad(w_ukv[:, h * kv:h * kv + MLA_NOPE], 0, LANES)
                            for h in range(MLA_HEADS)], axis=1).astype(BF16)
    w_v_t = jnp.concatenate([w_ukv[:, h * kv + MLA_NOPE:(h + 1) * kv].T
                             for h in range(MLA_HEADS)], axis=0).astype(BF16)
    return dict(w_t=w_t, w_r=w_r, w_uq_t=w_uq_t, w_kp=w_kp, w_v_t=w_v_t,
                gq_col=q_norm.reshape(-1, 1), gkv_col=kv_norm.reshape(-1, 1), gkv_row=kv_norm.reshape(1, -1))


def _lambda_kernel(lv_ref, o_ref, *, lam_init):
    lv = lv_ref[...]
    a = jnp.sum(lv[0:1] * lv[1:2], axis=1, keepdims=True)
    c = jnp.sum(lv[2:3] * lv[3:4], axis=1, keepdims=True)
    o_ref[...] = jnp.exp(a) - jnp.exp(c) + lam_init


def _diff_lambda(lam_vecs, lam_init):
    return pl.pallas_call(
        functools.partial(_lambda_kernel, lam_init=lam_init),
        out_shape=jax.ShapeDtypeStruct((1, 1), F32),
        name="diff_lambda",
    )(lam_vecs)


def _tiles(seq):
    return dict(tm=min(512, seq), mla_tq=min(1024, seq), mla_tk=256, diff_tq=min(512, seq), diff_tk=256, swa_tq=256)


def kernel(x, w_in, mla_q_norm, mla_kv_norm, mla_w_uq, mla_w_ukv, diff_lambda, diff_subln, swa_sink,
           rel_bias, w_out, ln_g, ln_b):
    depth = w_in.shape[0]
    seq = x.shape[1]
    cfg = _tiles(seq)
    alpha = (2 * depth) ** 0.25
    tabs = _rope_tables(seq)
    e_min, e_max = _diff_near_offsets(cfg["diff_tq"] // cfg["diff_tk"])
    diff_bias = _diff_bias_tiles(rel_bias, cfg["diff_tq"], cfg["diff_tk"], e_min, e_max)
    swa_bias = _swa_bias_tiles(rel_bias, cfg["swa_tq"])
    half = REL_BUCKETS // 2
    far = jnp.stack([rel_bias[half - 1, :DIFF_HEADS], rel_bias[2 * half - 1, :DIFF_HEADS]]) * LOG2E
    for l in range(depth):
        lam_init = 0.8 - 0.6 * math.exp(-0.3 * l)
        lw = _layer_weights(w_in[l], mla_q_norm[l], mla_kv_norm[l], mla_w_uq[l], mla_w_ukv[l])
        qm, km, vm, qd, kd, vd, sq, sk, sv, g = _in_proj(x, lw, tabs, cfg["tm"])
        om = _mla_attention(qm, km, vm, cfg["mla_tq"], cfg["mla_tk"])
        lam = _diff_lambda(diff_lambda[l], lam_init)
        od = _diff_attention(qd, kd, vd, diff_bias, far, lam, diff_subln[l].reshape(-1, 1), cfg["diff_tq"],
                             cfg["diff_tk"], lam_init)
        osw = _swa_attention(sq, sk, sv, swa_bias, swa_sink[l], cfg["swa_tq"])
        x = _out_proj(x, om, od, osw, g, w_out[l].astype(BF16), ln_g[l].reshape(1, -1), ln_b[l].reshape(1, -1),
                      cfg["tm"], alpha)
    return x
```

```python
import functools
import math

import numpy as np
import jax
import jax.numpy as jnp
from jax import lax
from jax.experimental import pallas as pl
from jax.experimental.pallas import tpu as pltpu

D_MODEL = 1024
MLA_HEADS, MLA_Q_RANK, MLA_KV_RANK, MLA_NOPE, MLA_ROPE, MLA_V = 6, 192, 128, 64, 32, 64
DIFF_HEADS, DIFF_QK, DIFF_V = 4, 32, 64
SWA_HEADS, SWA_KV_HEADS, SWA_DIM, WINDOW = 6, 2, 64, 128
W_MLA, W_DIFF, W_SWA = MLA_HEADS * MLA_V, DIFF_HEADS * DIFF_V, SWA_HEADS * SWA_DIM
D_MIX = W_MLA + W_DIFF + W_SWA
IN_SIZES = (MLA_Q_RANK, MLA_KV_RANK, MLA_ROPE, DIFF_HEADS * 2 * DIFF_QK, DIFF_HEADS * 2 * DIFF_QK,
            W_DIFF, W_SWA, SWA_KV_HEADS * SWA_DIM, SWA_KV_HEADS * SWA_DIM, D_MIX)
REL_BUCKETS, REL_MAX_DIST = 32, 128
ROPE_THETA = 10000.0
RMS_EPS, LN_EPS = 1e-6, 1e-5

LOG2E = math.log2(math.e)
LANES = 128
ONES_ROWS = 16
VMEM_LIMIT = 56 * 1024 * 1024

F32, BF16 = jnp.float32, jnp.bfloat16
NEG_INF = float("-inf")


def _cparams(sem):
    return pltpu.CompilerParams(dimension_semantics=sem, vmem_limit_bytes=VMEM_LIMIT)


def _nt(a, b):
    return lax.dot_general(a, b, (((1,), (1,)), ((), ())), preferred_element_type=F32)


def _nn(a, b):
    return jnp.dot(a, b, preferred_element_type=F32)


def _rope_tables_kernel(fcol_ref, frow_ref, cos_t_ref, sin_t_ref, cos_p_ref, sin_p_ref, *, tile):
    p0 = pl.program_id(0) * tile
    pos_l = (p0 + lax.broadcasted_iota(jnp.int32, (MLA_ROPE // 2, tile), 1)).astype(F32)
    ang_t = pos_l * fcol_ref[...]
    cos_t_ref[...] = jnp.cos(ang_t)
    sin_t_ref[...] = jnp.sin(ang_t)
    pos_s = (p0 + lax.broadcasted_iota(jnp.int32, (tile, LANES), 0)).astype(F32)
    ang_p = pos_s * frow_ref[...]
    lane = lax.broadcasted_iota(jnp.int32, (tile, LANES), 1)
    live = (lane >= MLA_NOPE) & (lane < MLA_NOPE + MLA_ROPE)
    cos_p_ref[...] = jnp.where(live, jnp.cos(ang_p), 0.0)
    sin_p_ref[...] = jnp.where(live, jnp.sin(ang_p), 0.0)


def _rope_tables(seq):
    half = MLA_ROPE // 2
    freqs = (ROPE_THETA ** (-np.arange(half, dtype=np.float64) / half)).astype(np.float32)
    fcol = jnp.asarray(freqs.reshape(half, 1))
    frow_np = np.zeros((1, LANES), np.float32)
    frow_np[0, MLA_NOPE:MLA_NOPE + half] = freqs
    frow_np[0, MLA_NOPE + half:MLA_NOPE + MLA_ROPE] = freqs
    frow = jnp.asarray(frow_np)
    tile = min(seq, 1024)
    return pl.pallas_call(
        functools.partial(_rope_tables_kernel, tile=tile),
        grid=(seq // tile,),
        in_specs=[pl.BlockSpec((half, 1), lambda i: (0, 0)), pl.BlockSpec((1, LANES), lambda i: (0, 0))],
        out_specs=[pl.BlockSpec((half, tile), lambda i: (0, i)), pl.BlockSpec((half, tile), lambda i: (0, i)),
                   pl.BlockSpec((tile, LANES), lambda i: (i, 0)), pl.BlockSpec((tile, LANES), lambda i: (i, 0))],
        out_shape=[jax.ShapeDtypeStruct((half, seq), F32), jax.ShapeDtypeStruct((half, seq), F32),
                   jax.ShapeDtypeStruct((seq, LANES), F32), jax.ShapeDtypeStruct((seq, LANES), F32)],
        compiler_params=_cparams(("arbitrary",)),
        name="rope_tables",
    )(fcol, frow)


def _t5_bucket(rel):
    half = REL_BUCKETS // 2
    max_exact = half // 2
    n = jnp.abs(rel)
    large = jnp.full(rel.shape, max_exact, jnp.int32)
    for k in range(1, half - max_exact):
        thr = max_exact * (REL_MAX_DIST / max_exact) ** (k / (half - max_exact))
        thr_i = int(round(thr)) if abs(thr - round(thr)) < 1e-9 else int(math.ceil(thr))
        large = large + (n >= thr_i).astype(jnp.int32)
    return jnp.where(rel > 0, half, 0) + jnp.where(n < max_exact, n, large)


def _bias_lookup(tab_ref, bucket, head):
    out = jnp.zeros(bucket.shape, F32)
    for b in range(REL_BUCKETS):
        out = jnp.where(bucket == b, tab_ref[b, head], out)
    return out


def _diff_bias_kernel(tab_ref, out_ref, *, tq, tk, e_min):
    h = pl.program_id(0)
    d = (pl.program_id(1) + e_min) * tk
    ki = lax.broadcasted_iota(jnp.int32, (tk, tq), 0)
    qj = lax.broadcasted_iota(jnp.int32, (tk, tq), 1)
    out_ref[0, 0] = _bias_lookup(tab_ref, _t5_bucket(d + ki - qj), h) * LOG2E


def _diff_bias_tiles(rel_bias, tq, tk, e_min, e_max):
    n_e = e_max - e_min + 1
    return pl.pallas_call(
        functools.partial(_diff_bias_kernel, tq=tq, tk=tk, e_min=e_min),
        grid=(DIFF_HEADS, n_e),
        in_specs=[pl.BlockSpec(memory_space=pltpu.SMEM)],
        out_specs=pl.BlockSpec((1, 1, tk, tq), lambda h, d: (h, d, 0, 0)),
        out_shape=jax.ShapeDtypeStruct((DIFF_HEADS, n_e, tk, tq), F32),
        compiler_params=_cparams(("arbitrary", "arbitrary")),
        name="diff_bias_tiles",
    )(rel_bias)


def _swa_bias_kernel(tab_ref, out_ref, *, tq):
    h = pl.program_id(0)
    wk = tq + 2 * WINDOW
    ki = lax.broadcasted_iota(jnp.int32, (wk, tq), 0)
    qj = lax.broadcasted_iota(jnp.int32, (wk, tq), 1)
    rel = ki - WINDOW - qj
    bias = _bias_lookup(tab_ref, _t5_bucket(rel), DIFF_HEADS + h) * LOG2E
    out_ref[0] = jnp.where(jnp.abs(rel) <= WINDOW, bias, NEG_INF)


def _swa_bias_tiles(rel_bias, tq):
    wk = tq + 2 * WINDOW
    return pl.pallas_call(
        functools.partial(_swa_bias_kernel, tq=tq),
        grid=(SWA_HEADS,),
        in_specs=[pl.BlockSpec(memory_space=pltpu.SMEM)],
        out_specs=pl.BlockSpec((1, wk, tq), lambda h: (h, 0, 0)),
        out_shape=jax.ShapeDtypeStruct((SWA_HEADS, wk, tq), F32),
        compiler_params=_cparams(("arbitrary",)),
        name="swa_bias_tiles",
    )(rel_bias)


_T_CQ, _T_CKV, _T_DQ, _T_DV, _T_SQ, _T_SV, _T_GATE = 0, 192, 320, 576, 832, 1216, 1344
_T_ROWS = 2368
_R_CKV, _R_KR, _R_KRROT, _R_DK, _R_SK = 0, 128, 256, 384, 896
_R_COLS = 1024


def _rms_t(v, g_col):
    ms = jnp.mean(v * v, axis=0, keepdims=True)
    return v * lax.rsqrt(ms + RMS_EPS) * g_col


def _in_proj_kernel(x_ref, wt_ref, wr_ref, wuq_ref, wkp_ref, wvt_ref, gq_ref, gkv_col_ref, gkv_row_ref,
                    cos_t_ref, sin_t_ref, cos_p_ref, sin_p_ref, ones_ref,
                    qm_ref, km_ref, vm_ref, qd_ref, kd_ref, vd_ref, sq_ref, sk_ref, sv_ref, g_ref):
    xb = x_ref[0].astype(BF16)

    def proj_t(lo, hi):
        return _nt(wt_ref[lo:hi, :], xb)

    cqn = _rms_t(proj_t(_T_CQ, _T_CKV), gq_ref[...]).astype(BF16)
    q_all = _nn(wuq_ref[...], cqn)
    cos_t, sin_t = cos_t_ref[...], sin_t_ref[...]
    q_scale = LOG2E / math.sqrt(MLA_NOPE + MLA_ROPE)
    half = MLA_ROPE // 2
    for h in range(MLA_HEADS):
        blk = q_all[h * LANES:(h + 1) * LANES]
        x1 = blk[MLA_NOPE:MLA_NOPE + half]
        x2 = blk[MLA_NOPE + half:MLA_NOPE + MLA_ROPE]
        roped = jnp.concatenate([blk[:MLA_NOPE], x1 * cos_t - x2 * sin_t, x2 * cos_t + x1 * sin_t,
                                 blk[MLA_NOPE + MLA_ROPE:]], axis=0)
        qm_ref[0, h] = (roped * q_scale).astype(BF16)

    ckvn_t = _rms_t(proj_t(_T_CKV, _T_DQ), gkv_col_ref[...]).astype(BF16)
    v_all = _nn(wvt_ref[...], ckvn_t).astype(BF16)
    for h in range(MLA_HEADS):
        vm_ref[0, h] = v_all[h * MLA_V:(h + 1) * MLA_V]

    r = _nn(xb, wr_ref[...])
    ckv = r[:, _R_CKV:_R_KR]
    ms = jnp.mean(ckv * ckv, axis=1, keepdims=True)
    ckvn = (ckv * lax.rsqrt(ms + RMS_EPS) * gkv_row_ref[...]).astype(BF16)
    k_nope = _nn(ckvn, wkp_ref[...])
    k_rope = r[:, _R_KR:_R_KRROT] * cos_p_ref[...] + r[:, _R_KRROT:_R_DK] * sin_p_ref[...]
    for h in range(MLA_HEADS):
        km_ref[0, h] = (k_nope[:, h * LANES:(h + 1) * LANES] + k_rope).astype(BF16)
    ones_row = ones_ref[...]
    for h in range(DIFF_HEADS):
        kd_ref[0, h] = (r[:, _R_DK + h * LANES:_R_DK + (h + 1) * LANES] + ones_row).astype(BF16)
    sk_ref[0] = r[:, _R_SK:_R_COLS].astype(BF16)

    qd_ref[0] = (proj_t(_T_DQ, _T_DV) * (LOG2E / math.sqrt(DIFF_QK))).astype(BF16)
    vd_ref[0] = proj_t(_T_DV, _T_SQ).astype(BF16)
    sq_ref[0] = (proj_t(_T_SQ, _T_SV) * (LOG2E / math.sqrt(SWA_DIM))).astype(BF16)
    sv_ref[0] = proj_t(_T_SV, _T_GATE).astype(BF16)
    gate = proj_t(_T_GATE, _T_ROWS)
    g_ref[0] = (gate * jax.nn.sigmoid(gate)).astype(BF16)


def _in_proj(x, lw, tabs, tm):
    b, s, d = x.shape
    cos_t, sin_t, cos_p, sin_p = tabs
    full = lambda a: pl.BlockSpec(a.shape, lambda bi, i: (0,) * a.ndim)
    ones_np = np.zeros((1, LANES), np.float32)
    ones_np[0, 2 * DIFF_QK:2 * DIFF_QK + 2] = 1.0
    ones_row = jnp.asarray(ones_np)
    half = MLA_ROPE // 2
    ins = [x, lw["w_t"], lw["w_r"], lw["w_uq_t"], lw["w_kp"], lw["w_v_t"], lw["gq_col"], lw["gkv_col"],
           lw["gkv_row"], cos_t, sin_t, cos_p, sin_p, ones_row]
    in_specs = [pl.BlockSpec((1, tm, d), lambda bi, i: (bi, i, 0))]
    in_specs += [full(a) for a in ins[1:9]]
    in_specs += [pl.BlockSpec((half, tm), lambda bi, i: (0, i)), pl.BlockSpec((half, tm), lambda bi, i: (0, i)),
                 pl.BlockSpec((tm, LANES), lambda bi, i: (i, 0)), pl.BlockSpec((tm, LANES), lambda bi, i: (i, 0)),
                 full(ones_row)]
    sds = jax.ShapeDtypeStruct
    out_shape = [
        sds((b, MLA_HEADS, LANES, s), BF16),
        sds((b, MLA_HEADS, s, LANES), BF16),
        sds((b, MLA_HEADS, MLA_V, s), BF16),
        sds((b, DIFF_HEADS * 2 * DIFF_QK, s), BF16),
        sds((b, DIFF_HEADS, s, LANES), BF16),
        sds((b, W_DIFF, s), BF16),
        sds((b, W_SWA, s), BF16),
        sds((b, s, LANES), BF16),
        sds((b, SWA_KV_HEADS * SWA_DIM, s), BF16),
        sds((b, D_MIX, s), BF16),
    ]
    out_specs = [
        pl.BlockSpec((1, MLA_HEADS, LANES, tm), lambda bi, i: (bi, 0, 0, i)),
        pl.BlockSpec((1, MLA_HEADS, tm, LANES), lambda bi, i: (bi, 0, i, 0)),
        pl.BlockSpec((1, MLA_HEADS, MLA_V, tm), lambda bi, i: (bi, 0, 0, i)),
        pl.BlockSpec((1, DIFF_HEADS * 2 * DIFF_QK, tm), lambda bi, i: (bi, 0, i)),
        pl.BlockSpec((1, DIFF_HEADS, tm, LANES), lambda bi, i: (bi, 0, i, 0)),
        pl.BlockSpec((1, W_DIFF, tm), lambda bi, i: (bi, 0, i)),
        pl.BlockSpec((1, W_SWA, tm), lambda bi, i: (bi, 0, i)),
        pl.BlockSpec((1, tm, LANES), lambda bi, i: (bi, i, 0)),
        pl.BlockSpec((1, SWA_KV_HEADS * SWA_DIM, tm), lambda bi, i: (bi, 0, i)),
        pl.BlockSpec((1, D_MIX, tm), lambda bi, i: (bi, 0, i)),
    ]
    return pl.pallas_call(
        _in_proj_kernel,
        grid=(b, s // tm),
        in_specs=in_specs,
        out_specs=out_specs,
        out_shape=out_shape,
        compiler_params=_cparams(("parallel", "parallel")),
        name="in_proj",
    )(*ins)


def _with_ones(v_t):
    tk = v_t.shape[1]
    row = lax.broadcasted_iota(jnp.int32, (ONES_ROWS, tk), 0)
    return jnp.concatenate([v_t, jnp.where(row == 0, 1.0, 0.0).astype(v_t.dtype)], axis=0)


def _online_step(s_t, tile_max, v_ext, m, acc):
    m_new = jnp.maximum(m, tile_max)
    p = jnp.exp2(s_t - m_new).astype(BF16)
    alpha = jnp.exp2(m - m_new)
    return m_new, alpha * acc + _nn(v_ext, p)


def _tile_off(j, t):
    return j * t if isinstance(j, int) else pl.multiple_of(j * t, t)


PIPE_TILES = 4
COL_BLOCK = 256


def _pipelined_tiles(n_tiles, n_sub, produce_ctx, produce, consume_ctx, consume, carry, peel_first=False):
    u_n = PIPE_TILES
    trip = 2 * u_n
    assert n_tiles % trip == 0 and n_tiles >= trip

    def fill(slot, v0):
        for u in range(u_n):
            ctx = produce_ctx(v0 + u)
            for c in range(n_sub):
                produce(slot, u, c, ctx)

    def half(c_slot, v_c, p_slot, v_p, carry):
        carry = list(carry)
        for u in range(u_n):
            pctx = None if p_slot is None else produce_ctx(v_p + u)
            cctx = consume_ctx(v_c + u)
            for c in range(n_sub):
                if p_slot is not None:
                    produce(p_slot, u, c, pctx)
                carry[c] = consume(c_slot, u, c, cctx, carry[c])
        return tuple(carry)

    def body(v0, carry, last):
        carry = half(0, v0, 1, v0 + u_n, carry)
        return half(1, v0 + u_n, None if last else 0, v0 + trip, carry)

    fill(0, 0)
    n_trips = n_tiles // trip
    first = 0
    if peel_first and n_trips > 1:
        carry = body(0, carry, False)
        first = 1
    carry = lax.fori_loop(first, n_trips - 1, lambda i, c: body(i * trip, c, False), carry)
    return body((n_trips - 1) * trip, carry, True)


def _score_scratch(n_sub, tk):
    n = 2 * PIPE_TILES * n_sub
    return [pltpu.VMEM((tk, COL_BLOCK), F32)] * n + [pltpu.VMEM((1, COL_BLOCK), F32)] * n


def _score_refs(bufs, n_sub):
    n = 2 * PIPE_TILES * n_sub
    at = lambda slot, u, c: (slot * PIPE_TILES + u) * n_sub + c
    return (lambda slot, u, c: bufs[at(slot, u, c)]), (lambda slot, u, c: bufs[n + at(slot, u, c)])


def _mla_attn_kernel(q_ref, k_ref, v_ref, o_ref, *bufs, tk, nk):
    tq = q_ref.shape[3]
    n_sub = tq // COL_BLOCK
    s_buf, mx_buf = _score_refs(bufs, n_sub)
    q_blk = [q_ref[0, 0, :, c * COL_BLOCK:(c + 1) * COL_BLOCK] for c in range(n_sub)]

    def produce_ctx(j):
        return k_ref[0, 0, pl.ds(_tile_off(j, tk), tk), :]

    def produce(slot, u, c, k):
        s_t = _nn(k, q_blk[c])
        s_buf(slot, u, c)[...] = s_t
        mx_buf(slot, u, c)[...] = jnp.max(s_t, axis=0, keepdims=True)

    def consume_ctx(j):
        return _with_ones(v_ref[0, 0, :, pl.ds(_tile_off(j, tk), tk)])

    def consume(slot, u, c, v_ext, carry):
        return _online_step(s_buf(slot, u, c)[...], mx_buf(slot, u, c)[...], v_ext, *carry)

    m0 = jnp.full((1, COL_BLOCK), NEG_INF, F32)
    acc0 = jnp.zeros((MLA_V + ONES_ROWS, COL_BLOCK), F32)
    carry = _pipelined_tiles(nk, n_sub, produce_ctx, produce, consume_ctx, consume, ((m0, acc0),) * n_sub)
    for c, (_, acc) in enumerate(carry):
        o_ref[0, :, c * COL_BLOCK:(c + 1) * COL_BLOCK] = (acc[:MLA_V] / acc[MLA_V:MLA_V + 1]).astype(o_ref.dtype)


def _mla_attention(qm, km, vm, tq, tk):
    b, h, _, s = qm.shape
    return pl.pallas_call(
        functools.partial(_mla_attn_kernel, tk=tk, nk=s // tk),
        grid=(b, h, s // tq),
        in_specs=[pl.BlockSpec((1, 1, LANES, tq), lambda bi, hi, qi: (bi, hi, 0, qi)),
                  pl.BlockSpec((1, 1, s, LANES), lambda bi, hi, qi: (bi, hi, 0, 0)),
                  pl.BlockSpec((1, 1, MLA_V, s), lambda bi, hi, qi: (bi, hi, 0, 0))],
        out_specs=pl.BlockSpec((1, MLA_V, tq), lambda bi, hi, qi: (bi, hi, qi)),
        out_shape=jax.ShapeDtypeStruct((b, W_MLA, s), BF16),
        scratch_shapes=_score_scratch(tq // COL_BLOCK, tk),
        compiler_params=_cparams(("parallel", "parallel", "arbitrary")),
        name="mla_attention",
    )(qm, km, vm)


NEAR_TILES = 4


def _diff_near_offsets(r):
    return min(-1, r - NEAR_TILES), NEAR_TILES - 1


def _diff_attn_kernel(far_ref, lam_ref, q_ref, k_ref, v_ref, bias_ref, subln_ref, o_ref,
                      qfar_buf, *bufs, tq, tk, nk, lam_init):
    s_buf, mx_buf = _score_refs(bufs, 2 * (tq // COL_BLOCK))
    h = pl.program_id(1)
    qi = pl.program_id(2)
    t = tq
    r = tq // tk
    q_t = q_ref[0]
    r16 = lax.broadcasted_iota(jnp.int32, (ONES_ROWS, t), 0)
    zero_map = jnp.zeros((DIFF_QK, t), BF16)
    zero_pad = jnp.zeros((LANES - 2 * DIFF_QK - ONES_ROWS, t), BF16)

    def q_aug(mp, side):
        own = q_t[mp * DIFF_QK:(mp + 1) * DIFF_QK]
        own = [own, zero_map] if mp == 0 else [zero_map, own]
        if side is None:
            aug = jnp.zeros((ONES_ROWS, t), BF16)
        else:
            c = jnp.full((ONES_ROWS, t), far_ref[side, h], F32)
            c_hi = c.astype(BF16).astype(F32)
            aug = jnp.where(r16 == 0, c_hi, jnp.where(r16 == 1, c - c_hi, 0.0)).astype(BF16)
        return jnp.concatenate(own + [aug, zero_pad], axis=0)

    near_lo = jnp.clip(r * qi - 1, 0, nk - NEAR_TILES)
    e_min, _ = _diff_near_offsets(r)
    q_near = [q_aug(0, None), q_aug(1, None)]
    for side in range(2):
        for mp in range(2):
            qfar_buf[side, mp] = q_aug(mp, side)

    def key_tile(v):
        if isinstance(v, int) and v < NEAR_TILES:
            return near_lo + v, True, None
        idx = v - NEAR_TILES
        side = (idx >= near_lo).astype(jnp.int32)
        return idx + NEAR_TILES * side, False, side

    n_cb = tq // COL_BLOCK
    cols = lambda cb: slice(cb * COL_BLOCK, (cb + 1) * COL_BLOCK)

    def produce_ctx(v):
        j, near, side = key_tile(v)
        return k_ref[0, 0, pl.ds(_tile_off(j, tk), tk), :], near, side, j

    def produce(slot, u, c, ctx):
        k, near, side, j = ctx
        mp, cb = divmod(c, n_cb)
        if near:
            s_t = _nn(k, q_near[mp][:, cols(cb)]) + bias_ref[0, j - r * qi - e_min, :, cols(cb)]
        else:
            s_t = _nn(k, qfar_buf[side, mp, :, cols(cb)])
        s_buf(slot, u, c)[...] = s_t
        mx_buf(slot, u, c)[...] = jnp.max(s_t, axis=0, keepdims=True)

    def consume_ctx(v):
        j, _, _ = key_tile(v)
        return _with_ones(v_ref[0, :, pl.ds(_tile_off(j, tk), tk)])

    def consume(slot, u, c, v_ext, carry):
        return _online_step(s_buf(slot, u, c)[...], mx_buf(slot, u, c)[...], v_ext, *carry)

    m0 = jnp.full((1, COL_BLOCK), NEG_INF, F32)
    acc0 = jnp.zeros((DIFF_V + ONES_ROWS, COL_BLOCK), F32)
    carry = _pipelined_tiles(nk, 2 * n_cb, produce_ctx, produce, consume_ctx, consume,
                             ((m0, acc0),) * (2 * n_cb), peel_first=True)

    lam = lam_ref[0, 0]
    for cb in range(n_cb):
        a0, a1 = carry[cb][1], carry[n_cb + cb][1]
        o = a0[:DIFF_V] / a0[DIFF_V:DIFF_V + 1] - lam * (a1[:DIFF_V] / a1[DIFF_V:DIFF_V + 1])
        ms = jnp.mean(o * o, axis=0, keepdims=True)
        o = o * lax.rsqrt(ms + RMS_EPS) * subln_ref[...] * (1.0 - lam_init)
        o_ref[0, :, cols(cb)] = o.astype(o_ref.dtype)


def _diff_attention(qd, kd, vd, bias_tiles, far, lam, subln_col, tq, tk, lam_init):
    b, _, s = qd.shape
    n_e = bias_tiles.shape[1]
    assert tq % tk == 0 and tq // tk + 2 <= NEAR_TILES and tk >= REL_MAX_DIST
    smem = pl.BlockSpec(memory_space=pltpu.SMEM)
    return pl.pallas_call(
        functools.partial(_diff_attn_kernel, tq=tq, tk=tk, nk=s // tk, lam_init=lam_init),
        grid=(b, DIFF_HEADS, s // tq),
        in_specs=[smem, smem,
                  pl.BlockSpec((1, 2 * DIFF_QK, tq), lambda bi, hi, qi: (bi, hi, qi)),
                  pl.BlockSpec((1, 1, s, LANES), lambda bi, hi, qi: (bi, hi, 0, 0)),
                  pl.BlockSpec((1, DIFF_V, s), lambda bi, hi, qi: (bi, hi, 0)),
                  pl.BlockSpec((1, n_e, tk, tq), lambda bi, hi, qi: (hi, 0, 0, 0)),
                  pl.BlockSpec((DIFF_V, 1), lambda bi, hi, qi: (0, 0))],
        out_specs=pl.BlockSpec((1, DIFF_V, tq), lambda bi, hi, qi: (bi, hi, qi)),
        out_shape=jax.ShapeDtypeStruct((b, W_DIFF, s), BF16),
        scratch_shapes=[pltpu.VMEM((2, 2, LANES, tq), BF16)] + _score_scratch(2 * (tq // COL_BLOCK), tk),
        compiler_params=_cparams(("parallel", "parallel", "arbitrary")),
        name="diff_attention",
    )(far, lam, qd, kd, vd, bias_tiles, subln_col)


def _swa_attn_kernel(sink_ref, q_ref, k0_ref, k1_ref, k2_ref, k3_ref, v0_ref, v1_ref, v2_ref, v3_ref,
                     bias_ref, o_ref, *, tq, seq):
    q0 = pl.program_id(1) * tq
    k_win = jnp.concatenate([k0_ref[0], k1_ref[0], k2_ref[0], k3_ref[0]], axis=0)
    v_win = jnp.concatenate([v0_ref[0], v1_ref[0], v2_ref[0], v3_ref[0]], axis=1)
    wk = tq + 2 * WINDOW
    kpos = q0 - WINDOW + lax.broadcasted_iota(jnp.int32, (wk, tq), 0)
    valid = (kpos >= 0) & (kpos < seq)
    grp = SWA_HEADS // SWA_KV_HEADS
    zeros = jnp.zeros((SWA_DIM, tq), BF16)
    for hq in range(SWA_HEADS):
        g = hq // grp
        q_t = q_ref[0, hq * SWA_DIM:(hq + 1) * SWA_DIM, :]
        q_pad = jnp.concatenate([q_t, zeros] if g == 0 else [zeros, q_t], axis=0)
        s_t = jnp.where(valid, _nn(k_win, q_pad) + bias_ref[hq], NEG_INF)
        sink = sink_ref[hq] * LOG2E
        m = jnp.maximum(jnp.max(s_t, axis=0, keepdims=True), sink)
        p = jnp.exp2(s_t - m)
        den = jnp.sum(p, axis=0, keepdims=True) + jnp.exp2(sink - m)
        o = _nn(v_win[g * SWA_DIM:(g + 1) * SWA_DIM], p.astype(BF16)) / den
        o_ref[0, hq * SWA_DIM:(hq + 1) * SWA_DIM, :] = o.astype(o_ref.dtype)


def _swa_attention(sq, sk, sv, bias_tiles, sink, tq):
    b, _, s = sq.shape
    assert tq % WINDOW == 0
    r = tq // WINDOW
    nblk = s // WINDOW
    smem = pl.BlockSpec(memory_space=pltpu.SMEM)
    wk = tq + 2 * WINDOW
    nw = wk // WINDOW

    def kspec(c):
        return pl.BlockSpec((1, WINDOW, LANES),
                            lambda bi, qi: (bi, jnp.clip(qi * r - 1 + c, 0, nblk - 1), 0))

    def vspec(c):
        return pl.BlockSpec((1, SWA_KV_HEADS * SWA_DIM, WINDOW),
                            lambda bi, qi: (bi, 0, jnp.clip(qi * r - 1 + c, 0, nblk - 1)))

    assert nw == 4
    return pl.pallas_call(
        functools.partial(_swa_attn_kernel, tq=tq, seq=s),
        grid=(b, s // tq),
        in_specs=[smem, pl.BlockSpec((1, W_SWA, tq), lambda bi, qi: (bi, 0, qi))]
        + [kspec(c) for c in range(nw)] + [vspec(c) for c in range(nw)]
        + [pl.BlockSpec((SWA_HEADS, wk, tq), lambda bi, qi: (0, 0, 0))],
        out_specs=pl.BlockSpec((1, W_SWA, tq), lambda bi, qi: (bi, 0, qi)),
        out_shape=jax.ShapeDtypeStruct((b, W_SWA, s), BF16),
        compiler_params=_cparams(("parallel", "parallel")),
        name="swa_attention",
    )(sink, sq, sk, sk, sk, sk, sv, sv, sv, sv, bias_tiles)


def _out_proj_kernel(x_ref, om_ref, od_ref, os_ref, g_ref, w_ref, lng_ref, lnb_ref, o_ref, *, alpha):
    o_t = jnp.concatenate([om_ref[0], od_ref[0], os_ref[0]], axis=0)
    og_t = o_t * g_ref[0]
    y = lax.dot_general(og_t, w_ref[...], (((0,), (0,)), ((), ())), preferred_element_type=F32)
    z = alpha * x_ref[0] + y
    mu = jnp.mean(z, axis=1, keepdims=True)
    zc = z - mu
    var = jnp.mean(zc * zc, axis=1, keepdims=True)
    o_ref[0] = zc * lax.rsqrt(var + LN_EPS) * lng_ref[...] + lnb_ref[...]


def _out_proj(x, om, od, osw, g, w_out, ln_g, ln_b, tm, alpha):
    b, s, d = x.shape
    tspec = lambda rows: pl.BlockSpec((1, rows, tm), lambda bi, i: (bi, 0, i))
    return pl.pallas_call(
        functools.partial(_out_proj_kernel, alpha=alpha),
        grid=(b, s // tm),
        in_specs=[pl.BlockSpec((1, tm, d), lambda bi, i: (bi, i, 0)),
                  tspec(W_MLA), tspec(W_DIFF), tspec(W_SWA), tspec(D_MIX),
                  pl.BlockSpec((D_MIX, d), lambda bi, i: (0, 0)),
                  pl.BlockSpec((1, d), lambda bi, i: (0, 0)), pl.BlockSpec((1, d), lambda bi, i: (0, 0))],
        out_specs=pl.BlockSpec((1, tm, d), lambda bi, i: (bi, i, 0)),
        out_shape=jax.ShapeDtypeStruct((b, s, d), F32),
        compiler_params=_cparams(("parallel", "parallel")),
        name="out_proj",
    )(x, om, od, osw, g, w_out, ln_g, ln_b)


def _layer_weights(w_in, q_norm, kv_norm, w_uq, w_ukv):
    cq, ckv, kr, dq, dk, dv, sq, sk, sv, gate = jnp.split(w_in, np.cumsum(IN_SIZES)[:-1].tolist(), axis=1)
    d = w_in.shape[0]
    half = MLA_ROPE // 2
    w_t = jnp.concatenate([cq, ckv, dq, dv, sq, sv, gate], axis=1).T.astype(BF16)
    pad = lambda a, lo, width: jnp.pad(a, ((0, 0), (lo, width - lo - a.shape[1])))
    kr_p = pad(kr, MLA_NOPE, LANES)
    kr_rot_p = pad(jnp.concatenate([-kr[:, half:], kr[:, :half]], axis=1), MLA_NOPE, LANES)
    dk_p = jnp.concatenate([pad(dk[:, h * 2 * DIFF_QK:(h + 1) * 2 * DIFF_QK], 0, LANES)
                            for h in range(DIFF_HEADS)], axis=1)
    w_r = jnp.concatenate([ckv, kr_p, kr_rot_p, dk_p, sk], axis=1).astype(BF16)
    assert w_t.shape == (_T_ROWS, d) and w_r.shape == (d, _R_COLS)
    qk = MLA_NOPE + MLA_ROPE
    w_uq_t = jnp.concatenate([jnp.pad(w_uq[:, h * qk:(h + 1) * qk].T, ((0, LANES - qk), (0, 0)))
                              for h in range(MLA_HEADS)], axis=0).astype(BF16)
    kv = MLA_NOPE + MLA_V
    w_kp = jnp.concatenate([pad(w_ukv[:, h * kv:h * kv + MLA_NOPE], 0, LANES)
                            for h in range(MLA_HEADS)], axis=1).astype(BF16)
    w_v_t = jnp.concatenate([w_ukv[:, h * kv + MLA_NOPE:(h + 1) * kv].T
                             for h in range(MLA_HEADS)], axis=0).astype(BF16)
    return dict(w_t=w_t, w_r=w_r, w_uq_t=w_uq_t, w_kp=w_kp, w_v_t=w_v_t,
                gq_col=q_norm.reshape(-1, 1), gkv_col=kv_norm.reshape(-1, 1), gkv_row=kv_norm.reshape(1, -1))


def _lambda_kernel(lv_ref, o_ref, *, lam_init):
    lv = lv_ref[...]
    a = jnp.sum(lv[0:1] * lv[1:2], axis=1, keepdims=True)
    c = jnp.sum(lv[2:3] * lv[3:4], axis=1, keepdims=True)
    o_ref[...] = jnp.exp(a) - jnp.exp(c) + lam_init


def _diff_lambda(lam_vecs, lam_init):
    return pl.pallas_call(
        functools.partial(_lambda_kernel, lam_init=lam_init),
        out_shape=jax.ShapeDtypeStruct((1, 1), F32),
        name="diff_lambda",
    )(lam_vecs)


def _tiles(seq):
    return dict(tm=min(512, seq), mla_tq=min(1024, seq), mla_tk=256, diff_tq=min(512, seq), diff_tk=256, swa_tq=256)


def kernel(x, w_in, mla_q_norm, mla_kv_norm, mla_w_uq, mla_w_ukv, diff_lambda, diff_subln, swa_sink,
           rel_bias, w_out, ln_g, ln_b):
    depth = w_in.shape[0]
    seq = x.shape[1]
    cfg = _tiles(seq)
    alpha = (2 * depth) ** 0.25
    tabs = _rope_tables(seq)
    e_min, e_max = _diff_near_offsets(cfg["diff_tq"] // cfg["diff_tk"])
    diff_bias = _diff_bias_tiles(rel_bias, cfg["diff_tq"], cfg["diff_tk"], e_min, e_max)
    swa_bias = _swa_bias_tiles(rel_bias, cfg["swa_tq"])
    half = REL_BUCKETS // 2
    far = jnp.stack([rel_bias[half - 1, :DIFF_HEADS], rel_bias[2 * half - 1, :DIFF_HEADS]]) * LOG2E
    for l in range(depth):
        lam_init = 0.8 - 0.6 * math.exp(-0.3 * l)
        lw = _layer_weights(w_in[l], mla_q_norm[l], mla_kv_norm[l], mla_w_uq[l], mla_w_ukv[l])
        qm, km, vm, qd, kd, vd, sq, sk, sv, g = _in_proj(x, lw, tabs, cfg["tm"])
        om = _mla_attention(qm, km, vm, cfg["mla_tq"], cfg["mla_tk"])
        lam = _diff_lambda(diff_lambda[l], lam_init)
        od = _diff_attention(qd, kd, vd, diff_bias, far, lam, diff_subln[l].reshape(-1, 1), cfg["diff_tq"],
                             cfg["diff_tk"], lam_init)
        osw = _swa_attention(sq, sk, sv, swa_bias, swa_sink[l], cfg["swa_tq"])
        x = _out_proj(x, om, od, osw, g, w_out[l].astype(BF16), ln_g[l].reshape(1, -1), ln_b[l].reshape(1, -1),
                      cfg["tm"], alpha)
    return x
```

```python
import functools
import math

import numpy as np
import jax
import jax.numpy as jnp
from jax import lax
from jax.experimental import pallas as pl
from jax.experimental.pallas import tpu as pltpu

D_MODEL = 1024
MLA_HEADS, MLA_Q_RANK, MLA_KV_RANK, MLA_NOPE, MLA_ROPE, MLA_V = 6, 192, 128, 64, 32, 64
DIFF_HEADS, DIFF_QK, DIFF_V = 4, 32, 64
SWA_HEADS, SWA_KV_HEADS, SWA_DIM, WINDOW = 6, 2, 64, 128
W_MLA, W_DIFF, W_SWA = MLA_HEADS * MLA_V, DIFF_HEADS * DIFF_V, SWA_HEADS * SWA_DIM
D_MIX = W_MLA + W_DIFF + W_SWA
IN_SIZES = (MLA_Q_RANK, MLA_KV_RANK, MLA_ROPE, DIFF_HEADS * 2 * DIFF_QK, DIFF_HEADS * 2 * DIFF_QK,
            W_DIFF, W_SWA, SWA_KV_HEADS * SWA_DIM, SWA_KV_HEADS * SWA_DIM, D_MIX)
REL_BUCKETS, REL_MAX_DIST = 32, 128
ROPE_THETA = 10000.0
RMS_EPS, LN_EPS = 1e-6, 1e-5

LOG2E = math.log2(math.e)
LANES = 128
ONES_ROWS = 16
VMEM_LIMIT = 56 * 1024 * 1024

F32, BF16 = jnp.float32, jnp.bfloat16
NEG_INF = float("-inf")


def _cparams(sem):
    return pltpu.CompilerParams(dimension_semantics=sem, vmem_limit_bytes=VMEM_LIMIT)


def _nt(a, b):
    return lax.dot_general(a, b, (((1,), (1,)), ((), ())), preferred_element_type=F32)


def _nn(a, b):
    return jnp.dot(a, b, preferred_element_type=F32)


def _rope_tables_kernel(fcol_ref, frow_ref, cos_t_ref, sin_t_ref, cos_p_ref, sin_p_ref, *, tile):
    p0 = pl.program_id(0) * tile
    pos_l = (p0 + lax.broadcasted_iota(jnp.int32, (MLA_ROPE // 2, tile), 1)).astype(F32)
    ang_t = pos_l * fcol_ref[...]
    cos_t_ref[...] = jnp.cos(ang_t)
    sin_t_ref[...] = jnp.sin(ang_t)
    pos_s = (p0 + lax.broadcasted_iota(jnp.int32, (tile, LANES), 0)).astype(F32)
    ang_p = pos_s * frow_ref[...]
    lane = lax.broadcasted_iota(jnp.int32, (tile, LANES), 1)
    live = (lane >= MLA_NOPE) & (lane < MLA_NOPE + MLA_ROPE)
    cos_p_ref[...] = jnp.where(live, jnp.cos(ang_p), 0.0)
    sin_p_ref[...] = jnp.where(live, jnp.sin(ang_p), 0.0)


def _rope_tables(seq):
    half = MLA_ROPE // 2
    freqs = (ROPE_THETA ** (-np.arange(half, dtype=np.float64) / half)).astype(np.float32)
    fcol = jnp.asarray(freqs.reshape(half, 1))
    frow_np = np.zeros((1, LANES), np.float32)
    frow_np[0, MLA_NOPE:MLA_NOPE + half] = freqs
    frow_np[0, MLA_NOPE + half:MLA_NOPE + MLA_ROPE] = freqs
    frow = jnp.asarray(frow_np)
    tile = min(seq, 1024)
    return pl.pallas_call(
        functools.partial(_rope_tables_kernel, tile=tile),
        grid=(seq // tile,),
        in_specs=[pl.BlockSpec((half, 1), lambda i: (0, 0)), pl.BlockSpec((1, LANES), lambda i: (0, 0))],
        out_specs=[pl.BlockSpec((half, tile), lambda i: (0, i)), pl.BlockSpec((half, tile), lambda i: (0, i)),
                   pl.BlockSpec((tile, LANES), lambda i: (i, 0)), pl.BlockSpec((tile, LANES), lambda i: (i, 0))],
        out_shape=[jax.ShapeDtypeStruct((half, seq), F32), jax.ShapeDtypeStruct((half, seq), F32),
                   jax.ShapeDtypeStruct((seq, LANES), F32), jax.ShapeDtypeStruct((seq, LANES), F32)],
        compiler_params=_cparams(("arbitrary",)),
        name="rope_tables",
    )(fcol, frow)


def _t5_bucket(rel):
    half = REL_BUCKETS // 2
    max_exact = half // 2
    n = jnp.abs(rel)
    large = jnp.full(rel.shape, max_exact, jnp.int32)
    for k in range(1, half - max_exact):
        thr = max_exact * (REL_MAX_DIST / max_exact) ** (k / (half - max_exact))
        thr_i = int(round(thr)) if abs(thr - round(thr)) < 1e-9 else int(math.ceil(thr))
        large = large + (n >= thr_i).astype(jnp.int32)
    return jnp.where(rel > 0, half, 0) + jnp.where(n < max_exact, n, large)


def _bias_lookup(tab_ref, bucket, head):
    out = jnp.zeros(bucket.shape, F32)
    for b in range(REL_BUCKETS):
        out = jnp.where(bucket == b, tab_ref[b, head], out)
    return out


def _diff_bias_kernel(tab_ref, out_ref, *, tq, tk, e_min):
    h = pl.program_id(0)
    d = (pl.program_id(1) + e_min) * tk
    ki = lax.broadcasted_iota(jnp.int32, (tk, tq), 0)
    qj = lax.broadcasted_iota(jnp.int32, (tk, tq), 1)
    out_ref[0, 0] = _bias_lookup(tab_ref, _t5_bucket(d + ki - qj), h) * LOG2E


def _diff_bias_tiles(rel_bias, tq, tk, e_min, e_max):
    n_e = e_max - e_min + 1
    return pl.pallas_call(
        functools.partial(_diff_bias_kernel, tq=tq, tk=tk, e_min=e_min),
        grid=(DIFF_HEADS, n_e),
        in_specs=[pl.BlockSpec(memory_space=pltpu.SMEM)],
        out_specs=pl.BlockSpec((1, 1, tk, tq), lambda h, d: (h, d, 0, 0)),
        out_shape=jax.ShapeDtypeStruct((DIFF_HEADS, n_e, tk, tq), F32),
        compiler_params=_cparams(("arbitrary", "arbitrary")),
        name="diff_bias_tiles",
    )(rel_bias)


def _swa_bias_kernel(tab_ref, out_ref, *, tq):
    h = pl.program_id(0)
    wk = tq + 2 * WINDOW
    ki = lax.broadcasted_iota(jnp.int32, (wk, tq), 0)
    qj = lax.broadcasted_iota(jnp.int32, (wk, tq), 1)
    rel = ki - WINDOW - qj
    bias = _bias_lookup(tab_ref, _t5_bucket(rel), DIFF_HEADS + h) * LOG2E
    out_ref[0] = jnp.where(jnp.abs(rel) <= WINDOW, bias, NEG_INF)


def _swa_bias_tiles(rel_bias, tq):
    wk = tq + 2 * WINDOW
    return pl.pallas_call(
        functools.partial(_swa_bias_kernel, tq=tq),
        grid=(SWA_HEADS,),
        in_specs=[pl.BlockSpec(memory_space=pltpu.SMEM)],
        out_specs=pl.BlockSpec((1, wk, tq), lambda h: (h, 0, 0)),
        out_shape=jax.ShapeDtypeStruct((SWA_HEADS, wk, tq), F32),
        compiler_params=_cparams(("arbitrary",)),
        name="swa_bias_tiles",
    )(rel_bias)


_T_CQ, _T_CKV, _T_DQ, _T_DV, _T_SQ, _T_SV, _T_GATE = 0, 192, 320, 576, 832, 1216, 1344
_T_ROWS = 2368
_R_CKV, _R_KR, _R_KRROT, _R_DK, _R_SK = 0, 128, 256, 384, 896
_R_COLS = 1024


def _rms_t(v, g_col):
    ms = jnp.mean(v * v, axis=0, keepdims=True)
    return v * lax.rsqrt(ms + RMS_EPS) * g_col


def _in_proj_kernel(x_ref, wt_ref, wr_ref, wuq_ref, wkp_ref, wvt_ref, gq_ref, gkv_col_ref, gkv_row_ref,
                    cos_t_ref, sin_t_ref, cos_p_ref, sin_p_ref, ones_ref,
                    qm_ref, km_ref, vm_ref, qd_ref, kd_ref, vd_ref, sq_ref, sk_ref, sv_ref, g_ref):
    xb = x_ref[0].astype(BF16)

    def proj_t(lo, hi):
        return _nt(wt_ref[lo:hi, :], xb)

    cqn = _rms_t(proj_t(_T_CQ, _T_CKV), gq_ref[...]).astype(BF16)
    q_all = _nn(wuq_ref[...], cqn)
    cos_t, sin_t = cos_t_ref[...], sin_t_ref[...]
    q_scale = LOG2E / math.sqrt(MLA_NOPE + MLA_ROPE)
    half = MLA_ROPE // 2
    for h in range(MLA_HEADS):
        blk = q_all[h * LANES:(h + 1) * LANES]
        x1 = blk[MLA_NOPE:MLA_NOPE + half]
        x2 = blk[MLA_NOPE + half:MLA_NOPE + MLA_ROPE]
        roped = jnp.concatenate([blk[:MLA_NOPE], x1 * cos_t - x2 * sin_t, x2 * cos_t + x1 * sin_t,
                                 blk[MLA_NOPE + MLA_ROPE:]], axis=0)
        qm_ref[0, h] = (roped * q_scale).astype(BF16)

    ckvn_t = _rms_t(proj_t(_T_CKV, _T_DQ), gkv_col_ref[...]).astype(BF16)
    v_all = _nn(wvt_ref[...], ckvn_t).astype(BF16)
    for h in range(MLA_HEADS):
        vm_ref[0, h] = v_all[h * MLA_V:(h + 1) * MLA_V]

    r = _nn(xb, wr_ref[...])
    ckv = r[:, _R_CKV:_R_KR]
    ms = jnp.mean(ckv * ckv, axis=1, keepdims=True)
    ckvn = (ckv * lax.rsqrt(ms + RMS_EPS) * gkv_row_ref[...]).astype(BF16)
    k_nope = _nn(ckvn, wkp_ref[...])
    k_rope = r[:, _R_KR:_R_KRROT] * cos_p_ref[...] + r[:, _R_KRROT:_R_DK] * sin_p_ref[...]
    for h in range(MLA_HEADS):
        km_ref[0, h] = (k_nope[:, h * LANES:(h + 1) * LANES] + k_rope).astype(BF16)
    ones_row = ones_ref[...]
    for h in range(DIFF_HEADS):
        kd_ref[0, h] = (r[:, _R_DK + h * LANES:_R_DK + (h + 1) * LANES] + ones_row).astype(BF16)
    sk_ref[0] = r[:, _R_SK:_R_COLS].astype(BF16)

    qd_ref[0] = (proj_t(_T_DQ, _T_DV) * (LOG2E / math.sqrt(DIFF_QK))).astype(BF16)
    vd_ref[0] = proj_t(_T_DV, _T_SQ).astype(BF16)
    sq_ref[0] = (proj_t(_T_SQ, _T_SV) * (LOG2E / math.sqrt(SWA_DIM))).astype(BF16)
    sv_ref[0] = proj_t(_T_SV, _T_GATE).astype(BF16)
    gate = proj_t(_T_GATE, _T_ROWS)
    g_ref[0] = (gate * jax.nn.sigmoid(gate)).astype(BF16)


def _in_proj(x, lw, tabs, tm):
    b, s, d = x.shape
    cos_t, sin_t, cos_p, sin_p = tabs
    full = lambda a: pl.BlockSpec(a.shape, lambda bi, i: (0,) * a.ndim)
    ones_np = np.zeros((1, LANES), np.float32)
    ones_np[0, 2 * DIFF_QK:2 * DIFF_QK + 2] = 1.0
    ones_row = jnp.asarray(ones_np)
    half = MLA_ROPE // 2
    ins = [x, lw["w_t"], lw["w_r"], lw["w_uq_t"], lw["w_kp"], lw["w_v_t"], lw["gq_col"], lw["gkv_col"],
           lw["gkv_row"], cos_t, sin_t, cos_p, sin_p, ones_row]
    in_specs = [pl.BlockSpec((1, tm, d), lambda bi, i: (bi, i, 0))]
    in_specs += [full(a) for a in ins[1:9]]
    in_specs += [pl.BlockSpec((half, tm), lambda bi, i: (0, i)), pl.BlockSpec((half, tm), lambda bi, i: (0, i)),
                 pl.BlockSpec((tm, LANES), lambda bi, i: (i, 0)), pl.BlockSpec((tm, LANES), lambda bi, i: (i, 0)),
                 full(ones_row)]
    sds = jax.ShapeDtypeStruct
    out_shape = [
        sds((b, MLA_HEADS, LANES, s), BF16),
        sds((b, MLA_HEADS, s, LANES), BF16),
        sds((b, MLA_HEADS, MLA_V, s), BF16),
        sds((b, DIFF_HEADS * 2 * DIFF_QK, s), BF16),
        sds((b, DIFF_HEADS, s, LANES), BF16),
        sds((b, W_DIFF, s), BF16),
        sds((b, W_SWA, s), BF16),
        sds((b, s, LANES), BF16),
        sds((b, SWA_KV_HEADS * SWA_DIM, s), BF16),
        sds((b, D_MIX, s), BF16),
    ]
    out_specs = [
        pl.BlockSpec((1, MLA_HEADS, LANES, tm), lambda bi, i: (bi, 0, 0, i)),
        pl.BlockSpec((1, MLA_HEADS, tm, LANES), lambda bi, i: (bi, 0, i, 0)),
        pl.BlockSpec((1, MLA_HEADS, MLA_V, tm), lambda bi, i: (bi, 0, 0, i)),
        pl.BlockSpec((1, DIFF_HEADS * 2 * DIFF_QK, tm), lambda bi, i: (bi, 0, i)),
        pl.BlockSpec((1, DIFF_HEADS, tm, LANES), lambda bi, i: (bi, 0, i, 0)),
        pl.BlockSpec((1, W_DIFF, tm), lambda bi, i: (bi, 0, i)),
        pl.BlockSpec((1, W_SWA, tm), lambda bi, i: (bi, 0, i)),
        pl.BlockSpec((1, tm, LANES), lambda bi, i: (bi, i, 0)),
        pl.BlockSpec((1, SWA_KV_HEADS * SWA_DIM, tm), lambda bi, i: (bi, 0, i)),
        pl.BlockSpec((1, D_MIX, tm), lambda bi, i: (bi, 0, i)),
    ]
    return pl.pallas_call(
        _in_proj_kernel,
        grid=(b, s // tm),
        in_specs=in_specs,
        out_specs=out_specs,
        out_shape=out_shape,
        compiler_params=_cparams(("parallel", "parallel")),
        name="in_proj",
    )(*ins)


def _with_ones(v_t):
    tk = v_t.shape[1]
    row = lax.broadcasted_iota(jnp.int32, (ONES_ROWS, tk), 0)
    return jnp.concatenate([v_t, jnp.where(row == 0, 1.0, 0.0).astype(v_t.dtype)], axis=0)


def _online_step(s_t, tile_max, v_ext, m, acc):
    m_new = jnp.maximum(m, tile_max)
    p = jnp.exp2(s_t - m_new).astype(BF16)
    alpha = jnp.exp2(m - m_new)
    return m_new, alpha * acc + _nn(v_ext, p)


def _tile_off(j, t):
    return j * t if isinstance(j, int) else pl.multiple_of(j * t, t)


PIPE_TILES = 2
COL_BLOCK = 256
MAX_UNROLLED_TRIPS = 16


def _pipelined_tiles(n_tiles, n_sub, produce_ctx, produce, consume_ctx, consume, carry, peel_first=False):
    u_n = PIPE_TILES
    trip = 2 * u_n
    assert n_tiles % trip == 0 and n_tiles >= trip

    def fill(slot, v0):
        for u in range(u_n):
            ctx = produce_ctx(v0 + u)
            for c in range(n_sub):
                produce(slot, u, c, ctx)

    def half(c_slot, v_c, p_slot, v_p, carry):
        carry = list(carry)
        for u in range(u_n):
            pctx = None if p_slot is None else produce_ctx(v_p + u)
            cctx = consume_ctx(v_c + u)
            for c in range(n_sub):
                carry[c] = consume(c_slot, u, c, cctx, carry[c])
                if p_slot is not None:
                    produce(p_slot, u, c, pctx)
        return tuple(carry)

    def body(v0, carry, last):
        carry = half(0, v0, 1, v0 + u_n, carry)
        return half(1, v0 + u_n, None if last else 0, v0 + trip, carry)

    fill(0, 0)
    n_trips = n_tiles // trip
    if n_trips <= MAX_UNROLLED_TRIPS:
        for i in range(n_trips):
            carry = body(i * trip, carry, i == n_trips - 1)
        return carry
    first = 0
    if peel_first:
        carry = body(0, carry, False)
        first = 1
    carry = lax.fori_loop(first, n_trips - 1, lambda i, c: body(i * trip, c, False), carry)
    return body((n_trips - 1) * trip, carry, True)


def _score_scratch(n_sub, tk):
    n = 2 * PIPE_TILES * n_sub
    return [pltpu.VMEM((1, tk, COL_BLOCK), F32)] * n + [pltpu.VMEM((1, 1, COL_BLOCK), F32)] * n


def _score_refs(bufs, n_sub):
    n = 2 * PIPE_TILES * n_sub
    z = jnp.minimum(pl.program_id(0), 0)
    at = lambda slot, u, c: (slot * PIPE_TILES + u) * n_sub + c
    return (lambda slot, u, c: bufs[at(slot, u, c)].at[z]), (lambda slot, u, c: bufs[n + at(slot, u, c)].at[z])


def _mla_attn_kernel(q_ref, k_ref, v_ref, o_ref, *bufs, tk, nk):
    tq = q_ref.shape[3]
    n_sub = tq // COL_BLOCK
    s_buf, mx_buf = _score_refs(bufs, n_sub)
    q_blk = [q_ref[0, 0, :, c * COL_BLOCK:(c + 1) * COL_BLOCK] for c in range(n_sub)]

    def produce_ctx(j):
        return k_ref[0, 0, pl.ds(_tile_off(j, tk), tk), :]

    def produce(slot, u, c, k):
        s_t = _nn(k, q_blk[c])
        s_buf(slot, u, c)[...] = s_t
        mx_buf(slot, u, c)[...] = jnp.max(s_t, axis=0, keepdims=True)

    def consume_ctx(j):
        return _with_ones(v_ref[0, 0, :, pl.ds(_tile_off(j, tk), tk)])

    def consume(slot, u, c, v_ext, carry):
        return _online_step(s_buf(slot, u, c)[...], mx_buf(slot, u, c)[...], v_ext, *carry)

    m0 = jnp.full((1, COL_BLOCK), NEG_INF, F32)
    acc0 = jnp.zeros((MLA_V + ONES_ROWS, COL_BLOCK), F32)
    carry = _pipelined_tiles(nk, n_sub, produce_ctx, produce, consume_ctx, consume, ((m0, acc0),) * n_sub)
    for c, (_, acc) in enumerate(carry):
        o_ref[0, :, c * COL_BLOCK:(c + 1) * COL_BLOCK] = (acc[:MLA_V] / acc[MLA_V:MLA_V + 1]).astype(o_ref.dtype)


def _mla_attention(qm, km, vm, tq, tk):
    b, h, _, s = qm.shape
    return pl.pallas_call(
        functools.partial(_mla_attn_kernel, tk=tk, nk=s // tk),
        grid=(b, h, s // tq),
        in_specs=[pl.BlockSpec((1, 1, LANES, tq), lambda bi, hi, qi: (bi, hi, 0, qi)),
                  pl.BlockSpec((1, 1, s, LANES), lambda bi, hi, qi: (bi, hi, 0, 0)),
                  pl.BlockSpec((1, 1, MLA_V, s), lambda bi, hi, qi: (bi, hi, 0, 0))],
        out_specs=pl.BlockSpec((1, MLA_V, tq), lambda bi, hi, qi: (bi, hi, qi)),
        out_shape=jax.ShapeDtypeStruct((b, W_MLA, s), BF16),
        scratch_shapes=_score_scratch(tq // COL_BLOCK, tk),
        compiler_params=_cparams(("parallel", "parallel", "arbitrary")),
        name="mla_attention",
    )(qm, km, vm)


NEAR_TILES = 4


def _diff_near_offsets(r):
    return min(-1, r - NEAR_TILES), NEAR_TILES - 1


def _diff_attn_kernel(far_ref, lam_ref, q_ref, k_ref, v_ref, bias_ref, subln_ref, o_ref,
                      qfar_buf, *bufs, tq, tk, nk, lam_init):
    s_buf, mx_buf = _score_refs(bufs, 2 * (tq // COL_BLOCK))
    h = pl.program_id(1)
    qi = pl.program_id(2)
    t = tq
    r = tq // tk
    q_t = q_ref[0]
    r16 = lax.broadcasted_iota(jnp.int32, (ONES_ROWS, t), 0)
    zero_map = jnp.zeros((DIFF_QK, t), BF16)
    zero_pad = jnp.zeros((LANES - 2 * DIFF_QK - ONES_ROWS, t), BF16)

    def q_aug(mp, side):
        own = q_t[mp * DIFF_QK:(mp + 1) * DIFF_QK]
        own = [own, zero_map] if mp == 0 else [zero_map, own]
        if side is None:
            aug = jnp.zeros((ONES_ROWS, t), BF16)
        else:
            c = jnp.full((ONES_ROWS, t), far_ref[side, h], F32)
            c_hi = c.astype(BF16).astype(F32)
            aug = jnp.where(r16 == 0, c_hi, jnp.where(r16 == 1, c - c_hi, 0.0)).astype(BF16)
        return jnp.concatenate(own + [aug, zero_pad], axis=0)

    near_lo = jnp.clip(r * qi - 1, 0, nk - NEAR_TILES)
    e_min, _ = _diff_near_offsets(r)
    q_near = [q_aug(0, None), q_aug(1, None)]
    for side in range(2):
        for mp in range(2):
            qfar_buf[side, mp] = q_aug(mp, side)

    def key_tile(v):
        if isinstance(v, int) and v < NEAR_TILES:
            return near_lo + v, True, None
        idx = v - NEAR_TILES
        side = (idx >= near_lo).astype(jnp.int32)
        return idx + NEAR_TILES * side, False, side

    n_cb = tq // COL_BLOCK
    cols = lambda cb: slice(cb * COL_BLOCK, (cb + 1) * COL_BLOCK)

    def produce_ctx(v):
        j, near, side = key_tile(v)
        return k_ref[0, 0, pl.ds(_tile_off(j, tk), tk), :], near, side, j

    def produce(slot, u, c, ctx):
        k, near, side, j = ctx
        mp, cb = divmod(c, n_cb)
        if near:
            s_t = _nn(k, q_near[mp][:, cols(cb)]) + bias_ref[0, j - r * qi - e_min, :, cols(cb)]
        else:
            s_t = _nn(k, qfar_buf[side, mp, :, cols(cb)])
        s_buf(slot, u, c)[...] = s_t
        mx_buf(slot, u, c)[...] = jnp.max(s_t, axis=0, keepdims=True)

    def consume_ctx(v):
        j, _, _ = key_tile(v)
        return _with_ones(v_ref[0, :, pl.ds(_tile_off(j, tk), tk)])

    def consume(slot, u, c, v_ext, carry):
        return _online_step(s_buf(slot, u, c)[...], mx_buf(slot, u, c)[...], v_ext, *carry)

    m0 = jnp.full((1, COL_BLOCK), NEG_INF, F32)
    acc0 = jnp.zeros((DIFF_V + ONES_ROWS, COL_BLOCK), F32)
    carry = _pipelined_tiles(nk, 2 * n_cb, produce_ctx, produce, consume_ctx, consume,
                             ((m0, acc0),) * (2 * n_cb), peel_first=True)

    lam = lam_ref[0, 0]
    for cb in range(n_cb):
        a0, a1 = carry[cb][1], carry[n_cb + cb][1]
        o = a0[:DIFF_V] / a0[DIFF_V:DIFF_V + 1] - lam * (a1[:DIFF_V] / a1[DIFF_V:DIFF_V + 1])
        ms = jnp.mean(o * o, axis=0, keepdims=True)
        o = o * lax.rsqrt(ms + RMS_EPS) * subln_ref[...] * (1.0 - lam_init)
        o_ref[0, :, cols(cb)] = o.astype(o_ref.dtype)


def _diff_attention(qd, kd, vd, bias_tiles, far, lam, subln_col, tq, tk, lam_init):
    b, _, s = qd.shape
    n_e = bias_tiles.shape[1]
    assert tq % tk == 0 and tq // tk + 2 <= NEAR_TILES and tk >= REL_MAX_DIST
    smem = pl.BlockSpec(memory_space=pltpu.SMEM)
    return pl.pallas_call(
        functools.partial(_diff_attn_kernel, tq=tq, tk=tk, nk=s // tk, lam_init=lam_init),
        grid=(b, DIFF_HEADS, s // tq),
        in_specs=[smem, smem,
                  pl.BlockSpec((1, 2 * DIFF_QK, tq), lambda bi, hi, qi: (bi, hi, qi)),
                  pl.BlockSpec((1, 1, s, LANES), lambda bi, hi, qi: (bi, hi, 0, 0)),
                  pl.BlockSpec((1, DIFF_V, s), lambda bi, hi, qi: (bi, hi, 0)),
                  pl.BlockSpec((1, n_e, tk, tq), lambda bi, hi, qi: (hi, 0, 0, 0)),
                  pl.BlockSpec((DIFF_V, 1), lambda bi, hi, qi: (0, 0))],
        out_specs=pl.BlockSpec((1, DIFF_V, tq), lambda bi, hi, qi: (bi, hi, qi)),
        out_shape=jax.ShapeDtypeStruct((b, W_DIFF, s), BF16),
        scratch_shapes=[pltpu.VMEM((2, 2, LANES, tq), BF16)] + _score_scratch(2 * (tq // COL_BLOCK), tk),
        compiler_params=_cparams(("parallel", "parallel", "arbitrary")),
        name="diff_attention",
    )(far, lam, qd, kd, vd, bias_tiles, subln_col)


def _swa_attn_kernel(sink_ref, q_ref, k0_ref, k1_ref, k2_ref, k3_ref, v0_ref, v1_ref, v2_ref, v3_ref,
                     bias_ref, o_ref, *, tq, seq):
    q0 = pl.program_id(1) * tq
    k_win = jnp.concatenate([k0_ref[0], k1_ref[0], k2_ref[0], k3_ref[0]], axis=0)
    v_win = jnp.concatenate([v0_ref[0], v1_ref[0], v2_ref[0], v3_ref[0]], axis=1)
    wk = tq + 2 * WINDOW
    kpos = q0 - WINDOW + lax.broadcasted_iota(jnp.int32, (wk, tq), 0)
    valid = (kpos >= 0) & (kpos < seq)
    grp = SWA_HEADS // SWA_KV_HEADS
    zeros = jnp.zeros((SWA_DIM, tq), BF16)
    for hq in range(SWA_HEADS):
        g = hq // grp
        q_t = q_ref[0, hq * SWA_DIM:(hq + 1) * SWA_DIM, :]
        q_pad = jnp.concatenate([q_t, zeros] if g == 0 else [zeros, q_t], axis=0)
        s_t = jnp.where(valid, _nn(k_win, q_pad) + bias_ref[hq], NEG_INF)
        sink = sink_ref[hq] * LOG2E
        m = jnp.maximum(jnp.max(s_t, axis=0, keepdims=True), sink)
        p = jnp.exp2(s_t - m)
        den = jnp.sum(p, axis=0, keepdims=True) + jnp.exp2(sink - m)
        o = _nn(v_win[g * SWA_DIM:(g + 1) * SWA_DIM], p.astype(BF16)) / den
        o_ref[0, hq * SWA_DIM:(hq + 1) * SWA_DIM, :] = o.astype(o_ref.dtype)


def _swa_attention(sq, sk, sv, bias_tiles, sink, tq):
    b, _, s = sq.shape
    assert tq % WINDOW == 0
    r = tq // WINDOW
    nblk = s // WINDOW
    smem = pl.BlockSpec(memory_space=pltpu.SMEM)
    wk = tq + 2 * WINDOW
    nw = wk // WINDOW

    def kspec(c):
        return pl.BlockSpec((1, WINDOW, LANES),
                            lambda bi, qi: (bi, jnp.clip(qi * r - 1 + c, 0, nblk - 1), 0))

    def vspec(c):
        return pl.BlockSpec((1, SWA_KV_HEADS * SWA_DIM, WINDOW),
                            lambda bi, qi: (bi, 0, jnp.clip(qi * r - 1 + c, 0, nblk - 1)))

    assert nw == 4
    return pl.pallas_call(
        functools.partial(_swa_attn_kernel, tq=tq, seq=s),
        grid=(b, s // tq),
        in_specs=[smem, pl.BlockSpec((1, W_SWA, tq), lambda bi, qi: (bi, 0, qi))]
        + [kspec(c) for c in range(nw)] + [vspec(c) for c in range(nw)]
        + [pl.BlockSpec((SWA_HEADS, wk, tq), lambda bi, qi: (0, 0, 0))],
        out_specs=pl.BlockSpec((1, W_SWA, tq), lambda bi, qi: (bi, 0, qi)),
        out_shape=jax.ShapeDtypeStruct((b, W_SWA, s), BF16),
        compiler_params=_cparams(("parallel", "parallel")),
        name="swa_attention",
    )(sink, sq, sk, sk, sk, sk, sv, sv, sv, sv, bias_tiles)


def _out_proj_kernel(x_ref, om_ref, od_ref, os_ref, g_ref, w_ref, lng_ref, lnb_ref, o_ref, *, alpha):
    o_t = jnp.concatenate([om_ref[0], od_ref[0], os_ref[0]], axis=0)
    og_t = o_t * g_ref[0]
    y = lax.dot_general(og_t, w_ref[...], (((0,), (0,)), ((), ())), preferred_element_type=F32)
    z = alpha * x_ref[0] + y
    mu = jnp.mean(z, axis=1, keepdims=True)
    zc = z - mu
    var = jnp.mean(zc * zc, axis=1, keepdims=True)
    o_ref[0] = zc * lax.rsqrt(var + LN_EPS) * lng_ref[...] + lnb_ref[...]


def _out_proj(x, om, od, osw, g, w_out, ln_g, ln_b, tm, alpha):
    b, s, d = x.shape
    tspec = lambda rows: pl.BlockSpec((1, rows, tm), lambda bi, i: (bi, 0, i))
    return pl.pallas_call(
        functools.partial(_out_proj_kernel, alpha=alpha),
        grid=(b, s // tm),
        in_specs=[pl.BlockSpec((1, tm, d), lambda bi, i: (bi, i, 0)),
                  tspec(W_MLA), tspec(W_DIFF), tspec(W_SWA), tspec(D_MIX),
                  pl.BlockSpec((D_MIX, d), lambda bi, i: (0, 0)),
                  pl.BlockSpec((1, d), lambda bi, i: (0, 0)), pl.BlockSpec((1, d), lambda bi, i: (0, 0))],
        out_specs=pl.BlockSpec((1, tm, d), lambda bi, i: (bi, i, 0)),
        out_shape=jax.ShapeDtypeStruct((b, s, d), F32),
        compiler_params=_cparams(("parallel", "parallel")),
        name="out_proj",
    )(x, om, od, osw, g, w_out, ln_g, ln_b)


def _layer_weights(w_in, q_norm, kv_norm, w_uq, w_ukv):
    cq, ckv, kr, dq, dk, dv, sq, sk, sv, gate = jnp.split(w_in, np.cumsum(IN_SIZES)[:-1].tolist(), axis=1)
    d = w_in.shape[0]
    half = MLA_ROPE // 2
    w_t = jnp.concatenate([cq, ckv, dq, dv, sq, sv, gate], axis=1).T.astype(BF16)
    pad = lambda a, lo, width: jnp.pad(a, ((0, 0), (lo, width - lo - a.shape[1])))
    kr_p = pad(kr, MLA_NOPE, LANES)
    kr_rot_p = pad(jnp.concatenate([-kr[:, half:], kr[:, :half]], axis=1), MLA_NOPE, LANES)
    dk_p = jnp.concatenate([pad(dk[:, h * 2 * DIFF_QK:(h + 1) * 2 * DIFF_QK], 0, LANES)
                            for h in range(DIFF_HEADS)], axis=1)
    w_r = jnp.concatenate([ckv, kr_p, kr_rot_p, dk_p, sk], axis=1).astype(BF16)
    assert w_t.shape == (_T_ROWS, d) and w_r.shape == (d, _R_COLS)
    qk = MLA_NOPE + MLA_ROPE
    w_uq_t = jnp.concatenate([jnp.pad(w_uq[:, h * qk:(h + 1) * qk].T, ((0, LANES - qk), (0, 0)))
                              for h in range(MLA_HEADS)], axis=0).astype(BF16)
    kv = MLA_NOPE + MLA_V
    w_kp = jnp.concatenate([pad(w_ukv[:, h * kv:h * kv + MLA_NOPE], 0, LANES)
                            for h in range(MLA_HEADS)], axis=1).astype(BF16)
    w_v_t = jnp.concatenate([w_ukv[:, h * kv + MLA_NOPE:(h + 1) * kv].T
                             for h in range(MLA_HEADS)], axis=0).astype(BF16)
    return dict(w_t=w_t, w_r=w_r, w_uq_t=w_uq_t, w_kp=w_kp, w_v_t=w_v_t,
                gq_col=q_norm.reshape(-1, 1), gkv_col=kv_norm.reshape(-1, 1), gkv_row=kv_norm.reshape(1, -1))


def _lambda_kernel(lv_ref, o_ref, *, lam_init):
    lv = lv_ref[...]
    a = jnp.sum(lv[0:1] * lv[1:2], axis=1, keepdims=True)
    c = jnp.sum(lv[2:3] * lv[3:4], axis=1, keepdims=True)
    o_ref[...] = jnp.exp(a) - jnp.exp(c) + lam_init


def _diff_lambda(lam_vecs, lam_init):
    return pl.pallas_call(
        functools.partial(_lambda_kernel, lam_init=lam_init),
        out_shape=jax.ShapeDtypeStruct((1, 1), F32),
        name="diff_lambda",
    )(lam_vecs)


def _tiles(seq):
    return dict(tm=min(512, seq), mla_tq=min(1024, seq), mla_tk=256, diff_tq=min(512, seq), diff_tk=256, swa_tq=256)


def kernel(x, w_in, mla_q_norm, mla_kv_norm, mla_w_uq, mla_w_ukv, diff_lambda, diff_subln, swa_sink,
           rel_bias, w_out, ln_g, ln_b):
    depth = w_in.shape[0]
    seq = x.shape[1]
    cfg = _tiles(seq)
    alpha = (2 * depth) ** 0.25
    tabs = _rope_tables(seq)
    e_min, e_max = _diff_near_offsets(cfg["diff_tq"] // cfg["diff_tk"])
    diff_bias = _diff_bias_tiles(rel_bias, cfg["diff_tq"], cfg["diff_tk"], e_min, e_max)
    swa_bias = _swa_bias_tiles(rel_bias, cfg["swa_tq"])
    half = REL_BUCKETS // 2
    far = jnp.stack([rel_bias[half - 1, :DIFF_HEADS], rel_bias[2 * half - 1, :DIFF_HEADS]]) * LOG2E
    for l in range(depth):
        lam_init = 0.8 - 0.6 * math.exp(-0.3 * l)
        lw = _layer_weights(w_in[l], mla_q_norm[l], mla_kv_norm[l], mla_w_uq[l], mla_w_ukv[l])
        qm, km, vm, qd, kd, vd, sq, sk, sv, g = _in_proj(x, lw, tabs, cfg["tm"])
        om = _mla_attention(qm, km, vm, cfg["mla_tq"], cfg["mla_tk"])
        lam = _diff_lambda(diff_lambda[l], lam_init)
        od = _diff_attention(qd, kd, vd, diff_bias, far, lam, diff_subln[l].reshape(-1, 1), cfg["diff_tq"],
                             cfg["diff_tk"], lam_init)
        osw = _swa_attention(sq, sk, sv, swa_bias, swa_sink[l], cfg["swa_tq"])
        x = _out_proj(x, om, od, osw, g, w_out[l].astype(BF16), ln_g[l].reshape(1, -1), ln_b[l].reshape(1, -1),
                      cfg["tm"], alpha)
    return x
```

```python
import functools
import math

import numpy as np
import jax
import jax.numpy as jnp
from jax import lax
from jax.experimental import pallas as pl
from jax.experimental.pallas import tpu as pltpu

D_MODEL = 1024
MLA_HEADS, MLA_Q_RANK, MLA_KV_RANK, MLA_NOPE, MLA_ROPE, MLA_V = 6, 192, 128, 64, 32, 64
DIFF_HEADS, DIFF_QK, DIFF_V = 4, 32, 64
SWA_HEADS, SWA_KV_HEADS, SWA_DIM, WINDOW = 6, 2, 64, 128
W_MLA, W_DIFF, W_SWA = MLA_HEADS * MLA_V, DIFF_HEADS * DIFF_V, SWA_HEADS * SWA_DIM
D_MIX = W_MLA + W_DIFF + W_SWA
IN_SIZES = (MLA_Q_RANK, MLA_KV_RANK, MLA_ROPE, DIFF_HEADS * 2 * DIFF_QK, DIFF_HEADS * 2 * DIFF_QK,
            W_DIFF, W_SWA, SWA_KV_HEADS * SWA_DIM, SWA_KV_HEADS * SWA_DIM, D_MIX)
REL_BUCKETS, REL_MAX_DIST = 32, 128
ROPE_THETA = 10000.0
RMS_EPS, LN_EPS = 1e-6, 1e-5

LOG2E = math.log2(math.e)
LANES = 128
ONES_ROWS = 16
VMEM_LIMIT = 56 * 1024 * 1024

F32, BF16 = jnp.float32, jnp.bfloat16
NEG_INF = float("-inf")


def _cparams(sem):
    return pltpu.CompilerParams(dimension_semantics=sem, vmem_limit_bytes=VMEM_LIMIT)


def _nt(a, b):
    return lax.dot_general(a, b, (((1,), (1,)), ((), ())), preferred_element_type=F32)


def _nn(a, b):
    return jnp.dot(a, b, preferred_element_type=F32)


def _rope_tables_kernel(fcol_ref, frow_ref, cos_t_ref, sin_t_ref, cos_p_ref, sin_p_ref, *, tile):
    p0 = pl.program_id(0) * tile
    pos_l = (p0 + lax.broadcasted_iota(jnp.int32, (MLA_ROPE // 2, tile), 1)).astype(F32)
    ang_t = pos_l * fcol_ref[...]
    cos_t_ref[...] = jnp.cos(ang_t)
    sin_t_ref[...] = jnp.sin(ang_t)
    pos_s = (p0 + lax.broadcasted_iota(jnp.int32, (tile, LANES), 0)).astype(F32)
    ang_p = pos_s * frow_ref[...]
    lane = lax.broadcasted_iota(jnp.int32, (tile, LANES), 1)
    live = (lane >= MLA_NOPE) & (lane < MLA_NOPE + MLA_ROPE)
    cos_p_ref[...] = jnp.where(live, jnp.cos(ang_p), 0.0)
    sin_p_ref[...] = jnp.where(live, jnp.sin(ang_p), 0.0)


def _rope_tables(seq):
    half = MLA_ROPE // 2
    freqs = (ROPE_THETA ** (-np.arange(half, dtype=np.float64) / half)).astype(np.float32)
    fcol = jnp.asarray(freqs.reshape(half, 1))
    frow_np = np.zeros((1, LANES), np.float32)
    frow_np[0, MLA_NOPE:MLA_NOPE + half] = freqs
    frow_np[0, MLA_NOPE + half:MLA_NOPE + MLA_ROPE] = freqs
    frow = jnp.asarray(frow_np)
    tile = min(seq, 1024)
    return pl.pallas_call(
        functools.partial(_rope_tables_kernel, tile=tile),
        grid=(seq // tile,),
        in_specs=[pl.BlockSpec((half, 1), lambda i: (0, 0)), pl.BlockSpec((1, LANES), lambda i: (0, 0))],
        out_specs=[pl.BlockSpec((half, tile), lambda i: (0, i)), pl.BlockSpec((half, tile), lambda i: (0, i)),
                   pl.BlockSpec((tile, LANES), lambda i: (i, 0)), pl.BlockSpec((tile, LANES), lambda i: (i, 0))],
        out_shape=[jax.ShapeDtypeStruct((half, seq), F32), jax.ShapeDtypeStruct((half, seq), F32),
                   jax.ShapeDtypeStruct((seq, LANES), F32), jax.ShapeDtypeStruct((seq, LANES), F32)],
        compiler_params=_cparams(("arbitrary",)),
        name="rope_tables",
    )(fcol, frow)


def _t5_bucket(rel):
    half = REL_BUCKETS // 2
    max_exact = half // 2
    n = jnp.abs(rel)
    large = jnp.full(rel.shape, max_exact, jnp.int32)
    for k in range(1, half - max_exact):
        thr = max_exact * (REL_MAX_DIST / max_exact) ** (k / (half - max_exact))
        thr_i = int(round(thr)) if abs(thr - round(thr)) < 1e-9 else int(math.ceil(thr))
        large = large + (n >= thr_i).astype(jnp.int32)
    return jnp.where(rel > 0, half, 0) + jnp.where(n < max_exact, n, large)


def _bias_lookup(tab_ref, bucket, head):
    out = jnp.zeros(bucket.shape, F32)
    for b in range(REL_BUCKETS):
        out = jnp.where(bucket == b, tab_ref[b, head], out)
    return out


def _diff_bias_kernel(tab_ref, out_ref, *, tq, tk, e_min):
    h = pl.program_id(0)
    d = (pl.program_id(1) + e_min) * tk
    ki = lax.broadcasted_iota(jnp.int32, (tk, tq), 0)
    qj = lax.broadcasted_iota(jnp.int32, (tk, tq), 1)
    out_ref[0, 0] = _bias_lookup(tab_ref, _t5_bucket(d + ki - qj), h) * LOG2E


def _diff_bias_tiles(rel_bias, tq, tk, e_min, e_max):
    n_e = e_max - e_min + 1
    return pl.pallas_call(
        functools.partial(_diff_bias_kernel, tq=tq, tk=tk, e_min=e_min),
        grid=(DIFF_HEADS, n_e),
        in_specs=[pl.BlockSpec(memory_space=pltpu.SMEM)],
        out_specs=pl.BlockSpec((1, 1, tk, tq), lambda h, d: (h, d, 0, 0)),
        out_shape=jax.ShapeDtypeStruct((DIFF_HEADS, n_e, tk, tq), F32),
        compiler_params=_cparams(("arbitrary", "arbitrary")),
        name="diff_bias_tiles",
    )(rel_bias)


def _swa_bias_kernel(tab_ref, out_ref):
    variant = pl.program_id(0)
    h = pl.program_id(1)
    ki = lax.broadcasted_iota(jnp.int32, (3 * WINDOW, WINDOW), 0)
    qj = lax.broadcasted_iota(jnp.int32, (3 * WINDOW, WINDOW), 1)
    rel = ki - WINDOW - qj
    bias = _bias_lookup(tab_ref, _t5_bucket(rel), DIFF_HEADS + h) * LOG2E
    lo = jnp.where(variant == 1, WINDOW, 0)
    hi = jnp.where(variant == 2, 2 * WINDOW, 3 * WINDOW)
    live = (jnp.abs(rel) <= WINDOW) & (ki >= lo) & (ki < hi)
    out_ref[0, 0] = jnp.where(live, bias, NEG_INF)


def _swa_bias_tiles(rel_bias):
    return pl.pallas_call(
        _swa_bias_kernel,
        grid=(3, SWA_HEADS),
        in_specs=[pl.BlockSpec(memory_space=pltpu.SMEM)],
        out_specs=pl.BlockSpec((1, 1, 3 * WINDOW, WINDOW), lambda v, h: (v, h, 0, 0)),
        out_shape=jax.ShapeDtypeStruct((3, SWA_HEADS, 3 * WINDOW, WINDOW), F32),
        compiler_params=_cparams(("arbitrary", "arbitrary")),
        name="swa_bias_tiles",
    )(rel_bias)


_T_CQ, _T_CKV, _T_DQ, _T_DV, _T_SQ, _T_SV, _T_GATE = 0, 192, 320, 576, 832, 1216, 1344
_T_ROWS = 2368
_R_CKV, _R_KR, _R_KRROT, _R_DK, _R_SK = 0, 128, 256, 384, 896
_R_COLS = 1024


def _rms_t(v, g_col):
    ms = jnp.mean(v * v, axis=0, keepdims=True)
    return v * lax.rsqrt(ms + RMS_EPS) * g_col


def _in_proj_kernel(x_ref, wt_ref, wr_ref, wuq_ref, wkp_ref, wvt_ref, gq_ref, gkv_col_ref, gkv_row_ref,
                    cos_t_ref, sin_t_ref, cos_p_ref, sin_p_ref, ones_ref,
                    qm_ref, km_ref, vm_ref, qd_ref, kd_ref, vd_ref, sq_ref, sk_ref, sv_ref, g_ref):
    xb = x_ref[0].astype(BF16)

    def proj_t(lo, hi):
        return _nt(wt_ref[lo:hi, :], xb)

    cqn = _rms_t(proj_t(_T_CQ, _T_CKV), gq_ref[...]).astype(BF16)
    q_all = _nn(wuq_ref[...], cqn)
    cos_t, sin_t = cos_t_ref[...], sin_t_ref[...]
    q_scale = LOG2E / math.sqrt(MLA_NOPE + MLA_ROPE)
    half = MLA_ROPE // 2
    for h in range(MLA_HEADS):
        blk = q_all[h * LANES:(h + 1) * LANES]
        x1 = blk[MLA_NOPE:MLA_NOPE + half]
        x2 = blk[MLA_NOPE + half:MLA_NOPE + MLA_ROPE]
        roped = jnp.concatenate([blk[:MLA_NOPE], x1 * cos_t - x2 * sin_t, x2 * cos_t + x1 * sin_t,
                                 blk[MLA_NOPE + MLA_ROPE:]], axis=0)
        qm_ref[0, h] = (roped * q_scale).astype(BF16)

    ckvn_t = _rms_t(proj_t(_T_CKV, _T_DQ), gkv_col_ref[...]).astype(BF16)
    v_all = _nn(wvt_ref[...], ckvn_t).astype(BF16)
    for h in range(MLA_HEADS):
        vm_ref[0, h] = v_all[h * MLA_V:(h + 1) * MLA_V]

    r = _nn(xb, wr_ref[...])
    ckv = r[:, _R_CKV:_R_KR]
    ms = jnp.mean(ckv * ckv, axis=1, keepdims=True)
    ckvn = (ckv * lax.rsqrt(ms + RMS_EPS) * gkv_row_ref[...]).astype(BF16)
    k_nope = _nn(ckvn, wkp_ref[...])
    k_rope = r[:, _R_KR:_R_KRROT] * cos_p_ref[...] + r[:, _R_KRROT:_R_DK] * sin_p_ref[...]
    for h in range(MLA_HEADS):
        km_ref[0, h] = (k_nope[:, h * LANES:(h + 1) * LANES] + k_rope).astype(BF16)
    ones_row = ones_ref[...]
    for h in range(DIFF_HEADS):
        kd_ref[0, h] = (r[:, _R_DK + h * LANES:_R_DK + (h + 1) * LANES] + ones_row).astype(BF16)
    sk_ref[0] = r[:, _R_SK:_R_COLS].astype(BF16)

    qd_ref[0] = (proj_t(_T_DQ, _T_DV) * (LOG2E / math.sqrt(DIFF_QK))).astype(BF16)
    vd_ref[0] = proj_t(_T_DV, _T_SQ).astype(BF16)
    sq_ref[0] = (proj_t(_T_SQ, _T_SV) * (LOG2E / math.sqrt(SWA_DIM))).astype(BF16)
    sv_ref[0] = proj_t(_T_SV, _T_GATE).astype(BF16)
    gate = proj_t(_T_GATE, _T_ROWS)
    g_ref[0] = (gate * jax.nn.sigmoid(gate)).astype(BF16)


def _in_proj(x, lw, tabs, tm):
    b, s, d = x.shape
    cos_t, sin_t, cos_p, sin_p = tabs
    full = lambda a: pl.BlockSpec(a.shape, lambda bi, i: (0,) * a.ndim)
    ones_np = np.zeros((1, LANES), np.float32)
    ones_np[0, 2 * DIFF_QK:2 * DIFF_QK + 2] = 1.0
    ones_row = jnp.asarray(ones_np)
    half = MLA_ROPE // 2
    ins = [x, lw["w_t"], lw["w_r"], lw["w_uq_t"], lw["w_kp"], lw["w_v_t"], lw["gq_col"], lw["gkv_col"],
           lw["gkv_row"], cos_t, sin_t, cos_p, sin_p, ones_row]
    in_specs = [pl.BlockSpec((1, tm, d), lambda bi, i: (bi, i, 0))]
    in_specs += [full(a) for a in ins[1:9]]
    in_specs += [pl.BlockSpec((half, tm), lambda bi, i: (0, i)), pl.BlockSpec((half, tm), lambda bi, i: (0, i)),
                 pl.BlockSpec((tm, LANES), lambda bi, i: (i, 0)), pl.BlockSpec((tm, LANES), lambda bi, i: (i, 0)),
                 full(ones_row)]
    sds = jax.ShapeDtypeStruct
    out_shape = [
        sds((b, MLA_HEADS, LANES, s), BF16),
        sds((b, MLA_HEADS, s, LANES), BF16),
        sds((b, MLA_HEADS, MLA_V, s), BF16),
        sds((b, DIFF_HEADS * 2 * DIFF_QK, s), BF16),
        sds((b, DIFF_HEADS, s, LANES), BF16),
        sds((b, W_DIFF, s), BF16),
        sds((b, W_SWA, s), BF16),
        sds((b, s, LANES), BF16),
        sds((b, SWA_KV_HEADS * SWA_DIM, s), BF16),
        sds((b, D_MIX, s), BF16),
    ]
    out_specs = [
        pl.BlockSpec((1, MLA_HEADS, LANES, tm), lambda bi, i: (bi, 0, 0, i)),
        pl.BlockSpec((1, MLA_HEADS, tm, LANES), lambda bi, i: (bi, 0, i, 0)),
        pl.BlockSpec((1, MLA_HEADS, MLA_V, tm), lambda bi, i: (bi, 0, 0, i)),
        pl.BlockSpec((1, DIFF_HEADS * 2 * DIFF_QK, tm), lambda bi, i: (bi, 0, i)),
        pl.BlockSpec((1, DIFF_HEADS, tm, LANES), lambda bi, i: (bi, 0, i, 0)),
        pl.BlockSpec((1, W_DIFF, tm), lambda bi, i: (bi, 0, i)),
        pl.BlockSpec((1, W_SWA, tm), lambda bi, i: (bi, 0, i)),
        pl.BlockSpec((1, tm, LANES), lambda bi, i: (bi, i, 0)),
        pl.BlockSpec((1, SWA_KV_HEADS * SWA_DIM, tm), lambda bi, i: (bi, 0, i)),
        pl.BlockSpec((1, D_MIX, tm), lambda bi, i: (bi, 0, i)),
    ]
    return pl.pallas_call(
        _in_proj_kernel,
        grid=(b, s // tm),
        in_specs=in_specs,
        out_specs=out_specs,
        out_shape=out_shape,
        compiler_params=_cparams(("parallel", "parallel")),
        name="in_proj",
    )(*ins)


def _with_ones(v_t):
    tk = v_t.shape[1]
    row = lax.broadcasted_iota(jnp.int32, (ONES_ROWS, tk), 0)
    return jnp.concatenate([v_t, jnp.where(row == 0, 1.0, 0.0).astype(v_t.dtype)], axis=0)


def _online_step(s_t, tile_max, v_ext, m, acc):
    m_new = jnp.maximum(m, tile_max)
    p = jnp.exp2(s_t - m_new).astype(BF16)
    alpha = jnp.exp2(m - m_new)
    return m_new, alpha * acc + _nn(v_ext, p)


def _tile_off(j, t):
    return j * t if isinstance(j, int) else pl.multiple_of(j * t, t)


PIPE_TILES = 2
COL_BLOCK = 256
MAX_UNROLLED_TRIPS = 16


def _pipelined_tiles(n_tiles, n_sub, produce_ctx, produce, consume_ctx, consume, carry, peel_first=False):
    u_n = PIPE_TILES
    trip = 2 * u_n
    assert n_tiles % trip == 0 and n_tiles >= trip

    def fill(slot, v0):
        for u in range(u_n):
            ctx = produce_ctx(v0 + u)
            for c in range(n_sub):
                produce(slot, u, c, ctx)

    def half(c_slot, v_c, p_slot, v_p, carry):
        carry = list(carry)
        for u in range(u_n):
            pctx = None if p_slot is None else produce_ctx(v_p + u)
            cctx = consume_ctx(v_c + u)
            for c in range(n_sub):
                carry[c] = consume(c_slot, u, c, cctx, carry[c])
                if p_slot is not None:
                    produce(p_slot, u, c, pctx)
        return tuple(carry)

    def body(v0, carry, last):
        carry = half(0, v0, 1, v0 + u_n, carry)
        return half(1, v0 + u_n, None if last else 0, v0 + trip, carry)

    fill(0, 0)
    n_trips = n_tiles // trip
    if n_trips <= MAX_UNROLLED_TRIPS:
        for i in range(n_trips):
            carry = body(i * trip, carry, i == n_trips - 1)
        return carry
    first = 0
    if peel_first:
        carry = body(0, carry, False)
        first = 1
    carry = lax.fori_loop(first, n_trips - 1, lambda i, c: body(i * trip, c, False), carry)
    return body((n_trips - 1) * trip, carry, True)


def _score_scratch(n_sub, tk):
    n = 2 * PIPE_TILES * n_sub
    return [pltpu.VMEM((1, tk, COL_BLOCK), F32)] * n + [pltpu.VMEM((1, 1, COL_BLOCK), F32)] * n


def _score_refs(bufs, n_sub):
    n = 2 * PIPE_TILES * n_sub
    z = jnp.minimum(pl.program_id(0), 0)
    at = lambda slot, u, c: (slot * PIPE_TILES + u) * n_sub + c
    return (lambda slot, u, c: bufs[at(slot, u, c)].at[z]), (lambda slot, u, c: bufs[n + at(slot, u, c)].at[z])


def _mla_attn_kernel(q_ref, k_ref, v_ref, o_ref, *bufs, tk, nk):
    tq = q_ref.shape[3]
    n_sub = tq // COL_BLOCK
    s_buf, mx_buf = _score_refs(bufs, n_sub)
    q_blk = [q_ref[0, 0, :, c * COL_BLOCK:(c + 1) * COL_BLOCK] for c in range(n_sub)]

    def produce_ctx(j):
        return k_ref[0, 0, pl.ds(_tile_off(j, tk), tk), :]

    def produce(slot, u, c, k):
        s_t = _nn(k, q_blk[c])
        s_buf(slot, u, c)[...] = s_t
        mx_buf(slot, u, c)[...] = jnp.max(s_t, axis=0, keepdims=True)

    def consume_ctx(j):
        return _with_ones(v_ref[0, 0, :, pl.ds(_tile_off(j, tk), tk)])

    def consume(slot, u, c, v_ext, carry):
        return _online_step(s_buf(slot, u, c)[...], mx_buf(slot, u, c)[...], v_ext, *carry)

    m0 = jnp.full((1, COL_BLOCK), NEG_INF, F32)
    acc0 = jnp.zeros((MLA_V + ONES_ROWS, COL_BLOCK), F32)
    carry = _pipelined_tiles(nk, n_sub, produce_ctx, produce, consume_ctx, consume, ((m0, acc0),) * n_sub)
    for c, (_, acc) in enumerate(carry):
        o_ref[0, :, c * COL_BLOCK:(c + 1) * COL_BLOCK] = (acc[:MLA_V] / acc[MLA_V:MLA_V + 1]).astype(o_ref.dtype)


def _mla_attention(qm, km, vm, tq, tk):
    b, h, _, s = qm.shape
    return pl.pallas_call(
        functools.partial(_mla_attn_kernel, tk=tk, nk=s // tk),
        grid=(b, h, s // tq),
        in_specs=[pl.BlockSpec((1, 1, LANES, tq), lambda bi, hi, qi: (bi, hi, 0, qi)),
                  pl.BlockSpec((1, 1, s, LANES), lambda bi, hi, qi: (bi, hi, 0, 0)),
                  pl.BlockSpec((1, 1, MLA_V, s), lambda bi, hi, qi: (bi, hi, 0, 0))],
        out_specs=pl.BlockSpec((1, MLA_V, tq), lambda bi, hi, qi: (bi, hi, qi)),
        out_shape=jax.ShapeDtypeStruct((b, W_MLA, s), BF16),
        scratch_shapes=_score_scratch(tq // COL_BLOCK, tk),
        compiler_params=_cparams(("parallel", "parallel", "arbitrary")),
        name="mla_attention",
    )(qm, km, vm)


NEAR_TILES = 4


def _diff_near_offsets(r):
    return min(-1, r - NEAR_TILES), NEAR_TILES - 1


def _diff_attn_kernel(far_ref, lam_ref, q_ref, k_ref, v_ref, bias_ref, subln_ref, o_ref,
                      qfar_buf, *bufs, tq, tk, nk, lam_init):
    s_buf, mx_buf = _score_refs(bufs, 2 * (tq // COL_BLOCK))
    h = pl.program_id(1)
    qi = pl.program_id(2)
    t = tq
    r = tq // tk
    q_t = q_ref[0]
    r16 = lax.broadcasted_iota(jnp.int32, (ONES_ROWS, t), 0)
    zero_map = jnp.zeros((DIFF_QK, t), BF16)
    zero_pad = jnp.zeros((LANES - 2 * DIFF_QK - ONES_ROWS, t), BF16)

    def q_aug(mp, side):
        own = q_t[mp * DIFF_QK:(mp + 1) * DIFF_QK]
        own = [own, zero_map] if mp == 0 else [zero_map, own]
        if side is None:
            aug = jnp.zeros((ONES_ROWS, t), BF16)
        else:
            c = jnp.full((ONES_ROWS, t), far_ref[side, h], F32)
            c_hi = c.astype(BF16).astype(F32)
            aug = jnp.where(r16 == 0, c_hi, jnp.where(r16 == 1, c - c_hi, 0.0)).astype(BF16)
        return jnp.concatenate(own + [aug, zero_pad], axis=0)

    near_lo = jnp.clip(r * qi - 1, 0, nk - NEAR_TILES)
    e_min, _ = _diff_near_offsets(r)
    q_near = [q_aug(0, None), q_aug(1, None)]
    for side in range(2):
        for mp in range(2):
            qfar_buf[side, mp] = q_aug(mp, side)

    def key_tile(v):
        if isinstance(v, int) and v < NEAR_TILES:
            return near_lo + v, True, None
        idx = v - NEAR_TILES
        side = (idx >= near_lo).astype(jnp.int32)
        return idx + NEAR_TILES * side, False, side

    n_cb = tq // COL_BLOCK
    cols = lambda cb: slice(cb * COL_BLOCK, (cb + 1) * COL_BLOCK)

    def produce_ctx(v):
        j, near, side = key_tile(v)
        return k_ref[0, 0, pl.ds(_tile_off(j, tk), tk), :], near, side, j

    def produce(slot, u, c, ctx):
        k, near, side, j = ctx
        mp, cb = divmod(c, n_cb)
        if near:
            s_t = _nn(k, q_near[mp][:, cols(cb)]) + bias_ref[0, j - r * qi - e_min, :, cols(cb)]
        else:
            s_t = _nn(k, qfar_buf[side, mp, :, cols(cb)])
        s_buf(slot, u, c)[...] = s_t
        mx_buf(slot, u, c)[...] = jnp.max(s_t, axis=0, keepdims=True)

    def consume_ctx(v):
        j, _, _ = key_tile(v)
        return _with_ones(v_ref[0, :, pl.ds(_tile_off(j, tk), tk)])

    def consume(slot, u, c, v_ext, carry):
        return _online_step(s_buf(slot, u, c)[...], mx_buf(slot, u, c)[...], v_ext, *carry)

    m0 = jnp.full((1, COL_BLOCK), NEG_INF, F32)
    acc0 = jnp.zeros((DIFF_V + ONES_ROWS, COL_BLOCK), F32)
    carry = _pipelined_tiles(nk, 2 * n_cb, produce_ctx, produce, consume_ctx, consume,
                             ((m0, acc0),) * (2 * n_cb), peel_first=True)

    lam = lam_ref[0, 0]
    for cb in range(n_cb):
        a0, a1 = carry[cb][1], carry[n_cb + cb][1]
        o = a0[:DIFF_V] / a0[DIFF_V:DIFF_V + 1] - lam * (a1[:DIFF_V] / a1[DIFF_V:DIFF_V + 1])
        ms = jnp.mean(o * o, axis=0, keepdims=True)
        o = o * lax.rsqrt(ms + RMS_EPS) * subln_ref[...] * (1.0 - lam_init)
        o_ref[0, :, cols(cb)] = o.astype(o_ref.dtype)


def _diff_attention(qd, kd, vd, bias_tiles, far, lam, subln_col, tq, tk, lam_init):
    b, _, s = qd.shape
    n_e = bias_tiles.shape[1]
    assert tq % tk == 0 and tq // tk + 2 <= NEAR_TILES and tk >= REL_MAX_DIST
    smem = pl.BlockSpec(memory_space=pltpu.SMEM)
    return pl.pallas_call(
        functools.partial(_diff_attn_kernel, tq=tq, tk=tk, nk=s // tk, lam_init=lam_init),
        grid=(b, DIFF_HEADS, s // tq),
        in_specs=[smem, smem,
                  pl.BlockSpec((1, 2 * DIFF_QK, tq), lambda bi, hi, qi: (bi, hi, qi)),
                  pl.BlockSpec((1, 1, s, LANES), lambda bi, hi, qi: (bi, hi, 0, 0)),
                  pl.BlockSpec((1, DIFF_V, s), lambda bi, hi, qi: (bi, hi, 0)),
                  pl.BlockSpec((1, n_e, tk, tq), lambda bi, hi, qi: (hi, 0, 0, 0)),
                  pl.BlockSpec((DIFF_V, 1), lambda bi, hi, qi: (0, 0))],
        out_specs=pl.BlockSpec((1, DIFF_V, tq), lambda bi, hi, qi: (bi, hi, qi)),
        out_shape=jax.ShapeDtypeStruct((b, W_DIFF, s), BF16),
        scratch_shapes=[pltpu.VMEM((2, 2, LANES, tq), BF16)] + _score_scratch(2 * (tq // COL_BLOCK), tk),
        compiler_params=_cparams(("parallel", "parallel", "arbitrary")),
        name="diff_attention",
    )(far, lam, qd, kd, vd, bias_tiles, subln_col)


def _swa_attn_kernel(sink_ref, q_ref, k0_ref, k1_ref, k2_ref, k3_ref, v0_ref, v1_ref, v2_ref, v3_ref,
                     bias_ref, o_ref, *, nq):
    qi = pl.program_id(1)
    kb = [k0_ref[0], k1_ref[0], k2_ref[0], k3_ref[0]]
    vb = [v0_ref[0], v1_ref[0], v2_ref[0], v3_ref[0]]
    n_sb = len(kb) - 2
    k_win = [jnp.concatenate(kb[sb:sb + 3], axis=0) for sb in range(n_sb)]
    v_win = [jnp.concatenate(vb[sb:sb + 3], axis=1) for sb in range(n_sb)]
    variant = [jnp.where(qi == 0, 1, 0) if sb == 0 else jnp.where(qi == nq - 1, 2, 0) if sb == n_sb - 1 else 0
               for sb in range(n_sb)]
    grp = SWA_HEADS // SWA_KV_HEADS
    zeros = jnp.zeros((SWA_DIM, WINDOW), BF16)
    chains = [(hq, sb) for hq in range(SWA_HEADS) for sb in range(n_sb)]
    cols = lambda sb: slice(sb * WINDOW, (sb + 1) * WINDOW)
    s_t, p_t, den = {}, {}, {}
    for hq, sb in chains:
        q_t = q_ref[0, hq * SWA_DIM:(hq + 1) * SWA_DIM, cols(sb)]
        q_pad = jnp.concatenate([q_t, zeros] if hq // grp == 0 else [zeros, q_t], axis=0)
        s_t[hq, sb] = _nn(k_win[sb], q_pad) + bias_ref[variant[sb], hq]
    for hq, sb in chains:
        sink = sink_ref[hq] * LOG2E
        m = jnp.maximum(jnp.max(s_t[hq, sb], axis=0, keepdims=True), sink)
        p = jnp.exp2(s_t[hq, sb] - m)
        den[hq, sb] = jnp.sum(p, axis=0, keepdims=True) + jnp.exp2(sink - m)
        p_t[hq, sb] = p.astype(BF16)
    for hq, sb in chains:
        g = hq // grp
        o = _nn(v_win[sb][g * SWA_DIM:(g + 1) * SWA_DIM], p_t[hq, sb]) / den[hq, sb]
        o_ref[0, hq * SWA_DIM:(hq + 1) * SWA_DIM, cols(sb)] = o.astype(o_ref.dtype)


def _swa_attention(sq, sk, sv, bias_tiles, sink, tq):
    b, _, s = sq.shape
    assert tq % WINDOW == 0
    r = tq // WINDOW
    nblk = s // WINDOW
    smem = pl.BlockSpec(memory_space=pltpu.SMEM)
    wk = tq + 2 * WINDOW
    nw = wk // WINDOW

    def kspec(c):
        return pl.BlockSpec((1, WINDOW, LANES),
                            lambda bi, qi: (bi, jnp.clip(qi * r - 1 + c, 0, nblk - 1), 0))

    def vspec(c):
        return pl.BlockSpec((1, SWA_KV_HEADS * SWA_DIM, WINDOW),
                            lambda bi, qi: (bi, 0, jnp.clip(qi * r - 1 + c, 0, nblk - 1)))

    assert nw == 4
    return pl.pallas_call(
        functools.partial(_swa_attn_kernel, nq=s // tq),
        grid=(b, s // tq),
        in_specs=[smem, pl.BlockSpec((1, W_SWA, tq), lambda bi, qi: (bi, 0, qi))]
        + [kspec(c) for c in range(nw)] + [vspec(c) for c in range(nw)]
        + [pl.BlockSpec(bias_tiles.shape, lambda bi, qi: (0, 0, 0, 0))],
        out_specs=pl.BlockSpec((1, W_SWA, tq), lambda bi, qi: (bi, 0, qi)),
        out_shape=jax.ShapeDtypeStruct((b, W_SWA, s), BF16),
        compiler_params=_cparams(("parallel", "parallel")),
        name="swa_attention",
    )(sink, sq, sk, sk, sk, sk, sv, sv, sv, sv, bias_tiles)


def _out_proj_kernel(x_ref, om_ref, od_ref, os_ref, g_ref, w_ref, lng_ref, lnb_ref, o_ref, *, alpha):
    o_t = jnp.concatenate([om_ref[0], od_ref[0], os_ref[0]], axis=0)
    og_t = o_t * g_ref[0]
    y = lax.dot_general(og_t, w_ref[...], (((0,), (0,)), ((), ())), preferred_element_type=F32)
    z = alpha * x_ref[0] + y
    mu = jnp.mean(z, axis=1, keepdims=True)
    zc = z - mu
    var = jnp.mean(zc * zc, axis=1, keepdims=True)
    o_ref[0] = zc * lax.rsqrt(var + LN_EPS) * lng_ref[...] + lnb_ref[...]


def _out_proj(x, om, od, osw, g, w_out, ln_g, ln_b, tm, alpha):
    b, s, d = x.shape
    tspec = lambda rows: pl.BlockSpec((1, rows, tm), lambda bi, i: (bi, 0, i))
    return pl.pallas_call(
        functools.partial(_out_proj_kernel, alpha=alpha),
        grid=(b, s // tm),
        in_specs=[pl.BlockSpec((1, tm, d), lambda bi, i: (bi, i, 0)),
                  tspec(W_MLA), tspec(W_DIFF), tspec(W_SWA), tspec(D_MIX),
                  pl.BlockSpec((D_MIX, d), lambda bi, i: (0, 0)),
                  pl.BlockSpec((1, d), lambda bi, i: (0, 0)), pl.BlockSpec((1, d), lambda bi, i: (0, 0))],
        out_specs=pl.BlockSpec((1, tm, d), lambda bi, i: (bi, i, 0)),
        out_shape=jax.ShapeDtypeStruct((b, s, d), F32),
        compiler_params=_cparams(("parallel", "parallel")),
        name="out_proj",
    )(x, om, od, osw, g, w_out, ln_g, ln_b)


def _layer_weights(w_in, q_norm, kv_norm, w_uq, w_ukv):
    cq, ckv, kr, dq, dk, dv, sq, sk, sv, gate = jnp.split(w_in, np.cumsum(IN_SIZES)[:-1].tolist(), axis=1)
    d = w_in.shape[0]
    half = MLA_ROPE // 2
    w_t = jnp.concatenate([cq, ckv, dq, dv, sq, sv, gate], axis=1).T.astype(BF16)
    pad = lambda a, lo, width: jnp.pad(a, ((0, 0), (lo, width - lo - a.shape[1])))
    kr_p = pad(kr, MLA_NOPE, LANES)
    kr_rot_p = pad(jnp.concatenate([-kr[:, half:], kr[:, :half]], axis=1), MLA_NOPE, LANES)
    dk_p = jnp.concatenate([pad(dk[:, h * 2 * DIFF_QK:(h + 1) * 2 * DIFF_QK], 0, LANES)
                            for h in range(DIFF_HEADS)], axis=1)
    w_r = jnp.concatenate([ckv, kr_p, kr_rot_p, dk_p, sk], axis=1).astype(BF16)
    assert w_t.shape == (_T_ROWS, d) and w_r.shape == (d, _R_COLS)
    qk = MLA_NOPE + MLA_ROPE
    w_uq_t = jnp.concatenate([jnp.pad(w_uq[:, h * qk:(h + 1) * qk].T, ((0, LANES - qk), (0, 0)))
                              for h in range(MLA_HEADS)], axis=0).astype(BF16)
    kv = MLA_NOPE + MLA_V
    w_kp = jnp.concatenate([pad(w_ukv[:, h * kv:h * kv + MLA_NOPE], 0, LANES)
                            for h in range(MLA_HEADS)], axis=1).astype(BF16)
    w_v_t = jnp.concatenate([w_ukv[:, h * kv + MLA_NOPE:(h + 1) * kv].T
                             for h in range(MLA_HEADS)], axis=0).astype(BF16)
    return dict(w_t=w_t, w_r=w_r, w_uq_t=w_uq_t, w_kp=w_kp, w_v_t=w_v_t,
                gq_col=q_norm.reshape(-1, 1), gkv_col=kv_norm.reshape(-1, 1), gkv_row=kv_norm.reshape(1, -1))


def _lambda_kernel(lv_ref, o_ref, *, lam_init):
    lv = lv_ref[...]
    a = jnp.sum(lv[0:1] * lv[1:2], axis=1, keepdims=True)
    c = jnp.sum(lv[2:3] * lv[3:4], axis=1, keepdims=True)
    o_ref[...] = jnp.exp(a) - jnp.exp(c) + lam_init


def _diff_lambda(lam_vecs, lam_init):
    return pl.pallas_call(
        functools.partial(_lambda_kernel, lam_init=lam_init),
        out_shape=jax.ShapeDtypeStruct((1, 1), F32),
        name="diff_lambda",
    )(lam_vecs)


def _tiles(seq):
    return dict(tm=min(512, seq), mla_tq=min(1024, seq), mla_tk=256, diff_tq=min(512, seq), diff_tk=256, swa_tq=256)


def kernel(x, w_in, mla_q_norm, mla_kv_norm, mla_w_uq, mla_w_ukv, diff_lambda, diff_subln, swa_sink,
           rel_bias, w_out, ln_g, ln_b):
    depth = w_in.shape[0]
    seq = x.shape[1]
    cfg = _tiles(seq)
    alpha = (2 * depth) ** 0.25
    tabs = _rope_tables(seq)
    e_min, e_max = _diff_near_offsets(cfg["diff_tq"] // cfg["diff_tk"])
    diff_bias = _diff_bias_tiles(rel_bias, cfg["diff_tq"], cfg["diff_tk"], e_min, e_max)
    swa_bias = _swa_bias_tiles(rel_bias)
    half = REL_BUCKETS // 2
    far = jnp.stack([rel_bias[half - 1, :DIFF_HEADS], rel_bias[2 * half - 1, :DIFF_HEADS]]) * LOG2E
    for l in range(depth):
        lam_init = 0.8 - 0.6 * math.exp(-0.3 * l)
        lw = _layer_weights(w_in[l], mla_q_norm[l], mla_kv_norm[l], mla_w_uq[l], mla_w_ukv[l])
        qm, km, vm, qd, kd, vd, sq, sk, sv, g = _in_proj(x, lw, tabs, cfg["tm"])
        om = _mla_attention(qm, km, vm, cfg["mla_tq"], cfg["mla_tk"])
        lam = _diff_lambda(diff_lambda[l], lam_init)
        od = _diff_attention(qd, kd, vd, diff_bias, far, lam, diff_subln[l].reshape(-1, 1), cfg["diff_tq"],
                             cfg["diff_tk"], lam_init)
        osw = _swa_attention(sq, sk, sv, swa_bias, swa_sink[l], cfg["swa_tq"])
        x = _out_proj(x, om, od, osw, g, w_out[l].astype(BF16), ln_g[l].reshape(1, -1), ln_b[l].reshape(1, -1),
                      cfg["tm"], alpha)
    return x
```

```python
import functools
import math

import numpy as np
import jax
import jax.numpy as jnp
from jax import lax
from jax.experimental import pallas as pl
from jax.experimental.pallas import tpu as pltpu

D_MODEL = 1024
MLA_HEADS, MLA_Q_RANK, MLA_KV_RANK, MLA_NOPE, MLA_ROPE, MLA_V = 6, 192, 128, 64, 32, 64
DIFF_HEADS, DIFF_QK, DIFF_V = 4, 32, 64
SWA_HEADS, SWA_KV_HEADS, SWA_DIM, WINDOW = 6, 2, 64, 128
W_MLA, W_DIFF, W_SWA = MLA_HEADS * MLA_V, DIFF_HEADS * DIFF_V, SWA_HEADS * SWA_DIM
D_MIX = W_MLA + W_DIFF + W_SWA
IN_SIZES = (MLA_Q_RANK, MLA_KV_RANK, MLA_ROPE, DIFF_HEADS * 2 * DIFF_QK, DIFF_HEADS * 2 * DIFF_QK,
            W_DIFF, W_SWA, SWA_KV_HEADS * SWA_DIM, SWA_KV_HEADS * SWA_DIM, D_MIX)
REL_BUCKETS, REL_MAX_DIST = 32, 128
ROPE_THETA = 10000.0
RMS_EPS, LN_EPS = 1e-6, 1e-5

LOG2E = math.log2(math.e)
LANES = 128
ONES_ROWS = 16
VMEM_LIMIT = 56 * 1024 * 1024

F32, BF16 = jnp.float32, jnp.bfloat16
NEG_INF = float("-inf")


def _cparams(sem):
    return pltpu.CompilerParams(dimension_semantics=sem, vmem_limit_bytes=VMEM_LIMIT)


def _nt(a, b):
    return lax.dot_general(a, b, (((1,), (1,)), ((), ())), preferred_element_type=F32)


def _nn(a, b):
    return jnp.dot(a, b, preferred_element_type=F32)


def _rope_tables_kernel(fcol_ref, frow_ref, cos_t_ref, sin_t_ref, cos_p_ref, sin_p_ref, *, tile):
    p0 = pl.program_id(0) * tile
    pos_l = (p0 + lax.broadcasted_iota(jnp.int32, (MLA_ROPE // 2, tile), 1)).astype(F32)
    ang_t = pos_l * fcol_ref[...]
    cos_t_ref[...] = jnp.cos(ang_t)
    sin_t_ref[...] = jnp.sin(ang_t)
    pos_s = (p0 + lax.broadcasted_iota(jnp.int32, (tile, LANES), 0)).astype(F32)
    ang_p = pos_s * frow_ref[...]
    lane = lax.broadcasted_iota(jnp.int32, (tile, LANES), 1)
    live = (lane >= MLA_NOPE) & (lane < MLA_NOPE + MLA_ROPE)
    cos_p_ref[...] = jnp.where(live, jnp.cos(ang_p), 0.0)
    sin_p_ref[...] = jnp.where(live, jnp.sin(ang_p), 0.0)


def _rope_tables(seq):
    half = MLA_ROPE // 2
    freqs = (ROPE_THETA ** (-np.arange(half, dtype=np.float64) / half)).astype(np.float32)
    fcol = jnp.asarray(freqs.reshape(half, 1))
    frow_np = np.zeros((1, LANES), np.float32)
    frow_np[0, MLA_NOPE:MLA_NOPE + half] = freqs
    frow_np[0, MLA_NOPE + half:MLA_NOPE + MLA_ROPE] = freqs
    frow = jnp.asarray(frow_np)
    tile = min(seq, 1024)
    return pl.pallas_call(
        functools.partial(_rope_tables_kernel, tile=tile),
        grid=(seq // tile,),
        in_specs=[pl.BlockSpec((half, 1), lambda i: (0, 0)), pl.BlockSpec((1, LANES), lambda i: (0, 0))],
        out_specs=[pl.BlockSpec((half, tile), lambda i: (0, i)), pl.BlockSpec((half, tile), lambda i: (0, i)),
                   pl.BlockSpec((tile, LANES), lambda i: (i, 0)), pl.BlockSpec((tile, LANES), lambda i: (i, 0))],
        out_shape=[jax.ShapeDtypeStruct((half, seq), F32), jax.ShapeDtypeStruct((half, seq), F32),
                   jax.ShapeDtypeStruct((seq, LANES), F32), jax.ShapeDtypeStruct((seq, LANES), F32)],
        compiler_params=_cparams(("arbitrary",)),
        name="rope_tables",
    )(fcol, frow)


def _t5_bucket(rel):
    half = REL_BUCKETS // 2
    max_exact = half // 2
    n = jnp.abs(rel)
    large = jnp.full(rel.shape, max_exact, jnp.int32)
    for k in range(1, half - max_exact):
        thr = max_exact * (REL_MAX_DIST / max_exact) ** (k / (half - max_exact))
        thr_i = int(round(thr)) if abs(thr - round(thr)) < 1e-9 else int(math.ceil(thr))
        large = large + (n >= thr_i).astype(jnp.int32)
    return jnp.where(rel > 0, half, 0) + jnp.where(n < max_exact, n, large)


def _bias_lookup(tab_ref, bucket, head):
    out = jnp.zeros(bucket.shape, F32)
    for b in range(REL_BUCKETS):
        out = jnp.where(bucket == b, tab_ref[b, head], out)
    return out


def _diff_bias_kernel(tab_ref, out_ref, *, tq, tk, e_min):
    h = pl.program_id(0)
    d = (pl.program_id(1) + e_min) * tk
    ki = lax.broadcasted_iota(jnp.int32, (tk, tq), 0)
    qj = lax.broadcasted_iota(jnp.int32, (tk, tq), 1)
    out_ref[0, 0] = _bias_lookup(tab_ref, _t5_bucket(d + ki - qj), h) * LOG2E


def _diff_bias_tiles(rel_bias, tq, tk, e_min, e_max):
    n_e = e_max - e_min + 1
    return pl.pallas_call(
        functools.partial(_diff_bias_kernel, tq=tq, tk=tk, e_min=e_min),
        grid=(DIFF_HEADS, n_e),
        in_specs=[pl.BlockSpec(memory_space=pltpu.SMEM)],
        out_specs=pl.BlockSpec((1, 1, tk, tq), lambda h, d: (h, d, 0, 0)),
        out_shape=jax.ShapeDtypeStruct((DIFF_HEADS, n_e, tk, tq), F32),
        compiler_params=_cparams(("arbitrary", "arbitrary")),
        name="diff_bias_tiles",
    )(rel_bias)


def _swa_bias_kernel(tab_ref, out_ref):
    variant = pl.program_id(0)
    h = pl.program_id(1)
    ki = lax.broadcasted_iota(jnp.int32, (3 * WINDOW, WINDOW), 0)
    qj = lax.broadcasted_iota(jnp.int32, (3 * WINDOW, WINDOW), 1)
    rel = ki - WINDOW - qj
    bias = _bias_lookup(tab_ref, _t5_bucket(rel), DIFF_HEADS + h) * LOG2E
    lo = jnp.where(variant == 1, WINDOW, 0)
    hi = jnp.where(variant == 2, 2 * WINDOW, 3 * WINDOW)
    live = (jnp.abs(rel) <= WINDOW) & (ki >= lo) & (ki < hi)
    out_ref[0, 0] = jnp.where(live, bias, NEG_INF)


def _swa_bias_tiles(rel_bias):
    return pl.pallas_call(
        _swa_bias_kernel,
        grid=(3, SWA_HEADS),
        in_specs=[pl.BlockSpec(memory_space=pltpu.SMEM)],
        out_specs=pl.BlockSpec((1, 1, 3 * WINDOW, WINDOW), lambda v, h: (v, h, 0, 0)),
        out_shape=jax.ShapeDtypeStruct((3, SWA_HEADS, 3 * WINDOW, WINDOW), F32),
        compiler_params=_cparams(("arbitrary", "arbitrary")),
        name="swa_bias_tiles",
    )(rel_bias)


_T_CQ, _T_CKV, _T_DQ, _T_DV, _T_SQ, _T_SV, _T_GATE = 0, 192, 320, 576, 832, 1216, 1344
_T_ROWS = 2368
_R_CKV, _R_KR, _R_KRROT, _R_DK, _R_SK = 0, 128, 256, 384, 896
_R_COLS = 1024


def _rms_t(v, g_col):
    ms = jnp.mean(v * v, axis=0, keepdims=True)
    return v * lax.rsqrt(ms + RMS_EPS) * g_col


def _in_proj_kernel(x_ref, wt_ref, wr_ref, wuq_ref, wkp_ref, wvt_ref, gq_ref, gkv_col_ref, gkv_row_ref,
                    cos_t_ref, sin_t_ref, cos_p_ref, sin_p_ref, ones_ref,
                    qm_ref, km_ref, vm_ref, qd_ref, kd_ref, vd_ref, sq_ref, sk_ref, sv_ref, g_ref):
    xb = x_ref[0].astype(BF16)

    def proj_t(lo, hi):
        return _nt(wt_ref[lo:hi, :], xb)

    def gate_rows(lo, hi):
        gate = proj_t(_T_GATE + lo, _T_GATE + hi)
        g_ref[0, lo:hi, :] = (gate * jax.nn.sigmoid(gate)).astype(BF16)

    cq_t = proj_t(_T_CQ, _T_CKV)
    ckv_t = proj_t(_T_CKV, _T_DQ)
    r = _nn(xb, wr_ref[...])
    gate_rows(0, D_MIX // 2)

    cqn = _rms_t(cq_t, gq_ref[...]).astype(BF16)
    q_all = _nn(wuq_ref[...], cqn)
    ckvn_t = _rms_t(ckv_t, gkv_col_ref[...]).astype(BF16)
    v_all = _nn(wvt_ref[...], ckvn_t).astype(BF16)
    ckv = r[:, _R_CKV:_R_KR]
    ms = jnp.mean(ckv * ckv, axis=1, keepdims=True)
    ckvn = (ckv * lax.rsqrt(ms + RMS_EPS) * gkv_row_ref[...]).astype(BF16)
    k_nope = _nn(ckvn, wkp_ref[...])
    gate_rows(D_MIX // 2, D_MIX)

    cos_t, sin_t = cos_t_ref[...], sin_t_ref[...]
    q_scale = LOG2E / math.sqrt(MLA_NOPE + MLA_ROPE)
    half = MLA_ROPE // 2
    for h in range(MLA_HEADS):
        blk = q_all[h * LANES:(h + 1) * LANES]
        x1 = blk[MLA_NOPE:MLA_NOPE + half]
        x2 = blk[MLA_NOPE + half:MLA_NOPE + MLA_ROPE]
        roped = jnp.concatenate([blk[:MLA_NOPE], x1 * cos_t - x2 * sin_t, x2 * cos_t + x1 * sin_t,
                                 blk[MLA_NOPE + MLA_ROPE:]], axis=0)
        qm_ref[0, h] = (roped * q_scale).astype(BF16)
    for h in range(MLA_HEADS):
        vm_ref[0, h] = v_all[h * MLA_V:(h + 1) * MLA_V]
    qd_ref[0] = (proj_t(_T_DQ, _T_DV) * (LOG2E / math.sqrt(DIFF_QK))).astype(BF16)

    k_rope = r[:, _R_KR:_R_KRROT] * cos_p_ref[...] + r[:, _R_KRROT:_R_DK] * sin_p_ref[...]
    for h in range(MLA_HEADS):
        km_ref[0, h] = (k_nope[:, h * LANES:(h + 1) * LANES] + k_rope).astype(BF16)
    ones_row = ones_ref[...]
    for h in range(DIFF_HEADS):
        kd_ref[0, h] = (r[:, _R_DK + h * LANES:_R_DK + (h + 1) * LANES] + ones_row).astype(BF16)
    sk_ref[0] = r[:, _R_SK:_R_COLS].astype(BF16)

    vd_ref[0] = proj_t(_T_DV, _T_SQ).astype(BF16)
    sq_ref[0] = (proj_t(_T_SQ, _T_SV) * (LOG2E / math.sqrt(SWA_DIM))).astype(BF16)
    sv_ref[0] = proj_t(_T_SV, _T_GATE).astype(BF16)


def _layer_spec(a, layer):
    return pl.BlockSpec((None,) + a.shape[1:], lambda *_: (layer,) + (0,) * (a.ndim - 1))


def _in_proj(x, lw, layer, tabs, tm):
    b, s, d = x.shape
    cos_t, sin_t, cos_p, sin_p = tabs
    full = lambda a: pl.BlockSpec(a.shape, lambda bi, i: (0,) * a.ndim)
    ones_np = np.zeros((1, LANES), np.float32)
    ones_np[0, 2 * DIFF_QK:2 * DIFF_QK + 2] = 1.0
    ones_row = jnp.asarray(ones_np)
    half = MLA_ROPE // 2
    ins = [x, lw["w_t"], lw["w_r"], lw["w_uq_t"], lw["w_kp"], lw["w_v_t"], lw["gq_col"], lw["gkv_col"],
           lw["gkv_row"], cos_t, sin_t, cos_p, sin_p, ones_row]
    in_specs = [pl.BlockSpec((1, tm, d), lambda bi, i: (bi, i, 0))]
    in_specs += [_layer_spec(a, layer) for a in ins[1:9]]
    in_specs += [pl.BlockSpec((half, tm), lambda bi, i: (0, i)), pl.BlockSpec((half, tm), lambda bi, i: (0, i)),
                 pl.BlockSpec((tm, LANES), lambda bi, i: (i, 0)), pl.BlockSpec((tm, LANES), lambda bi, i: (i, 0)),
                 full(ones_row)]
    sds = jax.ShapeDtypeStruct
    out_shape = [
        sds((b, MLA_HEADS, LANES, s), BF16),
        sds((b, MLA_HEADS, s, LANES), BF16),
        sds((b, MLA_HEADS, MLA_V, s), BF16),
        sds((b, DIFF_HEADS * 2 * DIFF_QK, s), BF16),
        sds((b, DIFF_HEADS, s, LANES), BF16),
        sds((b, W_DIFF, s), BF16),
        sds((b, W_SWA, s), BF16),
        sds((b, s, LANES), BF16),
        sds((b, SWA_KV_HEADS * SWA_DIM, s), BF16),
        sds((b, D_MIX, s), BF16),
    ]
    out_specs = [
        pl.BlockSpec((1, MLA_HEADS, LANES, tm), lambda bi, i: (bi, 0, 0, i)),
        pl.BlockSpec((1, MLA_HEADS, tm, LANES), lambda bi, i: (bi, 0, i, 0)),
        pl.BlockSpec((1, MLA_HEADS, MLA_V, tm), lambda bi, i: (bi, 0, 0, i)),
        pl.BlockSpec((1, DIFF_HEADS * 2 * DIFF_QK, tm), lambda bi, i: (bi, 0, i)),
        pl.BlockSpec((1, DIFF_HEADS, tm, LANES), lambda bi, i: (bi, 0, i, 0)),
        pl.BlockSpec((1, W_DIFF, tm), lambda bi, i: (bi, 0, i)),
        pl.BlockSpec((1, W_SWA, tm), lambda bi, i: (bi, 0, i)),
        pl.BlockSpec((1, tm, LANES), lambda bi, i: (bi, i, 0)),
        pl.BlockSpec((1, SWA_KV_HEADS * SWA_DIM, tm), lambda bi, i: (bi, 0, i)),
        pl.BlockSpec((1, D_MIX, tm), lambda bi, i: (bi, 0, i)),
    ]
    return pl.pallas_call(
        _in_proj_kernel,
        grid=(b, s // tm),
        in_specs=in_specs,
        out_specs=out_specs,
        out_shape=out_shape,
        compiler_params=_cparams(("parallel", "parallel")),
        name="in_proj",
    )(*ins)


def _with_ones(v_t):
    tk = v_t.shape[1]
    row = lax.broadcasted_iota(jnp.int32, (ONES_ROWS, tk), 0)
    return jnp.concatenate([v_t, jnp.where(row == 0, 1.0, 0.0).astype(v_t.dtype)], axis=0)


def _online_step(s_t, tile_max, v_ext, m, acc):
    m_new = jnp.maximum(m, tile_max)
    p = jnp.exp2(s_t - m_new).astype(BF16)
    alpha = jnp.exp2(m - m_new)
    return m_new, alpha * acc + _nn(v_ext, p)


def _tile_off(j, t):
    return j * t if isinstance(j, int) else pl.multiple_of(j * t, t)


PIPE_TILES = 2
COL_BLOCK = 256
OUT_CHUNK = 256
MAX_UNROLLED_TRIPS = 16


def _pipelined_tiles(n_tiles, n_sub, produce_ctx, produce, consume_ctx, consume, carry, peel_first=False):
    u_n = PIPE_TILES
    trip = 2 * u_n
    assert n_tiles % trip == 0 and n_tiles >= trip

    def fill(slot, v0):
        for u in range(u_n):
            ctx = produce_ctx(v0 + u)
            for c in range(n_sub):
                produce(slot, u, c, ctx)

    def half(c_slot, v_c, p_slot, v_p, carry):
        carry = list(carry)
        for u in range(u_n):
            pctx = None if p_slot is None else produce_ctx(v_p + u)
            cctx = consume_ctx(v_c + u)
            for c in range(n_sub):
                carry[c] = consume(c_slot, u, c, cctx, carry[c])
                if p_slot is not None:
                    produce(p_slot, u, c, pctx)
        return tuple(carry)

    def body(v0, carry, last):
        carry = half(0, v0, 1, v0 + u_n, carry)
        return half(1, v0 + u_n, None if last else 0, v0 + trip, carry)

    fill(0, 0)
    n_trips = n_tiles // trip
    if n_trips <= MAX_UNROLLED_TRIPS:
        for i in range(n_trips):
            carry = body(i * trip, carry, i == n_trips - 1)
        return carry
    first = 0
    if peel_first:
        carry = body(0, carry, False)
        first = 1
    carry = lax.fori_loop(first, n_trips - 1, lambda i, c: body(i * trip, c, False), carry)
    return body((n_trips - 1) * trip, carry, True)


def _score_scratch(n_sub, tk):
    n = 2 * PIPE_TILES * n_sub
    return [pltpu.VMEM((1, tk, COL_BLOCK), F32)] * n + [pltpu.VMEM((1, 1, COL_BLOCK), F32)] * n


def _score_refs(bufs, n_sub):
    n = 2 * PIPE_TILES * n_sub
    z = jnp.minimum(pl.program_id(0), 0)
    at = lambda slot, u, c: (slot * PIPE_TILES + u) * n_sub + c
    return (lambda slot, u, c: bufs[at(slot, u, c)].at[z]), (lambda slot, u, c: bufs[n + at(slot, u, c)].at[z])


def _mla_attn_kernel(q_ref, k_ref, v_ref, o_ref, *bufs, tk, nk):
    tq = q_ref.shape[3]
    n_sub = tq // COL_BLOCK
    s_buf, mx_buf = _score_refs(bufs, n_sub)
    q_blk = [q_ref[0, 0, :, c * COL_BLOCK:(c + 1) * COL_BLOCK] for c in range(n_sub)]

    def produce_ctx(j):
        return k_ref[0, 0, pl.ds(_tile_off(j, tk), tk), :]

    def produce(slot, u, c, k):
        s_t = _nn(k, q_blk[c])
        s_buf(slot, u, c)[...] = s_t
        mx_buf(slot, u, c)[...] = jnp.max(s_t, axis=0, keepdims=True)

    def consume_ctx(j):
        return _with_ones(v_ref[0, 0, :, pl.ds(_tile_off(j, tk), tk)])

    def consume(slot, u, c, v_ext, carry):
        return _online_step(s_buf(slot, u, c)[...], mx_buf(slot, u, c)[...], v_ext, *carry)

    m0 = jnp.full((1, COL_BLOCK), NEG_INF, F32)
    acc0 = jnp.zeros((MLA_V + ONES_ROWS, COL_BLOCK), F32)
    carry = _pipelined_tiles(nk, n_sub, produce_ctx, produce, consume_ctx, consume, ((m0, acc0),) * n_sub)
    for c, (_, acc) in enumerate(carry):
        o_ref[0, :, c * COL_BLOCK:(c + 1) * COL_BLOCK] = (acc[:MLA_V] / acc[MLA_V:MLA_V + 1]).astype(o_ref.dtype)


def _mla_attention(qm, km, vm, tq, tk):
    b, h, _, s = qm.shape
    return pl.pallas_call(
        functools.partial(_mla_attn_kernel, tk=tk, nk=s // tk),
        grid=(b, h, s // tq),
        in_specs=[pl.BlockSpec((1, 1, LANES, tq), lambda bi, hi, qi: (bi, hi, 0, qi)),
                  pl.BlockSpec((1, 1, s, LANES), lambda bi, hi, qi: (bi, hi, 0, 0)),
                  pl.BlockSpec((1, 1, MLA_V, s), lambda bi, hi, qi: (bi, hi, 0, 0))],
        out_specs=pl.BlockSpec((1, MLA_V, tq), lambda bi, hi, qi: (bi, hi, qi)),
        out_shape=jax.ShapeDtypeStruct((b, W_MLA, s), BF16),
        scratch_shapes=_score_scratch(tq // COL_BLOCK, tk),
        compiler_params=_cparams(("parallel", "parallel", "arbitrary")),
        name="mla_attention",
    )(qm, km, vm)


NEAR_TILES = 4


def _diff_near_offsets(r):
    return min(-1, r - NEAR_TILES), NEAR_TILES - 1


def _diff_attn_kernel(far_ref, lam_ref, q_ref, k_ref, v_ref, bias_ref, subln_ref, o_ref,
                      qfar_buf, *bufs, tq, tk, nk, layer, lam_init):
    s_buf, mx_buf = _score_refs(bufs, 2 * (tq // COL_BLOCK))
    h = pl.program_id(1)
    qi = pl.program_id(2)
    t = tq
    r = tq // tk
    q_t = q_ref[0]
    r16 = lax.broadcasted_iota(jnp.int32, (ONES_ROWS, t), 0)
    zero_map = jnp.zeros((DIFF_QK, t), BF16)
    zero_pad = jnp.zeros((LANES - 2 * DIFF_QK - ONES_ROWS, t), BF16)

    def q_aug(mp, side):
        own = q_t[mp * DIFF_QK:(mp + 1) * DIFF_QK]
        own = [own, zero_map] if mp == 0 else [zero_map, own]
        if side is None:
            aug = jnp.zeros((ONES_ROWS, t), BF16)
        else:
            c = jnp.full((ONES_ROWS, t), far_ref[side, h], F32)
            c_hi = c.astype(BF16).astype(F32)
            aug = jnp.where(r16 == 0, c_hi, jnp.where(r16 == 1, c - c_hi, 0.0)).astype(BF16)
        return jnp.concatenate(own + [aug, zero_pad], axis=0)

    near_lo = jnp.clip(r * qi - 1, 0, nk - NEAR_TILES)
    e_min, _ = _diff_near_offsets(r)
    q_near = [q_aug(0, None), q_aug(1, None)]
    for side in range(2):
        for mp in range(2):
            qfar_buf[side, mp] = q_aug(mp, side)

    def key_tile(v):
        if isinstance(v, int) and v < NEAR_TILES:
            return near_lo + v, True, None
        idx = v - NEAR_TILES
        side = (idx >= near_lo).astype(jnp.int32)
        return idx + NEAR_TILES * side, False, side

    n_cb = tq // COL_BLOCK
    cols = lambda cb: slice(cb * COL_BLOCK, (cb + 1) * COL_BLOCK)

    def produce_ctx(v):
        j, near, side = key_tile(v)
        return k_ref[0, 0, pl.ds(_tile_off(j, tk), tk), :], near, side, j

    def produce(slot, u, c, ctx):
        k, near, side, j = ctx
        mp, cb = divmod(c, n_cb)
        if near:
            s_t = _nn(k, q_near[mp][:, cols(cb)]) + bias_ref[0, j - r * qi - e_min, :, cols(cb)]
        else:
            s_t = _nn(k, qfar_buf[side, mp, :, cols(cb)])
        s_buf(slot, u, c)[...] = s_t
        mx_buf(slot, u, c)[...] = jnp.max(s_t, axis=0, keepdims=True)

    def consume_ctx(v):
        j, _, _ = key_tile(v)
        return _with_ones(v_ref[0, :, pl.ds(_tile_off(j, tk), tk)])

    def consume(slot, u, c, v_ext, carry):
        return _online_step(s_buf(slot, u, c)[...], mx_buf(slot, u, c)[...], v_ext, *carry)

    m0 = jnp.full((1, COL_BLOCK), NEG_INF, F32)
    acc0 = jnp.zeros((DIFF_V + ONES_ROWS, COL_BLOCK), F32)
    carry = _pipelined_tiles(nk, 2 * n_cb, produce_ctx, produce, consume_ctx, consume,
                             ((m0, acc0),) * (2 * n_cb), peel_first=True)

    lam = lam_ref[layer, 0]
    for cb in range(n_cb):
        a0, a1 = carry[cb][1], carry[n_cb + cb][1]
        o = a0[:DIFF_V] / a0[DIFF_V:DIFF_V + 1] - lam * (a1[:DIFF_V] / a1[DIFF_V:DIFF_V + 1])
        ms = jnp.mean(o * o, axis=0, keepdims=True)
        o = o * lax.rsqrt(ms + RMS_EPS) * subln_ref[...] * (1.0 - lam_init)
        o_ref[0, :, cols(cb)] = o.astype(o_ref.dtype)


def _diff_attention(qd, kd, vd, bias_tiles, far, lam, subln_col, layer, tq, tk, lam_init):
    b, _, s = qd.shape
    n_e = bias_tiles.shape[1]
    assert tq % tk == 0 and tq // tk + 2 <= NEAR_TILES and tk >= REL_MAX_DIST
    smem = pl.BlockSpec(memory_space=pltpu.SMEM)
    return pl.pallas_call(
        functools.partial(_diff_attn_kernel, tq=tq, tk=tk, nk=s // tk, layer=layer, lam_init=lam_init),
        grid=(b, DIFF_HEADS, s // tq),
        in_specs=[smem, smem,
                  pl.BlockSpec((1, 2 * DIFF_QK, tq), lambda bi, hi, qi: (bi, hi, qi)),
                  pl.BlockSpec((1, 1, s, LANES), lambda bi, hi, qi: (bi, hi, 0, 0)),
                  pl.BlockSpec((1, DIFF_V, s), lambda bi, hi, qi: (bi, hi, 0)),
                  pl.BlockSpec((1, n_e, tk, tq), lambda bi, hi, qi: (hi, 0, 0, 0)),
                  _layer_spec(subln_col, layer)],
        out_specs=pl.BlockSpec((1, DIFF_V, tq), lambda bi, hi, qi: (bi, hi, qi)),
        out_shape=jax.ShapeDtypeStruct((b, W_DIFF, s), BF16),
        scratch_shapes=[pltpu.VMEM((2, 2, LANES, tq), BF16)] + _score_scratch(2 * (tq // COL_BLOCK), tk),
        compiler_params=_cparams(("parallel", "parallel", "arbitrary")),
        name="diff_attention",
    )(far, lam, qd, kd, vd, bias_tiles, subln_col)


def _swa_attn_kernel(sink_ref, q_ref, k0_ref, k1_ref, k2_ref, k3_ref, v0_ref, v1_ref, v2_ref, v3_ref,
                     bias_ref, o_ref, *, nq, layer):
    qi = pl.program_id(1)
    kb = [k0_ref[0], k1_ref[0], k2_ref[0], k3_ref[0]]
    vb = [v0_ref[0], v1_ref[0], v2_ref[0], v3_ref[0]]
    n_sb = len(kb) - 2
    k_win = [jnp.concatenate(kb[sb:sb + 3], axis=0) for sb in range(n_sb)]
    v_win = [jnp.concatenate(vb[sb:sb + 3], axis=1) for sb in range(n_sb)]
    variant = [jnp.where(qi == 0, 1, 0) if sb == 0 else jnp.where(qi == nq - 1, 2, 0) if sb == n_sb - 1 else 0
               for sb in range(n_sb)]
    grp = SWA_HEADS // SWA_KV_HEADS
    zeros = jnp.zeros((SWA_DIM, WINDOW), BF16)
    chains = [(hq, sb) for hq in range(SWA_HEADS) for sb in range(n_sb)]
    cols = lambda sb: slice(sb * WINDOW, (sb + 1) * WINDOW)
    s_t, p_t, den = {}, {}, {}
    for hq, sb in chains:
        q_t = q_ref[0, hq * SWA_DIM:(hq + 1) * SWA_DIM, cols(sb)]
        q_pad = jnp.concatenate([q_t, zeros] if hq // grp == 0 else [zeros, q_t], axis=0)
        s_t[hq, sb] = _nn(k_win[sb], q_pad) + bias_ref[variant[sb], hq]
    for hq, sb in chains:
        sink = sink_ref[layer, hq] * LOG2E
        m = jnp.maximum(jnp.max(s_t[hq, sb], axis=0, keepdims=True), sink)
        p = jnp.exp2(s_t[hq, sb] - m)
        den[hq, sb] = jnp.sum(p, axis=0, keepdims=True) + jnp.exp2(sink - m)
        p_t[hq, sb] = p.astype(BF16)
    for hq, sb in chains:
        g = hq // grp
        o = _nn(v_win[sb][g * SWA_DIM:(g + 1) * SWA_DIM], p_t[hq, sb]) / den[hq, sb]
        o_ref[0, hq * SWA_DIM:(hq + 1) * SWA_DIM, cols(sb)] = o.astype(o_ref.dtype)


def _swa_attention(sq, sk, sv, bias_tiles, sink, layer, tq):
    b, _, s = sq.shape
    assert tq % WINDOW == 0
    r = tq // WINDOW
    nblk = s // WINDOW
    smem = pl.BlockSpec(memory_space=pltpu.SMEM)
    wk = tq + 2 * WINDOW
    nw = wk // WINDOW

    def kspec(c):
        return pl.BlockSpec((1, WINDOW, LANES),
                            lambda bi, qi: (bi, jnp.clip(qi * r - 1 + c, 0, nblk - 1), 0))

    def vspec(c):
        return pl.BlockSpec((1, SWA_KV_HEADS * SWA_DIM, WINDOW),
                            lambda bi, qi: (bi, 0, jnp.clip(qi * r - 1 + c, 0, nblk - 1)))

    assert nw == 4
    return pl.pallas_call(
        functools.partial(_swa_attn_kernel, nq=s // tq, layer=layer),
        grid=(b, s // tq),
        in_specs=[smem, pl.BlockSpec((1, W_SWA, tq), lambda bi, qi: (bi, 0, qi))]
        + [kspec(c) for c in range(nw)] + [vspec(c) for c in range(nw)]
        + [pl.BlockSpec(bias_tiles.shape, lambda bi, qi: (0, 0, 0, 0))],
        out_specs=pl.BlockSpec((1, W_SWA, tq), lambda bi, qi: (bi, 0, qi)),
        out_shape=jax.ShapeDtypeStruct((b, W_SWA, s), BF16),
        compiler_params=_cparams(("parallel", "parallel")),
        name="swa_attention",
    )(sink, sq, sk, sk, sk, sk, sv, sv, sv, sv, bias_tiles)


def _out_proj_kernel(x_ref, om_ref, od_ref, os_ref, g_ref, w_ref, lng_ref, lnb_ref, o_ref, *, alpha):
    tm = x_ref.shape[1]
    for lo in range(0, tm, OUT_CHUNK):
        tok = slice(lo, lo + OUT_CHUNK)
        o_t = jnp.concatenate([om_ref[0, :, tok], od_ref[0, :, tok], os_ref[0, :, tok]], axis=0)
        og_t = o_t * g_ref[0, :, tok]
        y = lax.dot_general(og_t, w_ref[...], (((0,), (0,)), ((), ())), preferred_element_type=F32)
        z = alpha * x_ref[0, tok, :] + y
        mu = jnp.mean(z, axis=1, keepdims=True)
        zc = z - mu
        var = jnp.mean(zc * zc, axis=1, keepdims=True)
        o_ref[0, tok, :] = zc * lax.rsqrt(var + LN_EPS) * lng_ref[...] + lnb_ref[...]


def _out_proj(x, om, od, osw, g, w_out, ln_g, ln_b, layer, tm, alpha):
    b, s, d = x.shape
    tspec = lambda rows: pl.BlockSpec((1, rows, tm), lambda bi, i: (bi, 0, i))
    return pl.pallas_call(
        functools.partial(_out_proj_kernel, alpha=alpha),
        grid=(b, s // tm),
        in_specs=[pl.BlockSpec((1, tm, d), lambda bi, i: (bi, i, 0)),
                  tspec(W_MLA), tspec(W_DIFF), tspec(W_SWA), tspec(D_MIX),
                  _layer_spec(w_out, layer), _layer_spec(ln_g, layer), _layer_spec(ln_b, layer)],
        out_specs=pl.BlockSpec((1, tm, d), lambda bi, i: (bi, i, 0)),
        out_shape=jax.ShapeDtypeStruct((b, s, d), F32),
        compiler_params=_cparams(("parallel", "parallel")),
        name="out_proj",
    )(x, om, od, osw, g, w_out, ln_g, ln_b)


def _layer_weights(w_in, q_norm, kv_norm, w_uq, w_ukv):
    cq, ckv, kr, dq, dk, dv, sq, sk, sv, gate = jnp.split(w_in, np.cumsum(IN_SIZES)[:-1].tolist(), axis=1)
    d = w_in.shape[0]
    half = MLA_ROPE // 2
    w_t = jnp.concatenate([cq, ckv, dq, dv, sq, sv, gate], axis=1).T.astype(BF16)
    pad = lambda a, lo, width: jnp.pad(a, ((0, 0), (lo, width - lo - a.shape[1])))
    kr_p = pad(kr, MLA_NOPE, LANES)
    kr_rot_p = pad(jnp.concatenate([-kr[:, half:], kr[:, :half]], axis=1), MLA_NOPE, LANES)
    dk_p = jnp.concatenate([pad(dk[:, h * 2 * DIFF_QK:(h + 1) * 2 * DIFF_QK], 0, LANES)
                            for h in range(DIFF_HEADS)], axis=1)
    w_r = jnp.concatenate([ckv, kr_p, kr_rot_p, dk_p, sk], axis=1).astype(BF16)
    assert w_t.shape == (_T_ROWS, d) and w_r.shape == (d, _R_COLS)
    qk = MLA_NOPE + MLA_ROPE
    w_uq_t = jnp.concatenate([jnp.pad(w_uq[:, h * qk:(h + 1) * qk].T, ((0, LANES - qk), (0, 0)))
                              for h in range(MLA_HEADS)], axis=0).astype(BF16)
    kv = MLA_NOPE + MLA_V
    w_kp = jnp.concatenate([pad(w_ukv[:, h * kv:h * kv + MLA_NOPE], 0, LANES)
                            for h in range(MLA_HEADS)], axis=1).astype(BF16)
    w_v_t = jnp.concatenate([w_ukv[:, h * kv + MLA_NOPE:(h + 1) * kv].T
                             for h in range(MLA_HEADS)], axis=0).astype(BF16)
    return dict(w_t=w_t, w_r=w_r, w_uq_t=w_uq_t, w_kp=w_kp, w_v_t=w_v_t,
                gq_col=q_norm.reshape(-1, 1), gkv_col=kv_norm.reshape(-1, 1), gkv_row=kv_norm.reshape(1, -1))


def _lambda_kernel(q1_ref, k1_ref, q2_ref, k2_ref, init_ref, o_ref):
    a = jnp.sum(q1_ref[...] * k1_ref[...], axis=1, keepdims=True)
    c = jnp.sum(q2_ref[...] * k2_ref[...], axis=1, keepdims=True)
    o_ref[...] = jnp.exp(a) - jnp.exp(c) + init_ref[...]


def _diff_lambda(lam_vecs, lam_inits):
    depth = lam_vecs.shape[0]
    return pl.pallas_call(
        _lambda_kernel,
        out_shape=jax.ShapeDtypeStruct((depth, 1), F32),
        name="diff_lambda",
    )(*(lam_vecs[:, i, :] for i in range(4)), jnp.asarray(np.asarray(lam_inits, np.float32).reshape(depth, 1)))


def _tiles(seq):
    return dict(tm=min(512, seq), mla_tq=min(1024, seq), mla_tk=256, diff_tq=min(512, seq), diff_tk=256, swa_tq=256)


def kernel(x, w_in, mla_q_norm, mla_kv_norm, mla_w_uq, mla_w_ukv, diff_lambda, diff_subln, swa_sink,
           rel_bias, w_out, ln_g, ln_b):
    depth = w_in.shape[0]
    seq = x.shape[1]
    cfg = _tiles(seq)
    alpha = (2 * depth) ** 0.25
    tabs = _rope_tables(seq)
    e_min, e_max = _diff_near_offsets(cfg["diff_tq"] // cfg["diff_tk"])
    diff_bias = _diff_bias_tiles(rel_bias, cfg["diff_tq"], cfg["diff_tk"], e_min, e_max)
    swa_bias = _swa_bias_tiles(rel_bias)
    half = REL_BUCKETS // 2
    far = jnp.stack([rel_bias[half - 1, :DIFF_HEADS], rel_bias[2 * half - 1, :DIFF_HEADS]]) * LOG2E
    lam_inits = [0.8 - 0.6 * math.exp(-0.3 * l) for l in range(depth)]
    lam = _diff_lambda(diff_lambda, lam_inits)
    lw = jax.vmap(_layer_weights)(w_in, mla_q_norm, mla_kv_norm, mla_w_uq, mla_w_ukv)
    subln_col = diff_subln[:, :, None]
    w_out_b, ln_g3, ln_b3 = w_out.astype(BF16), ln_g[:, None, :], ln_b[:, None, :]
    for l in range(depth):
        qm, km, vm, qd, kd, vd, sq, sk, sv, g = _in_proj(x, lw, l, tabs, cfg["tm"])
        om = _mla_attention(qm, km, vm, cfg["mla_tq"], cfg["mla_tk"])
        od = _diff_attention(qd, kd, vd, diff_bias, far, lam, subln_col, l, cfg["diff_tq"], cfg["diff_tk"],
                             lam_inits[l])
        osw = _swa_attention(sq, sk, sv, swa_bias, swa_sink, l, cfg["swa_tq"])
        x = _out_proj(x, om, od, osw, g, w_out_b, ln_g3, ln_b3, l, cfg["tm"], alpha)
    return x
```

```python
import functools
import math

import numpy as np
import jax
import jax.numpy as jnp
from jax import lax
from jax.experimental import pallas as pl
from jax.experimental.pallas import tpu as pltpu

D_MODEL = 1024
MLA_HEADS, MLA_Q_RANK, MLA_KV_RANK, MLA_NOPE, MLA_ROPE, MLA_V = 6, 192, 128, 64, 32, 64
DIFF_HEADS, DIFF_QK, DIFF_V = 4, 32, 64
SWA_HEADS, SWA_KV_HEADS, SWA_DIM, WINDOW = 6, 2, 64, 128
W_MLA, W_DIFF, W_SWA = MLA_HEADS * MLA_V, DIFF_HEADS * DIFF_V, SWA_HEADS * SWA_DIM
D_MIX = W_MLA + W_DIFF + W_SWA
IN_SIZES = (MLA_Q_RANK, MLA_KV_RANK, MLA_ROPE, DIFF_HEADS * 2 * DIFF_QK, DIFF_HEADS * 2 * DIFF_QK,
            W_DIFF, W_SWA, SWA_KV_HEADS * SWA_DIM, SWA_KV_HEADS * SWA_DIM, D_MIX)
REL_BUCKETS, REL_MAX_DIST = 32, 128
ROPE_THETA = 10000.0
RMS_EPS, LN_EPS = 1e-6, 1e-5

LOG2E = math.log2(math.e)
LANES = 128
ONES_ROWS = 16
VMEM_LIMIT = 56 * 1024 * 1024

F32, BF16 = jnp.float32, jnp.bfloat16
NEG_INF = float("-inf")


def _cparams(sem):
    return pltpu.CompilerParams(dimension_semantics=sem, vmem_limit_bytes=VMEM_LIMIT)


def _nt(a, b):
    return lax.dot_general(a, b, (((1,), (1,)), ((), ())), preferred_element_type=F32)


def _nn(a, b):
    return jnp.dot(a, b, preferred_element_type=F32)


def _rope_tables_kernel(fcol_ref, frow_ref, cos_t_ref, sin_t_ref, cos_p_ref, sin_p_ref, *, tile):
    p0 = pl.program_id(0) * tile
    pos_l = (p0 + lax.broadcasted_iota(jnp.int32, (MLA_ROPE // 2, tile), 1)).astype(F32)
    ang_t = pos_l * fcol_ref[...]
    cos_t_ref[...] = jnp.cos(ang_t)
    sin_t_ref[...] = jnp.sin(ang_t)
    pos_s = (p0 + lax.broadcasted_iota(jnp.int32, (tile, LANES), 0)).astype(F32)
    ang_p = pos_s * frow_ref[...]
    lane = lax.broadcasted_iota(jnp.int32, (tile, LANES), 1)
    live = (lane >= MLA_NOPE) & (lane < MLA_NOPE + MLA_ROPE)
    cos_p_ref[...] = jnp.where(live, jnp.cos(ang_p), 0.0)
    sin_p_ref[...] = jnp.where(live, jnp.sin(ang_p), 0.0)


def _rope_tables(seq):
    half = MLA_ROPE // 2
    freqs = (ROPE_THETA ** (-np.arange(half, dtype=np.float64) / half)).astype(np.float32)
    fcol = jnp.asarray(freqs.reshape(half, 1))
    frow_np = np.zeros((1, LANES), np.float32)
    frow_np[0, MLA_NOPE:MLA_NOPE + half] = freqs
    frow_np[0, MLA_NOPE + half:MLA_NOPE + MLA_ROPE] = freqs
    frow = jnp.asarray(frow_np)
    tile = min(seq, 1024)
    return pl.pallas_call(
        functools.partial(_rope_tables_kernel, tile=tile),
        grid=(seq // tile,),
        in_specs=[pl.BlockSpec((half, 1), lambda i: (0, 0)), pl.BlockSpec((1, LANES), lambda i: (0, 0))],
        out_specs=[pl.BlockSpec((half, tile), lambda i: (0, i)), pl.BlockSpec((half, tile), lambda i: (0, i)),
                   pl.BlockSpec((tile, LANES), lambda i: (i, 0)), pl.BlockSpec((tile, LANES), lambda i: (i, 0))],
        out_shape=[jax.ShapeDtypeStruct((half, seq), F32), jax.ShapeDtypeStruct((half, seq), F32),
                   jax.ShapeDtypeStruct((seq, LANES), F32), jax.ShapeDtypeStruct((seq, LANES), F32)],
        compiler_params=_cparams(("arbitrary",)),
        name="rope_tables",
    )(fcol, frow)


def _t5_bucket(rel):
    half = REL_BUCKETS // 2
    max_exact = half // 2
    n = jnp.abs(rel)
    large = jnp.full(rel.shape, max_exact, jnp.int32)
    for k in range(1, half - max_exact):
        thr = max_exact * (REL_MAX_DIST / max_exact) ** (k / (half - max_exact))
        thr_i = int(round(thr)) if abs(thr - round(thr)) < 1e-9 else int(math.ceil(thr))
        large = large + (n >= thr_i).astype(jnp.int32)
    return jnp.where(rel > 0, half, 0) + jnp.where(n < max_exact, n, large)


def _bias_lookup(tab_ref, bucket, head):
    out = jnp.zeros(bucket.shape, F32)
    for b in range(REL_BUCKETS):
        out = jnp.where(bucket == b, tab_ref[b, head], out)
    return out


def _diff_bias_kernel(tab_ref, out_ref, *, tq, tk, e_min):
    h = pl.program_id(0)
    d = (pl.program_id(1) + e_min) * tk
    ki = lax.broadcasted_iota(jnp.int32, (tk, tq), 0)
    qj = lax.broadcasted_iota(jnp.int32, (tk, tq), 1)
    out_ref[0, 0] = _bias_lookup(tab_ref, _t5_bucket(d + ki - qj), h) * LOG2E


def _diff_bias_tiles(rel_bias, tq, tk, e_min, e_max):
    n_e = e_max - e_min + 1
    return pl.pallas_call(
        functools.partial(_diff_bias_kernel, tq=tq, tk=tk, e_min=e_min),
        grid=(DIFF_HEADS, n_e),
        in_specs=[pl.BlockSpec(memory_space=pltpu.SMEM)],
        out_specs=pl.BlockSpec((1, 1, tk, tq), lambda h, d: (h, d, 0, 0)),
        out_shape=jax.ShapeDtypeStruct((DIFF_HEADS, n_e, tk, tq), F32),
        compiler_params=_cparams(("arbitrary", "arbitrary")),
        name="diff_bias_tiles",
    )(rel_bias)


def _swa_bias_kernel(tab_ref, out_ref):
    variant = pl.program_id(0)
    h = pl.program_id(1)
    ki = lax.broadcasted_iota(jnp.int32, (3 * WINDOW, WINDOW), 0)
    qj = lax.broadcasted_iota(jnp.int32, (3 * WINDOW, WINDOW), 1)
    rel = ki - WINDOW - qj
    bias = _bias_lookup(tab_ref, _t5_bucket(rel), DIFF_HEADS + h) * LOG2E
    lo = jnp.where(variant == 1, WINDOW, 0)
    hi = jnp.where(variant == 2, 2 * WINDOW, 3 * WINDOW)
    live = (jnp.abs(rel) <= WINDOW) & (ki >= lo) & (ki < hi)
    out_ref[0, 0] = jnp.where(live, bias, NEG_INF)


def _swa_bias_tiles(rel_bias):
    return pl.pallas_call(
        _swa_bias_kernel,
        grid=(3, SWA_HEADS),
        in_specs=[pl.BlockSpec(memory_space=pltpu.SMEM)],
        out_specs=pl.BlockSpec((1, 1, 3 * WINDOW, WINDOW), lambda v, h: (v, h, 0, 0)),
        out_shape=jax.ShapeDtypeStruct((3, SWA_HEADS, 3 * WINDOW, WINDOW), F32),
        compiler_params=_cparams(("arbitrary", "arbitrary")),
        name="swa_bias_tiles",
    )(rel_bias)


_T_CQ, _T_CKV, _T_DQ, _T_DV, _T_SQ, _T_SV, _T_GATE = 0, 192, 320, 576, 832, 1216, 1344
_T_ROWS = 2368
_R_CKV, _R_KR, _R_KRROT, _R_DK, _R_SK = 0, 128, 256, 384, 896
_R_COLS = 1024


def _rms_t(v, g_col):
    ms = jnp.mean(v * v, axis=0, keepdims=True)
    return v * lax.rsqrt(ms + RMS_EPS) * g_col


N_IN_OPERANDS, N_IN_OUTPUTS = 13, 10


def _in_proj_body(xb, wt_ref, wr_ref, wuq_ref, wkp_ref, wvt_ref, gq_ref, gkv_col_ref, gkv_row_ref,
                  cos_t_ref, sin_t_ref, cos_p_ref, sin_p_ref, ones_ref,
                  qm_ref, km_ref, vm_ref, qd_ref, kd_ref, vd_ref, sq_ref, sk_ref, sv_ref, g_ref):
    def proj_t(lo, hi):
        return _nt(wt_ref[lo:hi, :], xb)

    def gate_rows(lo, hi):
        gate = proj_t(_T_GATE + lo, _T_GATE + hi)
        g_ref[0, lo:hi, :] = (gate * jax.nn.sigmoid(gate)).astype(BF16)

    cq_t = proj_t(_T_CQ, _T_CKV)
    ckv_t = proj_t(_T_CKV, _T_DQ)
    r = _nn(xb, wr_ref[...])
    gate_rows(0, D_MIX // 2)

    cqn = _rms_t(cq_t, gq_ref[...]).astype(BF16)
    q_all = _nn(wuq_ref[...], cqn)
    ckvn_t = _rms_t(ckv_t, gkv_col_ref[...]).astype(BF16)
    v_all = _nn(wvt_ref[...], ckvn_t).astype(BF16)
    ckv = r[:, _R_CKV:_R_KR]
    ms = jnp.mean(ckv * ckv, axis=1, keepdims=True)
    ckvn = (ckv * lax.rsqrt(ms + RMS_EPS) * gkv_row_ref[...]).astype(BF16)
    k_nope = _nn(ckvn, wkp_ref[...])
    gate_rows(D_MIX // 2, D_MIX)

    cos_t, sin_t = cos_t_ref[...], sin_t_ref[...]
    q_scale = LOG2E / math.sqrt(MLA_NOPE + MLA_ROPE)
    half = MLA_ROPE // 2
    for h in range(MLA_HEADS):
        blk = q_all[h * LANES:(h + 1) * LANES]
        x1 = blk[MLA_NOPE:MLA_NOPE + half]
        x2 = blk[MLA_NOPE + half:MLA_NOPE + MLA_ROPE]
        roped = jnp.concatenate([blk[:MLA_NOPE], x1 * cos_t - x2 * sin_t, x2 * cos_t + x1 * sin_t,
                                 blk[MLA_NOPE + MLA_ROPE:]], axis=0)
        qm_ref[0, h] = (roped * q_scale).astype(BF16)
    for h in range(MLA_HEADS):
        vm_ref[0, h] = v_all[h * MLA_V:(h + 1) * MLA_V]
    qd_ref[0] = (proj_t(_T_DQ, _T_DV) * (LOG2E / math.sqrt(DIFF_QK))).astype(BF16)

    k_rope = r[:, _R_KR:_R_KRROT] * cos_p_ref[...] + r[:, _R_KRROT:_R_DK] * sin_p_ref[...]
    for h in range(MLA_HEADS):
        km_ref[0, h] = (k_nope[:, h * LANES:(h + 1) * LANES] + k_rope).astype(BF16)
    ones_row = ones_ref[...]
    for h in range(DIFF_HEADS):
        kd_ref[0, h] = (r[:, _R_DK + h * LANES:_R_DK + (h + 1) * LANES] + ones_row).astype(BF16)
    sk_ref[0] = r[:, _R_SK:_R_COLS].astype(BF16)

    vd_ref[0] = proj_t(_T_DV, _T_SQ).astype(BF16)
    sq_ref[0] = (proj_t(_T_SQ, _T_SV) * (LOG2E / math.sqrt(SWA_DIM))).astype(BF16)
    sv_ref[0] = proj_t(_T_SV, _T_GATE).astype(BF16)


def _layer_spec(a, layer):
    return pl.BlockSpec((None,) + a.shape[1:], lambda *_: (layer,) + (0,) * (a.ndim - 1))


def _in_proj_operands(lw, layer, tabs, tm):
    cos_t, sin_t, cos_p, sin_p = tabs
    ones_np = np.zeros((1, LANES), np.float32)
    ones_np[0, 2 * DIFF_QK:2 * DIFF_QK + 2] = 1.0
    ones_row = jnp.asarray(ones_np)
    half = MLA_ROPE // 2
    weights = [lw["w_t"], lw["w_r"], lw["w_uq_t"], lw["w_kp"], lw["w_v_t"], lw["gq_col"], lw["gkv_col"],
               lw["gkv_row"]]
    ins = weights + [cos_t, sin_t, cos_p, sin_p, ones_row]
    in_specs = [_layer_spec(a, layer) for a in weights]
    in_specs += [pl.BlockSpec((half, tm), lambda bi, i: (0, i)), pl.BlockSpec((half, tm), lambda bi, i: (0, i)),
                 pl.BlockSpec((tm, LANES), lambda bi, i: (i, 0)), pl.BlockSpec((tm, LANES), lambda bi, i: (i, 0)),
                 pl.BlockSpec(ones_row.shape, lambda bi, i: (0, 0))]
    assert len(ins) == N_IN_OPERANDS
    return ins, in_specs


def _in_proj_outputs(b, s, tm):
    sds = jax.ShapeDtypeStruct
    out_shape = [
        sds((b, MLA_HEADS, LANES, s), BF16),
        sds((b, MLA_HEADS, s, LANES), BF16),
        sds((b, MLA_HEADS, MLA_V, s), BF16),
        sds((b, DIFF_HEADS * 2 * DIFF_QK, s), BF16),
        sds((b, DIFF_HEADS, s, LANES), BF16),
        sds((b, W_DIFF, s), BF16),
        sds((b, W_SWA, s), BF16),
        sds((b, s, LANES), BF16),
        sds((b, SWA_KV_HEADS * SWA_DIM, s), BF16),
        sds((b, D_MIX, s), BF16),
    ]
    out_specs = [
        pl.BlockSpec((1, MLA_HEADS, LANES, tm), lambda bi, i: (bi, 0, 0, i)),
        pl.BlockSpec((1, MLA_HEADS, tm, LANES), lambda bi, i: (bi, 0, i, 0)),
        pl.BlockSpec((1, MLA_HEADS, MLA_V, tm), lambda bi, i: (bi, 0, 0, i)),
        pl.BlockSpec((1, DIFF_HEADS * 2 * DIFF_QK, tm), lambda bi, i: (bi, 0, i)),
        pl.BlockSpec((1, DIFF_HEADS, tm, LANES), lambda bi, i: (bi, 0, i, 0)),
        pl.BlockSpec((1, W_DIFF, tm), lambda bi, i: (bi, 0, i)),
        pl.BlockSpec((1, W_SWA, tm), lambda bi, i: (bi, 0, i)),
        pl.BlockSpec((1, tm, LANES), lambda bi, i: (bi, i, 0)),
        pl.BlockSpec((1, SWA_KV_HEADS * SWA_DIM, tm), lambda bi, i: (bi, 0, i)),
        pl.BlockSpec((1, D_MIX, tm), lambda bi, i: (bi, 0, i)),
    ]
    assert len(out_shape) == N_IN_OUTPUTS
    return out_shape, out_specs


N_OUT_OPERANDS = 7


def _out_proj_chunks(x_ref, om_ref, od_ref, os_ref, g_ref, w_ref, lng_ref, lnb_ref, alpha):
    for lo in range(0, x_ref.shape[1], OUT_CHUNK):
        tok = slice(lo, lo + OUT_CHUNK)
        o_t = jnp.concatenate([om_ref[0, :, tok], od_ref[0, :, tok], os_ref[0, :, tok]], axis=0)
        og_t = o_t * g_ref[0, :, tok]
        y = lax.dot_general(og_t, w_ref[...], (((0,), (0,)), ((), ())), preferred_element_type=F32)
        z = alpha * x_ref[0, tok, :] + y
        mu = jnp.mean(z, axis=1, keepdims=True)
        zc = z - mu
        var = jnp.mean(zc * zc, axis=1, keepdims=True)
        yield tok, zc * lax.rsqrt(var + LN_EPS) * lng_ref[...] + lnb_ref[...]


def _mix_kernel(x_ref, *refs, has_out, has_in, alpha):
    n_in = (N_OUT_OPERANDS if has_out else 0) + (N_IN_OPERANDS if has_in else 0)
    ins, outs = refs[:n_in], refs[n_in:]
    if has_out:
        out_ops, ins = ins[:N_OUT_OPERANDS], ins[N_OUT_OPERANDS:]
        xo_ref, outs = outs[0], outs[1:]
        rows = []
        for tok, x_new in _out_proj_chunks(x_ref, *out_ops, alpha):
            xo_ref[0, tok, :] = x_new
            rows.append(x_new.astype(BF16))
        xb = jnp.concatenate(rows, axis=0)
    else:
        xb = x_ref[0].astype(BF16)
    if has_in:
        _in_proj_body(xb, *ins, *outs)


def _mix(x, prev, nxt, tabs, tm, alpha):
    b, s, d = x.shape
    ins, in_specs = [x], [pl.BlockSpec((1, tm, d), lambda bi, i: (bi, i, 0))]
    out_shape, out_specs = [], []
    if prev is not None:
        om, od, osw, g, w_out, ln_g, ln_b, layer = prev
        tspec = lambda rows: pl.BlockSpec((1, rows, tm), lambda bi, i: (bi, 0, i))
        ins += [om, od, osw, g, w_out, ln_g, ln_b]
        in_specs += [tspec(W_MLA), tspec(W_DIFF), tspec(W_SWA), tspec(D_MIX),
                     _layer_spec(w_out, layer), _layer_spec(ln_g, layer), _layer_spec(ln_b, layer)]
        out_shape.append(jax.ShapeDtypeStruct((b, s, d), F32))
        out_specs.append(pl.BlockSpec((1, tm, d), lambda bi, i: (bi, i, 0)))
    if nxt is not None:
        lw, layer = nxt
        more_ins, more_specs = _in_proj_operands(lw, layer, tabs, tm)
        ins, in_specs = ins + more_ins, in_specs + more_specs
        shapes, specs = _in_proj_outputs(b, s, tm)
        out_shape, out_specs = out_shape + shapes, out_specs + specs
    return pl.pallas_call(
        functools.partial(_mix_kernel, has_out=prev is not None, has_in=nxt is not None, alpha=alpha),
        grid=(b, s // tm),
        in_specs=in_specs,
        out_specs=out_specs,
        out_shape=out_shape,
        compiler_params=_cparams(("parallel", "parallel")),
        name="mix_" + ("out" if prev is not None else "") + ("in" if nxt is not None else ""),
    )(*ins)


def _with_ones(v_t):
    tk = v_t.shape[1]
    row = lax.broadcasted_iota(jnp.int32, (ONES_ROWS, tk), 0)
    return jnp.concatenate([v_t, jnp.where(row == 0, 1.0, 0.0).astype(v_t.dtype)], axis=0)


def _online_step(s_t, tile_max, v_ext, m, acc):
    m_new = jnp.maximum(m, tile_max)
    p = jnp.exp2(s_t - m_new).astype(BF16)
    alpha = jnp.exp2(m - m_new)
    return m_new, alpha * acc + _nn(v_ext, p)


def _tile_off(j, t):
    return j * t if isinstance(j, int) else pl.multiple_of(j * t, t)


PIPE_TILES = 2
COL_BLOCK = 256
OUT_CHUNK = 256
MAX_UNROLLED_TRIPS = 16


def _pipelined_tiles(n_tiles, n_sub, produce_ctx, produce, consume_ctx, consume, carry, peel_first=False):
    u_n = PIPE_TILES
    trip = 2 * u_n
    assert n_tiles % trip == 0 and n_tiles >= trip

    def fill(slot, v0):
        for u in range(u_n):
            ctx = produce_ctx(v0 + u)
            for c in range(n_sub):
                produce(slot, u, c, ctx)

    def half(c_slot, v_c, p_slot, v_p, carry):
        carry = list(carry)
        for u in range(u_n):
            pctx = None if p_slot is None else produce_ctx(v_p + u)
            cctx = consume_ctx(v_c + u)
            for c in range(n_sub):
                carry[c] = consume(c_slot, u, c, cctx, carry[c])
                if p_slot is not None:
                    produce(p_slot, u, c, pctx)
        return tuple(carry)

    def body(v0, carry, last):
        carry = half(0, v0, 1, v0 + u_n, carry)
        return half(1, v0 + u_n, None if last else 0, v0 + trip, carry)

    fill(0, 0)
    n_trips = n_tiles // trip
    if n_trips <= MAX_UNROLLED_TRIPS:
        for i in range(n_trips):
            carry = body(i * trip, carry, i == n_trips - 1)
        return carry
    first = 0
    if peel_first:
        carry = body(0, carry, False)
        first = 1
    carry = lax.fori_loop(first, n_trips - 1, lambda i, c: body(i * trip, c, False), carry)
    return body((n_trips - 1) * trip, carry, True)


def _score_scratch(n_sub, tk):
    n = 2 * PIPE_TILES * n_sub
    return [pltpu.VMEM((1, tk, COL_BLOCK), F32)] * n + [pltpu.VMEM((1, 1, COL_BLOCK), F32)] * n


def _score_refs(bufs, n_sub):
    n = 2 * PIPE_TILES * n_sub
    z = jnp.minimum(pl.program_id(0), 0)
    at = lambda slot, u, c: (slot * PIPE_TILES + u) * n_sub + c
    return (lambda slot, u, c: bufs[at(slot, u, c)].at[z]), (lambda slot, u, c: bufs[n + at(slot, u, c)].at[z])


def _mla_attn_kernel(q_ref, k_ref, v_ref, o_ref, *bufs, tk, nk):
    tq = q_ref.shape[3]
    n_sub = tq // COL_BLOCK
    s_buf, mx_buf = _score_refs(bufs, n_sub)
    q_blk = [q_ref[0, 0, :, c * COL_BLOCK:(c + 1) * COL_BLOCK] for c in range(n_sub)]

    def produce_ctx(j):
        return k_ref[0, 0, pl.ds(_tile_off(j, tk), tk), :]

    def produce(slot, u, c, k):
        s_t = _nn(k, q_blk[c])
        s_buf(slot, u, c)[...] = s_t
        mx_buf(slot, u, c)[...] = jnp.max(s_t, axis=0, keepdims=True)

    def consume_ctx(j):
        return _with_ones(v_ref[0, 0, :, pl.ds(_tile_off(j, tk), tk)])

    def consume(slot, u, c, v_ext, carry):
        return _online_step(s_buf(slot, u, c)[...], mx_buf(slot, u, c)[...], v_ext, *carry)

    m0 = jnp.full((1, COL_BLOCK), NEG_INF, F32)
    acc0 = jnp.zeros((MLA_V + ONES_ROWS, COL_BLOCK), F32)
    carry = _pipelined_tiles(nk, n_sub, produce_ctx, produce, consume_ctx, consume, ((m0, acc0),) * n_sub)
    for c, (_, acc) in enumerate(carry):
        o_ref[0, :, c * COL_BLOCK:(c + 1) * COL_BLOCK] = (acc[:MLA_V] / acc[MLA_V:MLA_V + 1]).astype(o_ref.dtype)


def _mla_attention(qm, km, vm, tq, tk):
    b, h, _, s = qm.shape
    return pl.pallas_call(
        functools.partial(_mla_attn_kernel, tk=tk, nk=s // tk),
        grid=(b, h, s // tq),
        in_specs=[pl.BlockSpec((1, 1, LANES, tq), lambda bi, hi, qi: (bi, hi, 0, qi)),
                  pl.BlockSpec((1, 1, s, LANES), lambda bi, hi, qi: (bi, hi, 0, 0)),
                  pl.BlockSpec((1, 1, MLA_V, s), lambda bi, hi, qi: (bi, hi, 0, 0))],
        out_specs=pl.BlockSpec((1, MLA_V, tq), lambda bi, hi, qi: (bi, hi, qi)),
        out_shape=jax.ShapeDtypeStruct((b, W_MLA, s), BF16),
        scratch_shapes=_score_scratch(tq // COL_BLOCK, tk),
        compiler_params=_cparams(("parallel", "parallel", "arbitrary")),
        name="mla_attention",
    )(qm, km, vm)


NEAR_TILES = 4


def _diff_near_offsets(r):
    return min(-1, r - NEAR_TILES), NEAR_TILES - 1


def _diff_attn_kernel(far_ref, lam_ref, q_ref, k_ref, v_ref, bias_ref, subln_ref, o_ref,
                      qfar_buf, *bufs, tq, tk, nk, layer, lam_init):
    s_buf, mx_buf = _score_refs(bufs, 2 * (tq // COL_BLOCK))
    h = pl.program_id(1)
    qi = pl.program_id(2)
    t = tq
    r = tq // tk
    q_t = q_ref[0]
    r16 = lax.broadcasted_iota(jnp.int32, (ONES_ROWS, t), 0)
    zero_map = jnp.zeros((DIFF_QK, t), BF16)
    zero_pad = jnp.zeros((LANES - 2 * DIFF_QK - ONES_ROWS, t), BF16)

    def q_aug(mp, side):
        own = q_t[mp * DIFF_QK:(mp + 1) * DIFF_QK]
        own = [own, zero_map] if mp == 0 else [zero_map, own]
        if side is None:
            aug = jnp.zeros((ONES_ROWS, t), BF16)
        else:
            c = jnp.full((ONES_ROWS, t), far_ref[side, h], F32)
            c_hi = c.astype(BF16).astype(F32)
            aug = jnp.where(r16 == 0, c_hi, jnp.where(r16 == 1, c - c_hi, 0.0)).astype(BF16)
        return jnp.concatenate(own + [aug, zero_pad], axis=0)

    near_lo = jnp.clip(r * qi - 1, 0, nk - NEAR_TILES)
    e_min, _ = _diff_near_offsets(r)
    q_near = [q_aug(0, None), q_aug(1, None)]
    for side in range(2):
        for mp in range(2):
            qfar_buf[side, mp] = q_aug(mp, side)

    def key_tile(v):
        if isinstance(v, int) and v < NEAR_TILES:
            return near_lo + v, True, None
        idx = v - NEAR_TILES
        side = (idx >= near_lo).astype(jnp.int32)
        return idx + NEAR_TILES * side, False, side

    n_cb = tq // COL_BLOCK
    cols = lambda cb: slice(cb * COL_BLOCK, (cb + 1) * COL_BLOCK)

    def produce_ctx(v):
        j, near, side = key_tile(v)
        return k_ref[0, 0, pl.ds(_tile_off(j, tk), tk), :], near, side, j

    def produce(slot, u, c, ctx):
        k, near, side, j = ctx
        mp, cb = divmod(c, n_cb)
        if near:
            s_t = _nn(k, q_near[mp][:, cols(cb)]) + bias_ref[0, j - r * qi - e_min, :, cols(cb)]
        else:
            s_t = _nn(k, qfar_buf[side, mp, :, cols(cb)])
        s_buf(slot, u, c)[...] = s_t
        mx_buf(slot, u, c)[...] = jnp.max(s_t, axis=0, keepdims=True)

    def consume_ctx(v):
        j, _, _ = key_tile(v)
        return _with_ones(v_ref[0, :, pl.ds(_tile_off(j, tk), tk)])

    def consume(slot, u, c, v_ext, carry):
        return _online_step(s_buf(slot, u, c)[...], mx_buf(slot, u, c)[...], v_ext, *carry)

    m0 = jnp.full((1, COL_BLOCK), NEG_INF, F32)
    acc0 = jnp.zeros((DIFF_V + ONES_ROWS, COL_BLOCK), F32)
    carry = _pipelined_tiles(nk, 2 * n_cb, produce_ctx, produce, consume_ctx, consume,
                             ((m0, acc0),) * (2 * n_cb), peel_first=True)

    lam = lam_ref[layer, 0]
    for cb in range(n_cb):
        a0, a1 = carry[cb][1], carry[n_cb + cb][1]
        o = a0[:DIFF_V] / a0[DIFF_V:DIFF_V + 1] - lam * (a1[:DIFF_V] / a1[DIFF_V:DIFF_V + 1])
        ms = jnp.mean(o * o, axis=0, keepdims=True)
        o = o * lax.rsqrt(ms + RMS_EPS) * subln_ref[...] * (1.0 - lam_init)
        o_ref[0, :, cols(cb)] = o.astype(o_ref.dtype)


def _diff_attention(qd, kd, vd, bias_tiles, far, lam, subln_col, layer, tq, tk, lam_init):
    b, _, s = qd.shape
    n_e = bias_tiles.shape[1]
    assert tq % tk == 0 and tq // tk + 2 <= NEAR_TILES and tk >= REL_MAX_DIST
    smem = pl.BlockSpec(memory_space=pltpu.SMEM)
    return pl.pallas_call(
        functools.partial(_diff_attn_kernel, tq=tq, tk=tk, nk=s // tk, layer=layer, lam_init=lam_init),
        grid=(b, DIFF_HEADS, s // tq),
        in_specs=[smem, smem,
                  pl.BlockSpec((1, 2 * DIFF_QK, tq), lambda bi, hi, qi: (bi, hi, qi)),
                  pl.BlockSpec((1, 1, s, LANES), lambda bi, hi, qi: (bi, hi, 0, 0)),
                  pl.BlockSpec((1, DIFF_V, s), lambda bi, hi, qi: (bi, hi, 0)),
                  pl.BlockSpec((1, n_e, tk, tq), lambda bi, hi, qi: (hi, 0, 0, 0)),
                  _layer_spec(subln_col, layer)],
        out_specs=pl.BlockSpec((1, DIFF_V, tq), lambda bi, hi, qi: (bi, hi, qi)),
        out_shape=jax.ShapeDtypeStruct((b, W_DIFF, s), BF16),
        scratch_shapes=[pltpu.VMEM((2, 2, LANES, tq), BF16)] + _score_scratch(2 * (tq // COL_BLOCK), tk),
        compiler_params=_cparams(("parallel", "parallel", "arbitrary")),
        name="diff_attention",
    )(far, lam, qd, kd, vd, bias_tiles, subln_col)


def _swa_attn_kernel(sink_ref, q_ref, k0_ref, k1_ref, k2_ref, k3_ref, v0_ref, v1_ref, v2_ref, v3_ref,
                     bias_ref, o_ref, *, nq, layer):
    qi = pl.program_id(1)
    kb = [k0_ref[0], k1_ref[0], k2_ref[0], k3_ref[0]]
    vb = [v0_ref[0], v1_ref[0], v2_ref[0], v3_ref[0]]
    n_sb = len(kb) - 2
    k_win = [jnp.concatenate(kb[sb:sb + 3], axis=0) for sb in range(n_sb)]
    v_win = [jnp.concatenate(vb[sb:sb + 3], axis=1) for sb in range(n_sb)]
    variant = [jnp.where(qi == 0, 1, 0) if sb == 0 else jnp.where(qi == nq - 1, 2, 0) if sb == n_sb - 1 else 0
               for sb in range(n_sb)]
    grp = SWA_HEADS // SWA_KV_HEADS
    zeros = jnp.zeros((SWA_DIM, WINDOW), BF16)
    chains = [(hq, sb) for hq in range(SWA_HEADS) for sb in range(n_sb)]
    cols = lambda sb: slice(sb * WINDOW, (sb + 1) * WINDOW)
    s_t, p_t, den = {}, {}, {}
    for hq, sb in chains:
        q_t = q_ref[0, hq * SWA_DIM:(hq + 1) * SWA_DIM, cols(sb)]
        q_pad = jnp.concatenate([q_t, zeros] if hq // grp == 0 else [zeros, q_t], axis=0)
        s_t[hq, sb] = _nn(k_win[sb], q_pad) + bias_ref[variant[sb], hq]
    for hq, sb in chains:
        sink = sink_ref[layer, hq] * LOG2E
        m = jnp.maximum(jnp.max(s_t[hq, sb], axis=0, keepdims=True), sink)
        p = jnp.exp2(s_t[hq, sb] - m)
        den[hq, sb] = jnp.sum(p, axis=0, keepdims=True) + jnp.exp2(sink - m)
        p_t[hq, sb] = p.astype(BF16)
    for hq, sb in chains:
        g = hq // grp
        o = _nn(v_win[sb][g * SWA_DIM:(g + 1) * SWA_DIM], p_t[hq, sb]) / den[hq, sb]
        o_ref[0, hq * SWA_DIM:(hq + 1) * SWA_DIM, cols(sb)] = o.astype(o_ref.dtype)


def _swa_attention(sq, sk, sv, bias_tiles, sink, layer, tq):
    b, _, s = sq.shape
    assert tq % WINDOW == 0
    r = tq // WINDOW
    nblk = s // WINDOW
    smem = pl.BlockSpec(memory_space=pltpu.SMEM)
    wk = tq + 2 * WINDOW
    nw = wk // WINDOW

    def kspec(c):
        return pl.BlockSpec((1, WINDOW, LANES),
                            lambda bi, qi: (bi, jnp.clip(qi * r - 1 + c, 0, nblk - 1), 0))

    def vspec(c):
        return pl.BlockSpec((1, SWA_KV_HEADS * SWA_DIM, WINDOW),
                            lambda bi, qi: (bi, 0, jnp.clip(qi * r - 1 + c, 0, nblk - 1)))

    assert nw == 4
    return pl.pallas_call(
        functools.partial(_swa_attn_kernel, nq=s // tq, layer=layer),
        grid=(b, s // tq),
        in_specs=[smem, pl.BlockSpec((1, W_SWA, tq), lambda bi, qi: (bi, 0, qi))]
        + [kspec(c) for c in range(nw)] + [vspec(c) for c in range(nw)]
        + [pl.BlockSpec(bias_tiles.shape, lambda bi, qi: (0, 0, 0, 0))],
        out_specs=pl.BlockSpec((1, W_SWA, tq), lambda bi, qi: (bi, 0, qi)),
        out_shape=jax.ShapeDtypeStruct((b, W_SWA, s), BF16),
        compiler_params=_cparams(("parallel", "parallel")),
        name="swa_attention",
    )(sink, sq, sk, sk, sk, sk, sv, sv, sv, sv, bias_tiles)


def _layer_weights(w_in, q_norm, kv_norm, w_uq, w_ukv):
    cq, ckv, kr, dq, dk, dv, sq, sk, sv, gate = jnp.split(w_in, np.cumsum(IN_SIZES)[:-1].tolist(), axis=1)
    d = w_in.shape[0]
    half = MLA_ROPE // 2
    w_t = jnp.concatenate([cq, ckv, dq, dv, sq, sv, gate], axis=1).T.astype(BF16)
    pad = lambda a, lo, width: jnp.pad(a, ((0, 0), (lo, width - lo - a.shape[1])))
    kr_p = pad(kr, MLA_NOPE, LANES)
    kr_rot_p = pad(jnp.concatenate([-kr[:, half:], kr[:, :half]], axis=1), MLA_NOPE, LANES)
    dk_p = jnp.concatenate([pad(dk[:, h * 2 * DIFF_QK:(h + 1) * 2 * DIFF_QK], 0, LANES)
                            for h in range(DIFF_HEADS)], axis=1)
    w_r = jnp.concatenate([ckv, kr_p, kr_rot_p, dk_p, sk], axis=1).astype(BF16)
    assert w_t.shape == (_T_ROWS, d) and w_r.shape == (d, _R_COLS)
    qk = MLA_NOPE + MLA_ROPE
    w_uq_t = jnp.concatenate([jnp.pad(w_uq[:, h * qk:(h + 1) * qk].T, ((0, LANES - qk), (0, 0)))
                              for h in range(MLA_HEADS)], axis=0).astype(BF16)
    kv = MLA_NOPE + MLA_V
    w_kp = jnp.concatenate([pad(w_ukv[:, h * kv:h * kv + MLA_NOPE], 0, LANES)
                            for h in range(MLA_HEADS)], axis=1).astype(BF16)
    w_v_t = jnp.concatenate([w_ukv[:, h * kv + MLA_NOPE:(h + 1) * kv].T
                             for h in range(MLA_HEADS)], axis=0).astype(BF16)
    return dict(w_t=w_t, w_r=w_r, w_uq_t=w_uq_t, w_kp=w_kp, w_v_t=w_v_t,
                gq_col=q_norm.reshape(-1, 1), gkv_col=kv_norm.reshape(-1, 1), gkv_row=kv_norm.reshape(1, -1))


def _lambda_kernel(q1_ref, k1_ref, q2_ref, k2_ref, init_ref, o_ref):
    a = jnp.sum(q1_ref[...] * k1_ref[...], axis=1, keepdims=True)
    c = jnp.sum(q2_ref[...] * k2_ref[...], axis=1, keepdims=True)
    o_ref[...] = jnp.exp(a) - jnp.exp(c) + init_ref[...]


def _diff_lambda(lam_vecs, lam_inits):
    depth = lam_vecs.shape[0]
    return pl.pallas_call(
        _lambda_kernel,
        out_shape=jax.ShapeDtypeStruct((depth, 1), F32),
        name="diff_lambda",
    )(*(lam_vecs[:, i, :] for i in range(4)), jnp.asarray(np.asarray(lam_inits, np.float32).reshape(depth, 1)))


def _tiles(seq):
    return dict(tm=min(512, seq), mla_tq=min(1024, seq), mla_tk=256, diff_tq=min(512, seq), diff_tk=256, swa_tq=256)


def kernel(x, w_in, mla_q_norm, mla_kv_norm, mla_w_uq, mla_w_ukv, diff_lambda, diff_subln, swa_sink,
           rel_bias, w_out, ln_g, ln_b):
    depth = w_in.shape[0]
    seq = x.shape[1]
    cfg = _tiles(seq)
    alpha = (2 * depth) ** 0.25
    tabs = _rope_tables(seq)
    e_min, e_max = _diff_near_offsets(cfg["diff_tq"] // cfg["diff_tk"])
    diff_bias = _diff_bias_tiles(rel_bias, cfg["diff_tq"], cfg["diff_tk"], e_min, e_max)
    swa_bias = _swa_bias_tiles(rel_bias)
    half = REL_BUCKETS // 2
    far = jnp.stack([rel_bias[half - 1, :DIFF_HEADS], rel_bias[2 * half - 1, :DIFF_HEADS]]) * LOG2E
    lam_inits = [0.8 - 0.6 * math.exp(-0.3 * l) for l in range(depth)]
    lam = _diff_lambda(diff_lambda, lam_inits)
    lw = jax.vmap(_layer_weights)(w_in, mla_q_norm, mla_kv_norm, mla_w_uq, mla_w_ukv)
    subln_col = diff_subln[:, :, None]
    w_out_b, ln_g3, ln_b3 = w_out.astype(BF16), ln_g[:, None, :], ln_b[:, None, :]
    prev = None
    for l in range(depth + 1):
        nxt = (lw, l) if l < depth else None
        outs = _mix(x, prev, nxt, tabs, cfg["tm"], alpha)
        if prev is not None:
            x, outs = outs[0], outs[1:]
        if nxt is None:
            return x
        qm, km, vm, qd, kd, vd, sq, sk, sv, g = outs
        om = _mla_attention(qm, km, vm, cfg["mla_tq"], cfg["mla_tk"])
        od = _diff_attention(qd, kd, vd, diff_bias, far, lam, subln_col, l, cfg["diff_tq"], cfg["diff_tk"],
                             lam_inits[l])
        osw = _swa_attention(sq, sk, sv, swa_bias, swa_sink, l, cfg["swa_tq"])
        prev = (om, od, osw, g, w_out_b, ln_g3, ln_b3, l)
```

```python
import functools
import math

import numpy as np
import jax
import jax.numpy as jnp
from jax import lax
from jax.experimental import pallas as pl
from jax.experimental.pallas import tpu as pltpu

D_MODEL = 1024
MLA_HEADS, MLA_Q_RANK, MLA_KV_RANK, MLA_NOPE, MLA_ROPE, MLA_V = 6, 192, 128, 64, 32, 64
DIFF_HEADS, DIFF_QK, DIFF_V = 4, 32, 64
SWA_HEADS, SWA_KV_HEADS, SWA_DIM, WINDOW = 6, 2, 64, 128
W_MLA, W_DIFF, W_SWA = MLA_HEADS * MLA_V, DIFF_HEADS * DIFF_V, SWA_HEADS * SWA_DIM
D_MIX = W_MLA + W_DIFF + W_SWA
IN_SIZES = (MLA_Q_RANK, MLA_KV_RANK, MLA_ROPE, DIFF_HEADS * 2 * DIFF_QK, DIFF_HEADS * 2 * DIFF_QK,
            W_DIFF, W_SWA, SWA_KV_HEADS * SWA_DIM, SWA_KV_HEADS * SWA_DIM, D_MIX)
REL_BUCKETS, REL_MAX_DIST = 32, 128
ROPE_THETA = 10000.0
RMS_EPS, LN_EPS = 1e-6, 1e-5

LOG2E = math.log2(math.e)
LANES = 128
ONES_ROWS = 16
VMEM_LIMIT = 56 * 1024 * 1024
OUT_CHUNK = 256

F32, BF16 = jnp.float32, jnp.bfloat16
NEG_INF = float("-inf")


def _cparams(sem):
    return pltpu.CompilerParams(dimension_semantics=sem, vmem_limit_bytes=VMEM_LIMIT)


def _nt(a, b):
    return lax.dot_general(a, b, (((1,), (1,)), ((), ())), preferred_element_type=F32)


def _nn(a, b):
    return jnp.dot(a, b, preferred_element_type=F32)


def _rope_tables_kernel(fcol_ref, frow_ref, cos_t_ref, sin_t_ref, cos_p_ref, sin_p_ref, *, tile):
    p0 = pl.program_id(0) * tile
    pos_l = (p0 + lax.broadcasted_iota(jnp.int32, (MLA_ROPE // 2, tile), 1)).astype(F32)
    ang_t = pos_l * fcol_ref[...]
    cos_t_ref[...] = jnp.cos(ang_t)
    sin_t_ref[...] = jnp.sin(ang_t)
    pos_s = (p0 + lax.broadcasted_iota(jnp.int32, (tile, LANES), 0)).astype(F32)
    ang_p = pos_s * frow_ref[...]
    lane = lax.broadcasted_iota(jnp.int32, (tile, LANES), 1)
    live = (lane >= MLA_NOPE) & (lane < MLA_NOPE + MLA_ROPE)
    cos_p_ref[...] = jnp.where(live, jnp.cos(ang_p), 0.0)
    sin_p_ref[...] = jnp.where(live, jnp.sin(ang_p), 0.0)


def _rope_tables(seq):
    half = MLA_ROPE // 2
    freqs = (ROPE_THETA ** (-np.arange(half, dtype=np.float64) / half)).astype(np.float32)
    fcol = jnp.asarray(freqs.reshape(half, 1))
    frow_np = np.zeros((1, LANES), np.float32)
    frow_np[0, MLA_NOPE:MLA_NOPE + half] = freqs
    frow_np[0, MLA_NOPE + half:MLA_NOPE + MLA_ROPE] = freqs
    frow = jnp.asarray(frow_np)
    tile = min(seq, 1024)
    return pl.pallas_call(
        functools.partial(_rope_tables_kernel, tile=tile),
        grid=(seq // tile,),
        in_specs=[pl.BlockSpec((half, 1), lambda i: (0, 0)), pl.BlockSpec((1, LANES), lambda i: (0, 0))],
        out_specs=[pl.BlockSpec((half, tile), lambda i: (0, i)), pl.BlockSpec((half, tile), lambda i: (0, i)),
                   pl.BlockSpec((tile, LANES), lambda i: (i, 0)), pl.BlockSpec((tile, LANES), lambda i: (i, 0))],
        out_shape=[jax.ShapeDtypeStruct((half, seq), F32), jax.ShapeDtypeStruct((half, seq), F32),
                   jax.ShapeDtypeStruct((seq, LANES), F32), jax.ShapeDtypeStruct((seq, LANES), F32)],
        compiler_params=_cparams(("arbitrary",)),
        name="rope_tables",
    )(fcol, frow)


def _t5_bucket(rel):
    half = REL_BUCKETS // 2
    max_exact = half // 2
    n = jnp.abs(rel)
    large = jnp.full(rel.shape, max_exact, jnp.int32)
    for k in range(1, half - max_exact):
        thr = max_exact * (REL_MAX_DIST / max_exact) ** (k / (half - max_exact))
        thr_i = int(round(thr)) if abs(thr - round(thr)) < 1e-9 else int(math.ceil(thr))
        large = large + (n >= thr_i).astype(jnp.int32)
    return jnp.where(rel > 0, half, 0) + jnp.where(n < max_exact, n, large)


def _bias_lookup(tab_ref, bucket, head):
    out = jnp.zeros(bucket.shape, F32)
    for b in range(REL_BUCKETS):
        out = jnp.where(bucket == b, tab_ref[b, head], out)
    return out


def _diff_bias_kernel(tab_ref, out_ref, *, tq, tk, e_min):
    h = pl.program_id(0)
    d = (pl.program_id(1) + e_min) * tk
    ki = lax.broadcasted_iota(jnp.int32, (tk, tq), 0)
    qj = lax.broadcasted_iota(jnp.int32, (tk, tq), 1)
    out_ref[0, 0] = _bias_lookup(tab_ref, _t5_bucket(d + ki - qj), h) * LOG2E


def _diff_bias_tiles(rel_bias, tq, tk, e_min, e_max):
    n_e = e_max - e_min + 1
    return pl.pallas_call(
        functools.partial(_diff_bias_kernel, tq=tq, tk=tk, e_min=e_min),
        grid=(DIFF_HEADS, n_e),
        in_specs=[pl.BlockSpec(memory_space=pltpu.SMEM)],
        out_specs=pl.BlockSpec((1, 1, tk, tq), lambda h, d: (h, d, 0, 0)),
        out_shape=jax.ShapeDtypeStruct((DIFF_HEADS, n_e, tk, tq), F32),
        compiler_params=_cparams(("arbitrary", "arbitrary")),
        name="diff_bias_tiles",
    )(rel_bias)


def _swa_bias_kernel(tab_ref, out_ref):
    variant = pl.program_id(0)
    h = pl.program_id(1)
    ki = lax.broadcasted_iota(jnp.int32, (3 * WINDOW, WINDOW), 0)
    qj = lax.broadcasted_iota(jnp.int32, (3 * WINDOW, WINDOW), 1)
    rel = ki - WINDOW - qj
    bias = _bias_lookup(tab_ref, _t5_bucket(rel), DIFF_HEADS + h) * LOG2E
    lo = jnp.where(variant == 1, WINDOW, 0)
    hi = jnp.where(variant == 2, 2 * WINDOW, 3 * WINDOW)
    live = (jnp.abs(rel) <= WINDOW) & (ki >= lo) & (ki < hi)
    out_ref[0, 0] = jnp.where(live, bias, NEG_INF)


def _swa_bias_tiles(rel_bias):
    return pl.pallas_call(
        _swa_bias_kernel,
        grid=(3, SWA_HEADS),
        in_specs=[pl.BlockSpec(memory_space=pltpu.SMEM)],
        out_specs=pl.BlockSpec((1, 1, 3 * WINDOW, WINDOW), lambda v, h: (v, h, 0, 0)),
        out_shape=jax.ShapeDtypeStruct((3, SWA_HEADS, 3 * WINDOW, WINDOW), F32),
        compiler_params=_cparams(("arbitrary", "arbitrary")),
        name="swa_bias_tiles",
    )(rel_bias)


_T_CQ, _T_CKV, _T_DQ, _T_DV, _T_SQ, _T_SV, _T_GATE = 0, 192, 320, 576, 832, 1216, 1344
_T_ROWS = 2368
_R_CKV, _R_KR, _R_KRROT, _R_DK, _R_SK = 0, 128, 256, 384, 896
_R_COLS = 1024


def _rms_t(v, g_col):
    ms = jnp.mean(v * v, axis=0, keepdims=True)
    return v * lax.rsqrt(ms + RMS_EPS) * g_col


N_IN_OPERANDS, N_IN_OUTPUTS = 13, 10


def _in_proj_body(xb, wt_ref, wr_ref, wuq_ref, wkp_ref, wvt_ref, gq_ref, gkv_col_ref, gkv_row_ref,
                  cos_t_ref, sin_t_ref, cos_p_ref, sin_p_ref, ones_ref,
                  qm_ref, km_ref, vm_ref, qd_ref, kd_ref, vd_ref, sq_ref, sk_ref, sv_ref, g_ref):
    def proj_t(lo, hi):
        return _nt(wt_ref[lo:hi, :], xb)

    def gate_rows(lo, hi):
        gate = proj_t(_T_GATE + lo, _T_GATE + hi)
        g_ref[0, lo:hi, :] = (gate * jax.nn.sigmoid(gate)).astype(BF16)

    cq_t = proj_t(_T_CQ, _T_CKV)
    ckv_t = proj_t(_T_CKV, _T_DQ)
    r = _nn(xb, wr_ref[...])
    gate_rows(0, D_MIX // 2)

    cqn = _rms_t(cq_t, gq_ref[...]).astype(BF16)
    q_all = _nn(wuq_ref[...], cqn)
    ckvn_t = _rms_t(ckv_t, gkv_col_ref[...]).astype(BF16)
    v_all = _nn(wvt_ref[...], ckvn_t).astype(BF16)
    ckv = r[:, _R_CKV:_R_KR]
    ms = jnp.mean(ckv * ckv, axis=1, keepdims=True)
    ckvn = (ckv * lax.rsqrt(ms + RMS_EPS) * gkv_row_ref[...]).astype(BF16)
    k_nope = _nn(ckvn, wkp_ref[...])
    gate_rows(D_MIX // 2, D_MIX)

    cos_t, sin_t = cos_t_ref[...], sin_t_ref[...]
    q_scale = LOG2E / math.sqrt(MLA_NOPE + MLA_ROPE)
    half = MLA_ROPE // 2
    for h in range(MLA_HEADS):
        blk = q_all[h * LANES:(h + 1) * LANES]
        x1 = blk[MLA_NOPE:MLA_NOPE + half]
        x2 = blk[MLA_NOPE + half:MLA_NOPE + MLA_ROPE]
        roped = jnp.concatenate([blk[:MLA_NOPE], x1 * cos_t - x2 * sin_t, x2 * cos_t + x1 * sin_t,
                                 blk[MLA_NOPE + MLA_ROPE:]], axis=0)
        qm_ref[0, h] = (roped * q_scale).astype(BF16)
    for h in range(MLA_HEADS):
        vm_ref[0, h] = v_all[h * MLA_V:(h + 1) * MLA_V]
    qd_ref[0] = (proj_t(_T_DQ, _T_DV) * (LOG2E / math.sqrt(DIFF_QK))).astype(BF16)

    k_rope = r[:, _R_KR:_R_KRROT] * cos_p_ref[...] + r[:, _R_KRROT:_R_DK] * sin_p_ref[...]
    for h in range(MLA_HEADS):
        km_ref[0, h] = (k_nope[:, h * LANES:(h + 1) * LANES] + k_rope).astype(BF16)
    ones_row = ones_ref[...]
    for h in range(DIFF_HEADS):
        kd_ref[0, h] = (r[:, _R_DK + h * LANES:_R_DK + (h + 1) * LANES] + ones_row).astype(BF16)
    sk_ref[0] = r[:, _R_SK:_R_COLS].astype(BF16)

    vd_ref[0] = proj_t(_T_DV, _T_SQ).astype(BF16)
    sq_ref[0] = (proj_t(_T_SQ, _T_SV) * (LOG2E / math.sqrt(SWA_DIM))).astype(BF16)
    sv_ref[0] = proj_t(_T_SV, _T_GATE).astype(BF16)


def _layer_spec(a, layer):
    return pl.BlockSpec((None,) + a.shape[1:], lambda *_: (layer,) + (0,) * (a.ndim - 1))


def _in_proj_operands(lw, layer, tabs, tm):
    cos_t, sin_t, cos_p, sin_p = tabs
    ones_np = np.zeros((1, LANES), np.float32)
    ones_np[0, 2 * DIFF_QK:2 * DIFF_QK + 2] = 1.0
    ones_row = jnp.asarray(ones_np)
    half = MLA_ROPE // 2
    weights = [lw["w_t"], lw["w_r"], lw["w_uq_t"], lw["w_kp"], lw["w_v_t"], lw["gq_col"], lw["gkv_col"],
               lw["gkv_row"]]
    ins = weights + [cos_t, sin_t, cos_p, sin_p, ones_row]
    in_specs = [_layer_spec(a, layer) for a in weights]
    in_specs += [pl.BlockSpec((half, tm), lambda bi, i: (0, i)), pl.BlockSpec((half, tm), lambda bi, i: (0, i)),
                 pl.BlockSpec((tm, LANES), lambda bi, i: (i, 0)), pl.BlockSpec((tm, LANES), lambda bi, i: (i, 0)),
                 pl.BlockSpec(ones_row.shape, lambda bi, i: (0, 0))]
    assert len(ins) == N_IN_OPERANDS
    return ins, in_specs


def _in_proj_outputs(b, s, tm):
    sds = jax.ShapeDtypeStruct
    out_shape = [
        sds((b, MLA_HEADS, LANES, s), BF16),
        sds((b, MLA_HEADS, s, LANES), BF16),
        sds((b, MLA_HEADS, MLA_V, s), BF16),
        sds((b, DIFF_HEADS * 2 * DIFF_QK, s), BF16),
        sds((b, DIFF_HEADS, s, LANES), BF16),
        sds((b, W_DIFF, s), BF16),
        sds((b, W_SWA, s), BF16),
        sds((b, s, LANES), BF16),
        sds((b, SWA_KV_HEADS * SWA_DIM, s), BF16),
        sds((b, D_MIX, s), BF16),
    ]
    out_specs = [
        pl.BlockSpec((1, MLA_HEADS, LANES, tm), lambda bi, i: (bi, 0, 0, i)),
        pl.BlockSpec((1, MLA_HEADS, tm, LANES), lambda bi, i: (bi, 0, i, 0)),
        pl.BlockSpec((1, MLA_HEADS, MLA_V, tm), lambda bi, i: (bi, 0, 0, i)),
        pl.BlockSpec((1, DIFF_HEADS * 2 * DIFF_QK, tm), lambda bi, i: (bi, 0, i)),
        pl.BlockSpec((1, DIFF_HEADS, tm, LANES), lambda bi, i: (bi, 0, i, 0)),
        pl.BlockSpec((1, W_DIFF, tm), lambda bi, i: (bi, 0, i)),
        pl.BlockSpec((1, W_SWA, tm), lambda bi, i: (bi, 0, i)),
        pl.BlockSpec((1, tm, LANES), lambda bi, i: (bi, i, 0)),
        pl.BlockSpec((1, SWA_KV_HEADS * SWA_DIM, tm), lambda bi, i: (bi, 0, i)),
        pl.BlockSpec((1, D_MIX, tm), lambda bi, i: (bi, 0, i)),
    ]
    assert len(out_shape) == N_IN_OUTPUTS
    return out_shape, out_specs


def _n_out_operands(tm):
    return 8 + 2 * (tm // WINDOW + 2) + 1


def _swa_chunk(sink_ref, q_ref, kb, vb, bias_ref, variant, lo, layer):
    sbs = range(lo // WINDOW, (lo + OUT_CHUNK) // WINDOW)
    k_win = {sb: jnp.concatenate(kb[sb:sb + 3], axis=0) for sb in sbs}
    v_win = {sb: jnp.concatenate(vb[sb:sb + 3], axis=1) for sb in sbs}
    grp = SWA_HEADS // SWA_KV_HEADS
    zeros = jnp.zeros((SWA_DIM, WINDOW), BF16)
    chains = [(hq, sb) for hq in range(SWA_HEADS) for sb in sbs]
    s_t, p_t, den, o_t = {}, {}, {}, {}
    for hq, sb in chains:
        q_t = q_ref[0, hq * SWA_DIM:(hq + 1) * SWA_DIM, sb * WINDOW:(sb + 1) * WINDOW]
        q_pad = jnp.concatenate([q_t, zeros] if hq // grp == 0 else [zeros, q_t], axis=0)
        s_t[hq, sb] = _nn(k_win[sb], q_pad) + bias_ref[variant[sb], hq]
    for hq, sb in chains:
        sink = sink_ref[layer, hq] * LOG2E
        m = jnp.maximum(jnp.max(s_t[hq, sb], axis=0, keepdims=True), sink)
        p = jnp.exp2(s_t[hq, sb] - m)
        den[hq, sb] = jnp.sum(p, axis=0, keepdims=True) + jnp.exp2(sink - m)
        p_t[hq, sb] = p.astype(BF16)
    for hq, sb in chains:
        g = hq // grp
        o_t[hq, sb] = (_nn(v_win[sb][g * SWA_DIM:(g + 1) * SWA_DIM], p_t[hq, sb]) / den[hq, sb]).astype(BF16)
    return jnp.concatenate([jnp.concatenate([o_t[hq, sb] for sb in sbs], axis=1) for hq in range(SWA_HEADS)],
                           axis=0)


def _out_proj_chunks(x_ref, om_ref, od_ref, g_ref, w_ref, lng_ref, lnb_ref, sink_ref, sq_ref, *swa_refs,
                     alpha, layer, n_tiles):
    tm = x_ref.shape[1]
    n_kb = tm // WINDOW + 2
    kb = [r[0] for r in swa_refs[:n_kb]]
    vb = [r[0] for r in swa_refs[n_kb:2 * n_kb]]
    bias_ref = swa_refs[2 * n_kb]
    tile = pl.program_id(1)
    variant = [0] * (tm // WINDOW)
    variant[0] = jnp.where(tile == 0, 1, 0)
    variant[-1] = jnp.where(tile == n_tiles - 1, 2, 0)
    for lo in range(0, tm, OUT_CHUNK):
        tok = slice(lo, lo + OUT_CHUNK)
        os_t = _swa_chunk(sink_ref, sq_ref, kb, vb, bias_ref, variant, lo, layer)
        o_t = jnp.concatenate([om_ref[0, :, tok], od_ref[0, :, tok], os_t], axis=0)
        og_t = o_t * g_ref[0, :, tok]
        y = lax.dot_general(og_t, w_ref[...], (((0,), (0,)), ((), ())), preferred_element_type=F32)
        z = alpha * x_ref[0, tok, :] + y
        mu = jnp.mean(z, axis=1, keepdims=True)
        zc = z - mu
        var = jnp.mean(zc * zc, axis=1, keepdims=True)
        yield tok, zc * lax.rsqrt(var + LN_EPS) * lng_ref[...] + lnb_ref[...]


def _mix_kernel(x_ref, *refs, out_layer, has_in, alpha, n_tiles):
    has_out = out_layer is not None
    n_out = _n_out_operands(x_ref.shape[1]) if has_out else 0
    n_in = n_out + (N_IN_OPERANDS if has_in else 0)
    ins, outs = refs[:n_in], refs[n_in:]
    if has_out:
        out_ops, ins = ins[:n_out], ins[n_out:]
        xo_ref, outs = outs[0], outs[1:]
        rows = []
        for tok, x_new in _out_proj_chunks(x_ref, *out_ops, alpha=alpha, layer=out_layer, n_tiles=n_tiles):
            xo_ref[0, tok, :] = x_new
            rows.append(x_new.astype(BF16))
        xb = jnp.concatenate(rows, axis=0)
    else:
        xb = x_ref[0].astype(BF16)
    if has_in:
        _in_proj_body(xb, *ins, *outs)


def _mix(x, prev, nxt, tabs, tm, alpha):
    b, s, d = x.shape
    ins, in_specs = [x], [pl.BlockSpec((1, tm, d), lambda bi, i: (bi, i, 0))]
    out_shape, out_specs = [], []
    out_layer = None
    if prev is not None:
        om, od, g, sq, sk, sv, swa_bias, swa_sink, w_out, ln_g, ln_b, out_layer = prev
        assert tm % OUT_CHUNK == 0 and tm >= 2 * OUT_CHUNK and OUT_CHUNK % WINDOW == 0
        tspec = lambda rows: pl.BlockSpec((1, rows, tm), lambda bi, i: (bi, 0, i))
        r, nblk = tm // WINDOW, s // WINDOW
        n_kb = r + 2
        kspec = lambda c: pl.BlockSpec((1, WINDOW, LANES), lambda bi, i: (bi, jnp.clip(i * r - 1 + c, 0, nblk - 1), 0))
        vspec = lambda c: pl.BlockSpec((1, SWA_KV_HEADS * SWA_DIM, WINDOW),
                                       lambda bi, i: (bi, 0, jnp.clip(i * r - 1 + c, 0, nblk - 1)))
        ins += [om, od, g, w_out, ln_g, ln_b, swa_sink, sq] + [sk] * n_kb + [sv] * n_kb + [swa_bias]
        in_specs += [tspec(W_MLA), tspec(W_DIFF), tspec(D_MIX),
                     _layer_spec(w_out, out_layer), _layer_spec(ln_g, out_layer), _layer_spec(ln_b, out_layer),
                     pl.BlockSpec(memory_space=pltpu.SMEM), tspec(W_SWA)]
        in_specs += [kspec(c) for c in range(n_kb)] + [vspec(c) for c in range(n_kb)]
        in_specs += [pl.BlockSpec(swa_bias.shape, lambda bi, i: (0, 0, 0, 0))]
        assert len(ins) - 1 == _n_out_operands(tm)
        out_shape.append(jax.ShapeDtypeStruct((b, s, d), F32))
        out_specs.append(pl.BlockSpec((1, tm, d), lambda bi, i: (bi, i, 0)))
    if nxt is not None:
        lw, layer = nxt
        more_ins, more_specs = _in_proj_operands(lw, layer, tabs, tm)
        ins, in_specs = ins + more_ins, in_specs + more_specs
        shapes, specs = _in_proj_outputs(b, s, tm)
        out_shape, out_specs = out_shape + shapes, out_specs + specs
    return pl.pallas_call(
        functools.partial(_mix_kernel, out_layer=out_layer, has_in=nxt is not None, alpha=alpha, n_tiles=s // tm),
        grid=(b, s // tm),
        in_specs=in_specs,
        out_specs=out_specs,
        out_shape=out_shape,
        compiler_params=_cparams(("parallel", "parallel")),
        name="mix_" + ("out" if prev is not None else "") + ("in" if nxt is not None else ""),
    )(*ins)


def _with_ones(v_t):
    tk = v_t.shape[1]
    row = lax.broadcasted_iota(jnp.int32, (ONES_ROWS, tk), 0)
    return jnp.concatenate([v_t, jnp.where(row == 0, 1.0, 0.0).astype(v_t.dtype)], axis=0)


def _online_step(s_t, tile_max, v_ext, m, acc):
    m_new = jnp.maximum(m, tile_max)
    p = jnp.exp2(s_t - m_new).astype(BF16)
    alpha = jnp.exp2(m - m_new)
    return m_new, alpha * acc + _nn(v_ext, p)


def _tile_off(j, t):
    return j * t if isinstance(j, int) else pl.multiple_of(j * t, t)


PIPE_TILES = 2
COL_BLOCK = 256
MAX_UNROLLED_TRIPS = 16


def _pipelined_tiles(n_tiles, n_sub, produce_ctx, produce, consume_ctx, consume, carry, peel_first=False):
    u_n = PIPE_TILES
    trip = 2 * u_n
    assert n_tiles % trip == 0 and n_tiles >= trip

    def fill(slot, v0):
        for u in range(u_n):
            ctx = produce_ctx(v0 + u)
            for c in range(n_sub):
                produce(slot, u, c, ctx)

    def half(c_slot, v_c, p_slot, v_p, carry):
        carry = list(carry)
        for u in range(u_n):
            pctx = None if p_slot is None else produce_ctx(v_p + u)
            cctx = consume_ctx(v_c + u)
            for c in range(n_sub):
                carry[c] = consume(c_slot, u, c, cctx, carry[c])
                if p_slot is not None:
                    produce(p_slot, u, c, pctx)
        return tuple(carry)

    def body(v0, carry, last):
        carry = half(0, v0, 1, v0 + u_n, carry)
        return half(1, v0 + u_n, None if last else 0, v0 + trip, carry)

    fill(0, 0)
    n_trips = n_tiles // trip
    if n_trips <= MAX_UNROLLED_TRIPS:
        for i in range(n_trips):
            carry = body(i * trip, carry, i == n_trips - 1)
        return carry
    first = 0
    if peel_first:
        carry = body(0, carry, False)
        first = 1
    carry = lax.fori_loop(first, n_trips - 1, lambda i, c: body(i * trip, c, False), carry)
    return body((n_trips - 1) * trip, carry, True)


def _score_scratch(n_sub, tk):
    n = 2 * PIPE_TILES * n_sub
    return [pltpu.VMEM((1, tk, COL_BLOCK), F32)] * n + [pltpu.VMEM((1, 1, COL_BLOCK), F32)] * n


def _score_refs(bufs, n_sub):
    n = 2 * PIPE_TILES * n_sub
    z = jnp.minimum(pl.program_id(0), 0)
    at = lambda slot, u, c: (slot * PIPE_TILES + u) * n_sub + c
    return (lambda slot, u, c: bufs[at(slot, u, c)].at[z]), (lambda slot, u, c: bufs[n + at(slot, u, c)].at[z])


def _mla_attn_kernel(q_ref, k_ref, v_ref, o_ref, *bufs, tk, nk):
    tq = q_ref.shape[3]
    n_sub = tq // COL_BLOCK
    s_buf, mx_buf = _score_refs(bufs, n_sub)
    q_blk = [q_ref[0, 0, :, c * COL_BLOCK:(c + 1) * COL_BLOCK] for c in range(n_sub)]

    def produce_ctx(j):
        return k_ref[0, 0, pl.ds(_tile_off(j, tk), tk), :]

    def produce(slot, u, c, k):
        s_t = _nn(k, q_blk[c])
        s_buf(slot, u, c)[...] = s_t
        mx_buf(slot, u, c)[...] = jnp.max(s_t, axis=0, keepdims=True)

    def consume_ctx(j):
        return _with_ones(v_ref[0, 0, :, pl.ds(_tile_off(j, tk), tk)])

    def consume(slot, u, c, v_ext, carry):
        return _online_step(s_buf(slot, u, c)[...], mx_buf(slot, u, c)[...], v_ext, *carry)

    m0 = jnp.full((1, COL_BLOCK), NEG_INF, F32)
    acc0 = jnp.zeros((MLA_V + ONES_ROWS, COL_BLOCK), F32)
    carry = _pipelined_tiles(nk, n_sub, produce_ctx, produce, consume_ctx, consume, ((m0, acc0),) * n_sub)
    for c, (_, acc) in enumerate(carry):
        o_ref[0, :, c * COL_BLOCK:(c + 1) * COL_BLOCK] = (acc[:MLA_V] / acc[MLA_V:MLA_V + 1]).astype(o_ref.dtype)


def _mla_attention(qm, km, vm, tq, tk):
    b, h, _, s = qm.shape
    return pl.pallas_call(
        functools.partial(_mla_attn_kernel, tk=tk, nk=s // tk),
        grid=(b, h, s // tq),
        in_specs=[pl.BlockSpec((1, 1, LANES, tq), lambda bi, hi, qi: (bi, hi, 0, qi)),
                  pl.BlockSpec((1, 1, s, LANES), lambda bi, hi, qi: (bi, hi, 0, 0)),
                  pl.BlockSpec((1, 1, MLA_V, s), lambda bi, hi, qi: (bi, hi, 0, 0))],
        out_specs=pl.BlockSpec((1, MLA_V, tq), lambda bi, hi, qi: (bi, hi, qi)),
        out_shape=jax.ShapeDtypeStruct((b, W_MLA, s), BF16),
        scratch_shapes=_score_scratch(tq // COL_BLOCK, tk),
        compiler_params=_cparams(("parallel", "parallel", "arbitrary")),
        name="mla_attention",
    )(qm, km, vm)


NEAR_TILES = 4


def _diff_near_offsets(r):
    return min(-1, r - NEAR_TILES), NEAR_TILES - 1


def _diff_attn_kernel(far_ref, lam_ref, q_ref, k_ref, v_ref, bias_ref, subln_ref, o_ref,
                      qfar_buf, *bufs, tq, tk, nk, layer, lam_init):
    s_buf, mx_buf = _score_refs(bufs, 2 * (tq // COL_BLOCK))
    h = pl.program_id(1)
    qi = pl.program_id(2)
    t = tq
    r = tq // tk
    q_t = q_ref[0]
    r16 = lax.broadcasted_iota(jnp.int32, (ONES_ROWS, t), 0)
    zero_map = jnp.zeros((DIFF_QK, t), BF16)
    zero_pad = jnp.zeros((LANES - 2 * DIFF_QK - ONES_ROWS, t), BF16)

    def q_aug(mp, side):
        own = q_t[mp * DIFF_QK:(mp + 1) * DIFF_QK]
        own = [own, zero_map] if mp == 0 else [zero_map, own]
        if side is None:
            aug = jnp.zeros((ONES_ROWS, t), BF16)
        else:
            c = jnp.full((ONES_ROWS, t), far_ref[side, h], F32)
            c_hi = c.astype(BF16).astype(F32)
            aug = jnp.where(r16 == 0, c_hi, jnp.where(r16 == 1, c - c_hi, 0.0)).astype(BF16)
        return jnp.concatenate(own + [aug, zero_pad], axis=0)

    near_lo = jnp.clip(r * qi - 1, 0, nk - NEAR_TILES)
    e_min, _ = _diff_near_offsets(r)
    q_near = [q_aug(0, None), q_aug(1, None)]
    for side in range(2):
        for mp in range(2):
            qfar_buf[side, mp] = q_aug(mp, side)

    def key_tile(v):
        if isinstance(v, int) and v < NEAR_TILES:
            return near_lo + v, True, None
        idx = v - NEAR_TILES
        side = (idx >= near_lo).astype(jnp.int32)
        return idx + NEAR_TILES * side, False, side

    n_cb = tq // COL_BLOCK
    cols = lambda cb: slice(cb * COL_BLOCK, (cb + 1) * COL_BLOCK)

    def produce_ctx(v):
        j, near, side = key_tile(v)
        return k_ref[0, 0, pl.ds(_tile_off(j, tk), tk), :], near, side, j

    def produce(slot, u, c, ctx):
        k, near, side, j = ctx
        mp, cb = divmod(c, n_cb)
        if near:
            s_t = _nn(k, q_near[mp][:, cols(cb)]) + bias_ref[0, j - r * qi - e_min, :, cols(cb)]
        else:
            s_t = _nn(k, qfar_buf[side, mp, :, cols(cb)])
        s_buf(slot, u, c)[...] = s_t
        mx_buf(slot, u, c)[...] = jnp.max(s_t, axis=0, keepdims=True)

    def consume_ctx(v):
        j, _, _ = key_tile(v)
        return _with_ones(v_ref[0, :, pl.ds(_tile_off(j, tk), tk)])

    def consume(slot, u, c, v_ext, carry):
        return _online_step(s_buf(slot, u, c)[...], mx_buf(slot, u, c)[...], v_ext, *carry)

    m0 = jnp.full((1, COL_BLOCK), NEG_INF, F32)
    acc0 = jnp.zeros((DIFF_V + ONES_ROWS, COL_BLOCK), F32)
    carry = _pipelined_tiles(nk, 2 * n_cb, produce_ctx, produce, consume_ctx, consume,
                             ((m0, acc0),) * (2 * n_cb), peel_first=True)

    lam = lam_ref[layer, 0]
    for cb in range(n_cb):
        a0, a1 = carry[cb][1], carry[n_cb + cb][1]
        o = a0[:DIFF_V] / a0[DIFF_V:DIFF_V + 1] - lam * (a1[:DIFF_V] / a1[DIFF_V:DIFF_V + 1])
        ms = jnp.mean(o * o, axis=0, keepdims=True)
        o = o * lax.rsqrt(ms + RMS_EPS) * subln_ref[...] * (1.0 - lam_init)
        o_ref[0, :, cols(cb)] = o.astype(o_ref.dtype)


def _diff_attention(qd, kd, vd, bias_tiles, far, lam, subln_col, layer, tq, tk, lam_init):
    b, _, s = qd.shape
    n_e = bias_tiles.shape[1]
    assert tq % tk == 0 and tq // tk + 2 <= NEAR_TILES and tk >= REL_MAX_DIST
    smem = pl.BlockSpec(memory_space=pltpu.SMEM)
    return pl.pallas_call(
        functools.partial(_diff_attn_kernel, tq=tq, tk=tk, nk=s // tk, layer=layer, lam_init=lam_init),
        grid=(b, DIFF_HEADS, s // tq),
        in_specs=[smem, smem,
                  pl.BlockSpec((1, 2 * DIFF_QK, tq), lambda bi, hi, qi: (bi, hi, qi)),
                  pl.BlockSpec((1, 1, s, LANES), lambda bi, hi, qi: (bi, hi, 0, 0)),
                  pl.BlockSpec((1, DIFF_V, s), lambda bi, hi, qi: (bi, hi, 0)),
                  pl.BlockSpec((1, n_e, tk, tq), lambda bi, hi, qi: (hi, 0, 0, 0)),
                  _layer_spec(subln_col, layer)],
        out_specs=pl.BlockSpec((1, DIFF_V, tq), lambda bi, hi, qi: (bi, hi, qi)),
        out_shape=jax.ShapeDtypeStruct((b, W_DIFF, s), BF16),
        scratch_shapes=[pltpu.VMEM((2, 2, LANES, tq), BF16)] + _score_scratch(2 * (tq // COL_BLOCK), tk),
        compiler_params=_cparams(("parallel", "parallel", "arbitrary")),
        name="diff_attention",
    )(far, lam, qd, kd, vd, bias_tiles, subln_col)


def _layer_weights(w_in, q_norm, kv_norm, w_uq, w_ukv):
    cq, ckv, kr, dq, dk, dv, sq, sk, sv, gate = jnp.split(w_in, np.cumsum(IN_SIZES)[:-1].tolist(), axis=1)
    d = w_in.shape[0]
    half = MLA_ROPE // 2
    w_t = jnp.concatenate([cq, ckv, dq, dv, sq, sv, gate], axis=1).T.astype(BF16)
    pad = lambda a, lo, width: jnp.pad(a, ((0, 0), (lo, width - lo - a.shape[1])))
    kr_p = pad(kr, MLA_NOPE, LANES)
    kr_rot_p = pad(jnp.concatenate([-kr[:, half:], kr[:, :half]], axis=1), MLA_NOPE, LANES)
    dk_p = jnp.concatenate([pad(dk[:, h * 2 * DIFF_QK:(h + 1) * 2 * DIFF_QK], 0, LANES)
                            for h in range(DIFF_HEADS)], axis=1)
    w_r = jnp.concatenate([ckv, kr_p, kr_rot_p, dk_p, sk], axis=1).astype(BF16)
    assert w_t.shape == (_T_ROWS, d) and w_r.shape == (d, _R_COLS)
    qk = MLA_NOPE + MLA_ROPE
    w_uq_t = jnp.concatenate([jnp.pad(w_uq[:, h * qk:(h + 1) * qk].T, ((0, LANES - qk), (0, 0)))
                              for h in range(MLA_HEADS)], axis=0).astype(BF16)
    kv = MLA_NOPE + MLA_V
    w_kp = jnp.concatenate([pad(w_ukv[:, h * kv:h * kv + MLA_NOPE], 0, LANES)
                            for h in range(MLA_HEADS)], axis=1).astype(BF16)
    w_v_t = jnp.concatenate([w_ukv[:, h * kv + MLA_NOPE:(h + 1) * kv].T
                             for h in range(MLA_HEADS)], axis=0).astype(BF16)
    return dict(w_t=w_t, w_r=w_r, w_uq_t=w_uq_t, w_kp=w_kp, w_v_t=w_v_t,
                gq_col=q_norm.reshape(-1, 1), gkv_col=kv_norm.reshape(-1, 1), gkv_row=kv_norm.reshape(1, -1))


def _lambda_kernel(q1_ref, k1_ref, q2_ref, k2_ref, init_ref, o_ref):
    a = jnp.sum(q1_ref[...] * k1_ref[...], axis=1, keepdims=True)
    c = jnp.sum(q2_ref[...] * k2_ref[...], axis=1, keepdims=True)
    o_ref[...] = jnp.exp(a) - jnp.exp(c) + init_ref[...]


def _diff_lambda(lam_vecs, lam_inits):
    depth = lam_vecs.shape[0]
    return pl.pallas_call(
        _lambda_kernel,
        out_shape=jax.ShapeDtypeStruct((depth, 1), F32),
        name="diff_lambda",
    )(*(lam_vecs[:, i, :] for i in range(4)), jnp.asarray(np.asarray(lam_inits, np.float32).reshape(depth, 1)))


def _tiles(seq):
    return dict(tm=min(512, seq), mla_tq=min(1024, seq), mla_tk=256, diff_tq=min(512, seq), diff_tk=256)


def kernel(x, w_in, mla_q_norm, mla_kv_norm, mla_w_uq, mla_w_ukv, diff_lambda, diff_subln, swa_sink,
           rel_bias, w_out, ln_g, ln_b):
    depth = w_in.shape[0]
    seq = x.shape[1]
    cfg = _tiles(seq)
    alpha = (2 * depth) ** 0.25
    tabs = _rope_tables(seq)
    e_min, e_max = _diff_near_offsets(cfg["diff_tq"] // cfg["diff_tk"])
    diff_bias = _diff_bias_tiles(rel_bias, cfg["diff_tq"], cfg["diff_tk"], e_min, e_max)
    swa_bias = _swa_bias_tiles(rel_bias)
    half = REL_BUCKETS // 2
    far = jnp.stack([rel_bias[half - 1, :DIFF_HEADS], rel_bias[2 * half - 1, :DIFF_HEADS]]) * LOG2E
    lam_inits = [0.8 - 0.6 * math.exp(-0.3 * l) for l in range(depth)]
    lam = _diff_lambda(diff_lambda, lam_inits)
    lw = jax.vmap(_layer_weights)(w_in, mla_q_norm, mla_kv_norm, mla_w_uq, mla_w_ukv)
    subln_col = diff_subln[:, :, None]
    w_out_b, ln_g3, ln_b3 = w_out.astype(BF16), ln_g[:, None, :], ln_b[:, None, :]
    prev = None
    for l in range(depth + 1):
        nxt = (lw, l) if l < depth else None
        outs = _mix(x, prev, nxt, tabs, cfg["tm"], alpha)
        if prev is not None:
            x, outs = outs[0], outs[1:]
        if nxt is None:
            return x
        qm, km, vm, qd, kd, vd, sq, sk, sv, g = outs
        om = _mla_attention(qm, km, vm, cfg["mla_tq"], cfg["mla_tk"])
        od = _diff_attention(qd, kd, vd, diff_bias, far, lam, subln_col, l, cfg["diff_tq"], cfg["diff_tk"],
                             lam_inits[l])
        prev = (om, od, g, sq, sk, sv, swa_bias, swa_sink, w_out_b, ln_g3, ln_b3, l)
```

```python
import functools
import math

import numpy as np
import jax
import jax.numpy as jnp
from jax import lax
from jax.experimental import pallas as pl
from jax.experimental.pallas import tpu as pltpu

D_MODEL = 1024
MLA_HEADS, MLA_Q_RANK, MLA_KV_RANK, MLA_NOPE, MLA_ROPE, MLA_V = 6, 192, 128, 64, 32, 64
DIFF_HEADS, DIFF_QK, DIFF_V = 4, 32, 64
SWA_HEADS, SWA_KV_HEADS, SWA_DIM, WINDOW = 6, 2, 64, 128
W_MLA, W_DIFF, W_SWA = MLA_HEADS * MLA_V, DIFF_HEADS * DIFF_V, SWA_HEADS * SWA_DIM
D_MIX = W_MLA + W_DIFF + W_SWA
IN_SIZES = (MLA_Q_RANK, MLA_KV_RANK, MLA_ROPE, DIFF_HEADS * 2 * DIFF_QK, DIFF_HEADS * 2 * DIFF_QK,
            W_DIFF, W_SWA, SWA_KV_HEADS * SWA_DIM, SWA_KV_HEADS * SWA_DIM, D_MIX)
REL_BUCKETS, REL_MAX_DIST = 32, 128
ROPE_THETA = 10000.0
RMS_EPS, LN_EPS = 1e-6, 1e-5

LOG2E = math.log2(math.e)
LANES = 128
ONES_ROWS = 16
VMEM_LIMIT = 56 * 1024 * 1024
OUT_CHUNK = 256

F32, BF16 = jnp.float32, jnp.bfloat16
NEG_INF = float("-inf")


def _cparams(sem):
    return pltpu.CompilerParams(dimension_semantics=sem, vmem_limit_bytes=VMEM_LIMIT)


def _nt(a, b):
    return lax.dot_general(a, b, (((1,), (1,)), ((), ())), preferred_element_type=F32)


def _nn(a, b):
    return jnp.dot(a, b, preferred_element_type=F32)


def _rope_tables_kernel(fcol_ref, frow_ref, cos_t_ref, sin_t_ref, cos_p_ref, sin_p_ref, *, tile):
    p0 = pl.program_id(0) * tile
    pos_l = (p0 + lax.broadcasted_iota(jnp.int32, (MLA_ROPE // 2, tile), 1)).astype(F32)
    ang_t = pos_l * fcol_ref[...]
    cos_t_ref[...] = jnp.cos(ang_t)
    sin_t_ref[...] = jnp.sin(ang_t)
    pos_s = (p0 + lax.broadcasted_iota(jnp.int32, (tile, LANES), 0)).astype(F32)
    ang_p = pos_s * frow_ref[...]
    lane = lax.broadcasted_iota(jnp.int32, (tile, LANES), 1)
    live = (lane >= MLA_NOPE) & (lane < MLA_NOPE + MLA_ROPE)
    cos_p_ref[...] = jnp.where(live, jnp.cos(ang_p), 0.0)
    sin_p_ref[...] = jnp.where(live, jnp.sin(ang_p), 0.0)


def _rope_tables(seq):
    half = MLA_ROPE // 2
    freqs = (ROPE_THETA ** (-np.arange(half, dtype=np.float64) / half)).astype(np.float32)
    fcol = jnp.asarray(freqs.reshape(half, 1))
    frow_np = np.zeros((1, LANES), np.float32)
    frow_np[0, MLA_NOPE:MLA_NOPE + half] = freqs
    frow_np[0, MLA_NOPE + half:MLA_NOPE + MLA_ROPE] = freqs
    frow = jnp.asarray(frow_np)
    tile = min(seq, 1024)
    return pl.pallas_call(
        functools.partial(_rope_tables_kernel, tile=tile),
        grid=(seq // tile,),
        in_specs=[pl.BlockSpec((half, 1), lambda i: (0, 0)), pl.BlockSpec((1, LANES), lambda i: (0, 0))],
        out_specs=[pl.BlockSpec((half, tile), lambda i: (0, i)), pl.BlockSpec((half, tile), lambda i: (0, i)),
                   pl.BlockSpec((tile, LANES), lambda i: (i, 0)), pl.BlockSpec((tile, LANES), lambda i: (i, 0))],
        out_shape=[jax.ShapeDtypeStruct((half, seq), F32), jax.ShapeDtypeStruct((half, seq), F32),
                   jax.ShapeDtypeStruct((seq, LANES), F32), jax.ShapeDtypeStruct((seq, LANES), F32)],
        compiler_params=_cparams(("arbitrary",)),
        name="rope_tables",
    )(fcol, frow)


def _t5_bucket(rel):
    half = REL_BUCKETS // 2
    max_exact = half // 2
    n = jnp.abs(rel)
    large = jnp.full(rel.shape, max_exact, jnp.int32)
    for k in range(1, half - max_exact):
        thr = max_exact * (REL_MAX_DIST / max_exact) ** (k / (half - max_exact))
        thr_i = int(round(thr)) if abs(thr - round(thr)) < 1e-9 else int(math.ceil(thr))
        large = large + (n >= thr_i).astype(jnp.int32)
    return jnp.where(rel > 0, half, 0) + jnp.where(n < max_exact, n, large)


def _bias_lookup(tab_ref, bucket, head):
    out = jnp.zeros(bucket.shape, F32)
    for b in range(REL_BUCKETS):
        out = jnp.where(bucket == b, tab_ref[b, head], out)
    return out


def _diff_bias_kernel(tab_ref, out_ref, *, tq, tk, e_min):
    h = pl.program_id(0)
    d = (pl.program_id(1) + e_min) * tk
    ki = lax.broadcasted_iota(jnp.int32, (tk, tq), 0)
    qj = lax.broadcasted_iota(jnp.int32, (tk, tq), 1)
    out_ref[0, 0] = _bias_lookup(tab_ref, _t5_bucket(d + ki - qj), h) * LOG2E


def _diff_bias_tiles(rel_bias, tq, tk, e_min, e_max):
    n_e = e_max - e_min + 1
    return pl.pallas_call(
        functools.partial(_diff_bias_kernel, tq=tq, tk=tk, e_min=e_min),
        grid=(DIFF_HEADS, n_e),
        in_specs=[pl.BlockSpec(memory_space=pltpu.SMEM)],
        out_specs=pl.BlockSpec((1, 1, tk, tq), lambda h, d: (h, d, 0, 0)),
        out_shape=jax.ShapeDtypeStruct((DIFF_HEADS, n_e, tk, tq), F32),
        compiler_params=_cparams(("arbitrary", "arbitrary")),
        name="diff_bias_tiles",
    )(rel_bias)


def _swa_bias_kernel(tab_ref, out_ref):
    variant = pl.program_id(0)
    h = pl.program_id(1)
    ki = lax.broadcasted_iota(jnp.int32, (3 * WINDOW, WINDOW), 0)
    qj = lax.broadcasted_iota(jnp.int32, (3 * WINDOW, WINDOW), 1)
    rel = ki - WINDOW - qj
    bias = _bias_lookup(tab_ref, _t5_bucket(rel), DIFF_HEADS + h) * LOG2E
    lo = jnp.where(variant == 1, WINDOW, 0)
    hi = jnp.where(variant == 2, 2 * WINDOW, 3 * WINDOW)
    live = (jnp.abs(rel) <= WINDOW) & (ki >= lo) & (ki < hi)
    out_ref[0, 0] = jnp.where(live, bias, NEG_INF)


def _swa_bias_tiles(rel_bias):
    return pl.pallas_call(
        _swa_bias_kernel,
        grid=(3, SWA_HEADS),
        in_specs=[pl.BlockSpec(memory_space=pltpu.SMEM)],
        out_specs=pl.BlockSpec((1, 1, 3 * WINDOW, WINDOW), lambda v, h: (v, h, 0, 0)),
        out_shape=jax.ShapeDtypeStruct((3, SWA_HEADS, 3 * WINDOW, WINDOW), F32),
        compiler_params=_cparams(("arbitrary", "arbitrary")),
        name="swa_bias_tiles",
    )(rel_bias)


_T_CQ, _T_CKV, _T_DQ, _T_DV, _T_SQ, _T_SV, _T_GATE = 0, 192, 320, 576, 832, 1216, 1344
_T_ROWS = 2368
_R_CKV, _R_KR, _R_KRROT, _R_DK, _R_SK = 0, 128, 256, 384, 896
_R_COLS = 1024


def _rms_t(v, g_col):
    ms = jnp.mean(v * v, axis=0, keepdims=True)
    return v * lax.rsqrt(ms + RMS_EPS) * g_col


N_IN_OPERANDS, N_IN_OUTPUTS = 13, 10


def _in_proj_body(xb, wt_ref, wr_ref, wuq_ref, wkp_ref, wvt_ref, gq_ref, gkv_col_ref, gkv_row_ref,
                  cos_t_ref, sin_t_ref, cos_p_ref, sin_p_ref, ones_ref,
                  qm_ref, km_ref, vm_ref, qd_ref, kd_ref, vd_ref, sq_ref, sk_ref, sv_ref, g_ref):
    def proj_t(lo, hi):
        return _nt(wt_ref[lo:hi, :], xb)

    def gate_rows(lo, hi):
        gate = proj_t(_T_GATE + lo, _T_GATE + hi)
        g_ref[0, lo:hi, :] = (gate * jax.nn.sigmoid(gate)).astype(BF16)

    cq_t = proj_t(_T_CQ, _T_CKV)
    ckv_t = proj_t(_T_CKV, _T_DQ)
    r = _nn(xb, wr_ref[...])
    gate_rows(0, D_MIX // 2)

    cqn = _rms_t(cq_t, gq_ref[...]).astype(BF16)
    q_all = _nn(wuq_ref[...], cqn)
    ckvn_t = _rms_t(ckv_t, gkv_col_ref[...]).astype(BF16)
    v_all = _nn(wvt_ref[...], ckvn_t).astype(BF16)
    ckv = r[:, _R_CKV:_R_KR]
    ms = jnp.mean(ckv * ckv, axis=1, keepdims=True)
    ckvn = (ckv * lax.rsqrt(ms + RMS_EPS) * gkv_row_ref[...]).astype(BF16)
    k_nope = _nn(ckvn, wkp_ref[...])
    gate_rows(D_MIX // 2, D_MIX)

    cos_t, sin_t = cos_t_ref[...], sin_t_ref[...]
    q_scale = LOG2E / math.sqrt(MLA_NOPE + MLA_ROPE)
    half = MLA_ROPE // 2
    for h in range(MLA_HEADS):
        blk = q_all[h * LANES:(h + 1) * LANES]
        x1 = blk[MLA_NOPE:MLA_NOPE + half]
        x2 = blk[MLA_NOPE + half:MLA_NOPE + MLA_ROPE]
        roped = jnp.concatenate([blk[:MLA_NOPE], x1 * cos_t - x2 * sin_t, x2 * cos_t + x1 * sin_t,
                                 blk[MLA_NOPE + MLA_ROPE:]], axis=0)
        qm_ref[0, h] = (roped * q_scale).astype(BF16)
    for h in range(MLA_HEADS):
        vm_ref[0, h] = v_all[h * MLA_V:(h + 1) * MLA_V]
    qd_ref[0] = (proj_t(_T_DQ, _T_DV) * (LOG2E / math.sqrt(DIFF_QK))).astype(BF16)

    k_rope = r[:, _R_KR:_R_KRROT] * cos_p_ref[...] + r[:, _R_KRROT:_R_DK] * sin_p_ref[...]
    for h in range(MLA_HEADS):
        km_ref[0, h] = (k_nope[:, h * LANES:(h + 1) * LANES] + k_rope).astype(BF16)
    ones_row = ones_ref[...]
    for h in range(DIFF_HEADS):
        kd_ref[0, h] = (r[:, _R_DK + h * LANES:_R_DK + (h + 1) * LANES] + ones_row).astype(BF16)
    sk_ref[0] = r[:, _R_SK:_R_COLS].astype(BF16)

    vd_ref[0] = proj_t(_T_DV, _T_SQ).astype(BF16)
    sq_ref[0] = (proj_t(_T_SQ, _T_SV) * (LOG2E / math.sqrt(SWA_DIM))).astype(BF16)
    sv_ref[0] = proj_t(_T_SV, _T_GATE).astype(BF16)


def _layer_spec(a, layer):
    return pl.BlockSpec((None,) + a.shape[1:], lambda *_: (layer,) + (0,) * (a.ndim - 1))


def _in_proj_operands(lw, layer, tabs, tm):
    cos_t, sin_t, cos_p, sin_p = tabs
    ones_np = np.zeros((1, LANES), np.float32)
    ones_np[0, 2 * DIFF_QK:2 * DIFF_QK + 2] = 1.0
    ones_row = jnp.asarray(ones_np)
    half = MLA_ROPE // 2
    weights = [lw["w_t"], lw["w_r"], lw["w_uq_t"], lw["w_kp"], lw["w_v_t"], lw["gq_col"], lw["gkv_col"],
               lw["gkv_row"]]
    ins = weights + [cos_t, sin_t, cos_p, sin_p, ones_row]
    in_specs = [_layer_spec(a, layer) for a in weights]
    in_specs += [pl.BlockSpec((half, tm), lambda bi, i: (0, i)), pl.BlockSpec((half, tm), lambda bi, i: (0, i)),
                 pl.BlockSpec((tm, LANES), lambda bi, i: (i, 0)), pl.BlockSpec((tm, LANES), lambda bi, i: (i, 0)),
                 pl.BlockSpec(ones_row.shape, lambda bi, i: (0, 0))]
    assert len(ins) == N_IN_OPERANDS
    return ins, in_specs


def _in_proj_outputs(b, s, tm):
    sds = jax.ShapeDtypeStruct
    out_shape = [
        sds((b, MLA_HEADS, LANES, s), BF16),
        sds((b, MLA_HEADS, s, LANES), BF16),
        sds((b, MLA_HEADS, MLA_V, s), BF16),
        sds((b, DIFF_HEADS * 2 * DIFF_QK, s), BF16),
        sds((b, DIFF_HEADS, s, LANES), BF16),
        sds((b, W_DIFF, s), BF16),
        sds((b, W_SWA, s), BF16),
        sds((b, s, LANES), BF16),
        sds((b, SWA_KV_HEADS * SWA_DIM, s), BF16),
        sds((b, D_MIX, s), BF16),
    ]
    out_specs = [
        pl.BlockSpec((1, MLA_HEADS, LANES, tm), lambda bi, i: (bi, 0, 0, i)),
        pl.BlockSpec((1, MLA_HEADS, tm, LANES), lambda bi, i: (bi, 0, i, 0)),
        pl.BlockSpec((1, MLA_HEADS, MLA_V, tm), lambda bi, i: (bi, 0, 0, i)),
        pl.BlockSpec((1, DIFF_HEADS * 2 * DIFF_QK, tm), lambda bi, i: (bi, 0, i)),
        pl.BlockSpec((1, DIFF_HEADS, tm, LANES), lambda bi, i: (bi, 0, i, 0)),
        pl.BlockSpec((1, W_DIFF, tm), lambda bi, i: (bi, 0, i)),
        pl.BlockSpec((1, W_SWA, tm), lambda bi, i: (bi, 0, i)),
        pl.BlockSpec((1, tm, LANES), lambda bi, i: (bi, i, 0)),
        pl.BlockSpec((1, SWA_KV_HEADS * SWA_DIM, tm), lambda bi, i: (bi, 0, i)),
        pl.BlockSpec((1, D_MIX, tm), lambda bi, i: (bi, 0, i)),
    ]
    assert len(out_shape) == N_IN_OUTPUTS
    return out_shape, out_specs


def _n_out_operands(tm):
    return 8 + 2 * (tm // WINDOW + 2) + 1


def _swa_chunk(sink_ref, q_ref, kb, vb, bias_ref, variant, lo, layer):
    sbs = range(lo // WINDOW, (lo + OUT_CHUNK) // WINDOW)
    k_win = {sb: jnp.concatenate(kb[sb:sb + 3], axis=0) for sb in sbs}
    v_win = {sb: jnp.concatenate(vb[sb:sb + 3], axis=1) for sb in sbs}
    grp = SWA_HEADS // SWA_KV_HEADS
    zeros = jnp.zeros((SWA_DIM, WINDOW), BF16)
    chains = [(hq, sb) for hq in range(SWA_HEADS) for sb in sbs]
    s_t, p_t, den, o_t = {}, {}, {}, {}
    for hq, sb in chains:
        q_t = q_ref[0, hq * SWA_DIM:(hq + 1) * SWA_DIM, sb * WINDOW:(sb + 1) * WINDOW]
        q_pad = jnp.concatenate([q_t, zeros] if hq // grp == 0 else [zeros, q_t], axis=0)
        s_t[hq, sb] = _nn(k_win[sb], q_pad) + bias_ref[variant[sb], hq]
    for hq, sb in chains:
        sink = sink_ref[layer, hq] * LOG2E
        m = jnp.maximum(jnp.max(s_t[hq, sb], axis=0, keepdims=True), sink)
        p = jnp.exp2(s_t[hq, sb] - m)
        den[hq, sb] = jnp.sum(p, axis=0, keepdims=True) + jnp.exp2(sink - m)
        p_t[hq, sb] = p.astype(BF16)
    for hq, sb in chains:
        g = hq // grp
        o_t[hq, sb] = (_nn(v_win[sb][g * SWA_DIM:(g + 1) * SWA_DIM], p_t[hq, sb]) / den[hq, sb]).astype(BF16)
    return jnp.concatenate([jnp.concatenate([o_t[hq, sb] for sb in sbs], axis=1) for hq in range(SWA_HEADS)],
                           axis=0)


def _out_proj_chunks(x_ref, om_ref, od_ref, g_ref, w_ref, lng_ref, lnb_ref, sink_ref, sq_ref, *swa_refs,
                     alpha, layer, n_tiles):
    tm = x_ref.shape[1]
    n_kb = tm // WINDOW + 2
    kb = [r[0] for r in swa_refs[:n_kb]]
    vb = [r[0] for r in swa_refs[n_kb:2 * n_kb]]
    bias_ref = swa_refs[2 * n_kb]
    tile = pl.program_id(1)
    variant = [0] * (tm // WINDOW)
    variant[0] = jnp.where(tile == 0, 1, 0)
    variant[-1] = jnp.where(tile == n_tiles - 1, 2, 0)
    for lo in range(0, tm, OUT_CHUNK):
        tok = slice(lo, lo + OUT_CHUNK)
        os_t = _swa_chunk(sink_ref, sq_ref, kb, vb, bias_ref, variant, lo, layer)
        o_t = jnp.concatenate([om_ref[0, :, tok], od_ref[0, :, tok], os_t], axis=0)
        og_t = o_t * g_ref[0, :, tok]
        y = lax.dot_general(og_t, w_ref[...], (((0,), (0,)), ((), ())), preferred_element_type=F32)
        z = alpha * x_ref[0, tok, :] + y
        mu = jnp.mean(z, axis=1, keepdims=True)
        zc = z - mu
        var = jnp.mean(zc * zc, axis=1, keepdims=True)
        yield tok, zc * lax.rsqrt(var + LN_EPS) * lng_ref[...] + lnb_ref[...]


def _mix_kernel(x_ref, *refs, out_layer, has_in, alpha, n_tiles):
    has_out = out_layer is not None
    n_out = _n_out_operands(x_ref.shape[1]) if has_out else 0
    n_in = n_out + (N_IN_OPERANDS if has_in else 0)
    ins, outs = refs[:n_in], refs[n_in:]
    if has_out:
        out_ops, ins = ins[:n_out], ins[n_out:]
        xo_ref, outs = outs[0], outs[1:]
        rows = []
        for tok, x_new in _out_proj_chunks(x_ref, *out_ops, alpha=alpha, layer=out_layer, n_tiles=n_tiles):
            xo_ref[0, tok, :] = x_new
            rows.append(x_new.astype(BF16))
        xb = jnp.concatenate(rows, axis=0)
    else:
        xb = x_ref[0].astype(BF16)
    if has_in:
        _in_proj_body(xb, *ins, *outs)


def _mix(x, prev, nxt, tabs, tm, alpha):
    b, s, d = x.shape
    ins, in_specs = [x], [pl.BlockSpec((1, tm, d), lambda bi, i: (bi, i, 0))]
    out_shape, out_specs = [], []
    out_layer = None
    if prev is not None:
        om, od, g, sq, sk, sv, swa_bias, swa_sink, w_out, ln_g, ln_b, out_layer = prev
        assert tm % OUT_CHUNK == 0 and tm >= 2 * OUT_CHUNK and OUT_CHUNK % WINDOW == 0
        tspec = lambda rows: pl.BlockSpec((1, rows, tm), lambda bi, i: (bi, 0, i))
        r, nblk = tm // WINDOW, s // WINDOW
        n_kb = r + 2
        kspec = lambda c: pl.BlockSpec((1, WINDOW, LANES), lambda bi, i: (bi, jnp.clip(i * r - 1 + c, 0, nblk - 1), 0))
        vspec = lambda c: pl.BlockSpec((1, SWA_KV_HEADS * SWA_DIM, WINDOW),
                                       lambda bi, i: (bi, 0, jnp.clip(i * r - 1 + c, 0, nblk - 1)))
        ins += [om, od, g, w_out, ln_g, ln_b, swa_sink, sq] + [sk] * n_kb + [sv] * n_kb + [swa_bias]
        in_specs += [tspec(W_MLA), tspec(W_DIFF), tspec(D_MIX),
                     _layer_spec(w_out, out_layer), _layer_spec(ln_g, out_layer), _layer_spec(ln_b, out_layer),
                     pl.BlockSpec(memory_space=pltpu.SMEM), tspec(W_SWA)]
        in_specs += [kspec(c) for c in range(n_kb)] + [vspec(c) for c in range(n_kb)]
        in_specs += [pl.BlockSpec(swa_bias.shape, lambda bi, i: (0, 0, 0, 0))]
        assert len(ins) - 1 == _n_out_operands(tm)
        out_shape.append(jax.ShapeDtypeStruct((b, s, d), F32))
        out_specs.append(pl.BlockSpec((1, tm, d), lambda bi, i: (bi, i, 0)))
    if nxt is not None:
        lw, layer = nxt
        more_ins, more_specs = _in_proj_operands(lw, layer, tabs, tm)
        ins, in_specs = ins + more_ins, in_specs + more_specs
        shapes, specs = _in_proj_outputs(b, s, tm)
        out_shape, out_specs = out_shape + shapes, out_specs + specs
    return pl.pallas_call(
        functools.partial(_mix_kernel, out_layer=out_layer, has_in=nxt is not None, alpha=alpha, n_tiles=s // tm),
        grid=(b, s // tm),
        in_specs=in_specs,
        out_specs=out_specs,
        out_shape=out_shape,
        compiler_params=_cparams(("parallel", "parallel")),
        name="mix_" + ("out" if prev is not None else "") + ("in" if nxt is not None else ""),
    )(*ins)


def _with_ones(v_t):
    tk = v_t.shape[1]
    row = lax.broadcasted_iota(jnp.int32, (ONES_ROWS, tk), 0)
    return jnp.concatenate([v_t, jnp.where(row == 0, 1.0, 0.0).astype(v_t.dtype)], axis=0)


def _online_step(s_t, tile_max, v_ext, m, acc):
    m_new = jnp.maximum(m, tile_max)
    p = jnp.exp2(s_t - m_new).astype(BF16)
    alpha = jnp.exp2(m - m_new)
    return m_new, alpha * acc + _nn(v_ext, p)


def _tile_off(j, t):
    return j * t if isinstance(j, int) else pl.multiple_of(j * t, t)


PIPE_TILES = 2
COL_BLOCK = 256
MAX_UNROLLED_TILES = 64


def _pipelined_tiles(n_tiles, n_sub, produce_ctx, produce, consume_ctx, consume, carry):
    u_n = PIPE_TILES
    assert n_tiles % u_n == 0 and n_tiles <= MAX_UNROLLED_TILES
    carry = list(carry)
    for u in range(u_n):
        ctx = produce_ctx(u)
        for c in range(n_sub):
            produce(0, u, c, ctx)
    for half in range(n_tiles // u_n):
        c_slot, v_c = half % 2, half * u_n
        v_p = v_c + u_n
        for u in range(u_n):
            pctx = produce_ctx(v_p + u) if v_p < n_tiles else None
            cctx = consume_ctx(v_c + u)
            for c in range(n_sub):
                carry[c] = consume(c_slot, u, c, cctx, carry[c])
                if pctx is not None:
                    produce(1 - c_slot, u, c, pctx)
    return tuple(carry)


def _score_scratch(n_sub, tk):
    n = 2 * PIPE_TILES * n_sub
    return [pltpu.VMEM((1, tk, COL_BLOCK), F32)] * n + [pltpu.VMEM((1, 1, COL_BLOCK), F32)] * n


def _score_refs(bufs, n_sub):
    n = 2 * PIPE_TILES * n_sub
    z = jnp.minimum(pl.program_id(0), 0)
    at = lambda slot, u, c: (slot * PIPE_TILES + u) * n_sub + c
    return (lambda slot, u, c: bufs[at(slot, u, c)].at[z]), (lambda slot, u, c: bufs[n + at(slot, u, c)].at[z])


def _mla_attn_kernel(q_ref, k_ref, v_ref, o_ref, *bufs, tk, nk):
    tq = q_ref.shape[3]
    n_sub = tq // COL_BLOCK
    s_buf, mx_buf = _score_refs(bufs, n_sub)
    q_blk = [q_ref[0, 0, :, c * COL_BLOCK:(c + 1) * COL_BLOCK] for c in range(n_sub)]

    def produce_ctx(j):
        return k_ref[0, 0, pl.ds(_tile_off(j, tk), tk), :]

    def produce(slot, u, c, k):
        s_t = _nn(k, q_blk[c])
        s_buf(slot, u, c)[...] = s_t
        mx_buf(slot, u, c)[...] = jnp.max(s_t, axis=0, keepdims=True)

    def consume_ctx(j):
        return _with_ones(v_ref[0, 0, :, pl.ds(_tile_off(j, tk), tk)])

    def consume(slot, u, c, v_ext, carry):
        return _online_step(s_buf(slot, u, c)[...], mx_buf(slot, u, c)[...], v_ext, *carry)

    m0 = jnp.full((1, COL_BLOCK), NEG_INF, F32)
    acc0 = jnp.zeros((MLA_V + ONES_ROWS, COL_BLOCK), F32)
    carry = _pipelined_tiles(nk, n_sub, produce_ctx, produce, consume_ctx, consume, ((m0, acc0),) * n_sub)
    for c, (_, acc) in enumerate(carry):
        o_ref[0, :, c * COL_BLOCK:(c + 1) * COL_BLOCK] = (acc[:MLA_V] / acc[MLA_V:MLA_V + 1]).astype(o_ref.dtype)


def _mla_attention(qm, km, vm, tq, tk):
    b, h, _, s = qm.shape
    return pl.pallas_call(
        functools.partial(_mla_attn_kernel, tk=tk, nk=s // tk),
        grid=(b, h, s // tq),
        in_specs=[pl.BlockSpec((1, 1, LANES, tq), lambda bi, hi, qi: (bi, hi, 0, qi)),
                  pl.BlockSpec((1, 1, s, LANES), lambda bi, hi, qi: (bi, hi, 0, 0)),
                  pl.BlockSpec((1, 1, MLA_V, s), lambda bi, hi, qi: (bi, hi, 0, 0))],
        out_specs=pl.BlockSpec((1, MLA_V, tq), lambda bi, hi, qi: (bi, hi, qi)),
        out_shape=jax.ShapeDtypeStruct((b, W_MLA, s), BF16),
        scratch_shapes=_score_scratch(tq // COL_BLOCK, tk),
        compiler_params=_cparams(("parallel", "parallel", "arbitrary")),
        name="mla_attention",
    )(qm, km, vm)


NEAR_TILES = 4


def _diff_near_offsets(r):
    return min(-1, r - NEAR_TILES), NEAR_TILES - 1


def _diff_attn_kernel(far_ref, lam_ref, q_ref, k_ref, v_ref, bias_ref, subln_ref, o_ref,
                      qfar_buf, *bufs, tq, tk, nk, layer, lam_init):
    s_buf, mx_buf = _score_refs(bufs, 2 * (tq // COL_BLOCK))
    h = pl.program_id(1)
    qi = pl.program_id(2)
    t = tq
    r = tq // tk
    q_t = q_ref[0]
    r16 = lax.broadcasted_iota(jnp.int32, (ONES_ROWS, t), 0)
    zero_map = jnp.zeros((DIFF_QK, t), BF16)
    zero_pad = jnp.zeros((LANES - 2 * DIFF_QK - ONES_ROWS, t), BF16)

    def q_aug(mp, side):
        own = q_t[mp * DIFF_QK:(mp + 1) * DIFF_QK]
        own = [own, zero_map] if mp == 0 else [zero_map, own]
        if side is None:
            aug = jnp.zeros((ONES_ROWS, t), BF16)
        else:
            c = jnp.full((ONES_ROWS, t), far_ref[side, h], F32)
            c_hi = c.astype(BF16).astype(F32)
            aug = jnp.where(r16 == 0, c_hi, jnp.where(r16 == 1, c - c_hi, 0.0)).astype(BF16)
        return jnp.concatenate(own + [aug, zero_pad], axis=0)

    near_lo = jnp.clip(r * qi - 1, 0, nk - NEAR_TILES)
    e_min, _ = _diff_near_offsets(r)
    q_near = [q_aug(0, None), q_aug(1, None)]
    for side in range(2):
        for mp in range(2):
            qfar_buf[side, mp] = q_aug(mp, side)

    def key_tile(v):
        if v < NEAR_TILES:
            return near_lo + v, True, None
        idx = v - NEAR_TILES
        side = (idx >= near_lo).astype(jnp.int32)
        return idx + NEAR_TILES * side, False, side

    n_cb = tq // COL_BLOCK
    cols = lambda cb: slice(cb * COL_BLOCK, (cb + 1) * COL_BLOCK)

    def produce_ctx(v):
        j, near, side = key_tile(v)
        return k_ref[0, 0, pl.ds(_tile_off(j, tk), tk), :], near, side, j

    def produce(slot, u, c, ctx):
        k, near, side, j = ctx
        mp, cb = divmod(c, n_cb)
        if near:
            s_t = _nn(k, q_near[mp][:, cols(cb)]) + bias_ref[0, j - r * qi - e_min, :, cols(cb)]
        else:
            s_t = _nn(k, qfar_buf[side, mp, :, cols(cb)])
        s_buf(slot, u, c)[...] = s_t
        mx_buf(slot, u, c)[...] = jnp.max(s_t, axis=0, keepdims=True)

    def consume_ctx(v):
        j, _, _ = key_tile(v)
        return _with_ones(v_ref[0, :, pl.ds(_tile_off(j, tk), tk)])

    def consume(slot, u, c, v_ext, carry):
        return _online_step(s_buf(slot, u, c)[...], mx_buf(slot, u, c)[...], v_ext, *carry)

    m0 = jnp.full((1, COL_BLOCK), NEG_INF, F32)
    acc0 = jnp.zeros((DIFF_V + ONES_ROWS, COL_BLOCK), F32)
    carry = _pipelined_tiles(nk, 2 * n_cb, produce_ctx, produce, consume_ctx, consume,
                             ((m0, acc0),) * (2 * n_cb))

    lam = lam_ref[layer, 0]
    for cb in range(n_cb):
        a0, a1 = carry[cb][1], carry[n_cb + cb][1]
        o = a0[:DIFF_V] / a0[DIFF_V:DIFF_V + 1] - lam * (a1[:DIFF_V] / a1[DIFF_V:DIFF_V + 1])
        ms = jnp.mean(o * o, axis=0, keepdims=True)
        o = o * lax.rsqrt(ms + RMS_EPS) * subln_ref[...] * (1.0 - lam_init)
        o_ref[0, :, cols(cb)] = o.astype(o_ref.dtype)


def _diff_attention(qd, kd, vd, bias_tiles, far, lam, subln_col, layer, tq, tk, lam_init):
    b, _, s = qd.shape
    n_e = bias_tiles.shape[1]
    assert tq % tk == 0 and tq // tk + 2 <= NEAR_TILES and tk >= REL_MAX_DIST
    smem = pl.BlockSpec(memory_space=pltpu.SMEM)
    return pl.pallas_call(
        functools.partial(_diff_attn_kernel, tq=tq, tk=tk, nk=s // tk, layer=layer, lam_init=lam_init),
        grid=(b, DIFF_HEADS, s // tq),
        in_specs=[smem, smem,
                  pl.BlockSpec((1, 2 * DIFF_QK, tq), lambda bi, hi, qi: (bi, hi, qi)),
                  pl.BlockSpec((1, 1, s, LANES), lambda bi, hi, qi: (bi, hi, 0, 0)),
                  pl.BlockSpec((1, DIFF_V, s), lambda bi, hi, qi: (bi, hi, 0)),
                  pl.BlockSpec((1, n_e, tk, tq), lambda bi, hi, qi: (hi, 0, 0, 0)),
                  _layer_spec(subln_col, layer)],
        out_specs=pl.BlockSpec((1, DIFF_V, tq), lambda bi, hi, qi: (bi, hi, qi)),
        out_shape=jax.ShapeDtypeStruct((b, W_DIFF, s), BF16),
        scratch_shapes=[pltpu.VMEM((2, 2, LANES, tq), BF16)] + _score_scratch(2 * (tq // COL_BLOCK), tk),
        compiler_params=_cparams(("parallel", "parallel", "arbitrary")),
        name="diff_attention",
    )(far, lam, qd, kd, vd, bias_tiles, subln_col)


def _layer_weights(w_in, q_norm, kv_norm, w_uq, w_ukv):
    cq, ckv, kr, dq, dk, dv, sq, sk, sv, gate = jnp.split(w_in, np.cumsum(IN_SIZES)[:-1].tolist(), axis=1)
    d = w_in.shape[0]
    half = MLA_ROPE // 2
    w_t = jnp.concatenate([cq, ckv, dq, dv, sq, sv, gate], axis=1).T.astype(BF16)
    pad = lambda a, lo, width: jnp.pad(a, ((0, 0), (lo, width - lo - a.shape[1])))
    kr_p = pad(kr, MLA_NOPE, LANES)
    kr_rot_p = pad(jnp.concatenate([-kr[:, half:], kr[:, :half]], axis=1), MLA_NOPE, LANES)
    dk_p = jnp.concatenate([pad(dk[:, h * 2 * DIFF_QK:(h + 1) * 2 * DIFF_QK], 0, LANES)
                            for h in range(DIFF_HEADS)], axis=1)
    w_r = jnp.concatenate([ckv, kr_p, kr_rot_p, dk_p, sk], axis=1).astype(BF16)
    assert w_t.shape == (_T_ROWS, d) and w_r.shape == (d, _R_COLS)
    qk = MLA_NOPE + MLA_ROPE
    w_uq_t = jnp.concatenate([jnp.pad(w_uq[:, h * qk:(h + 1) * qk].T, ((0, LANES - qk), (0, 0)))
                              for h in range(MLA_HEADS)], axis=0).astype(BF16)
    kv = MLA_NOPE + MLA_V
    w_kp = jnp.concatenate([pad(w_ukv[:, h * kv:h * kv + MLA_NOPE], 0, LANES)
                            for h in range(MLA_HEADS)], axis=1).astype(BF16)
    w_v_t = jnp.concatenate([w_ukv[:, h * kv + MLA_NOPE:(h + 1) * kv].T
                             for h in range(MLA_HEADS)], axis=0).astype(BF16)
    return dict(w_t=w_t, w_r=w_r, w_uq_t=w_uq_t, w_kp=w_kp, w_v_t=w_v_t,
                gq_col=q_norm.reshape(-1, 1), gkv_col=kv_norm.reshape(-1, 1), gkv_row=kv_norm.reshape(1, -1))


def _lambda_kernel(q1_ref, k1_ref, q2_ref, k2_ref, init_ref, o_ref):
    a = jnp.sum(q1_ref[...] * k1_ref[...], axis=1, keepdims=True)
    c = jnp.sum(q2_ref[...] * k2_ref[...], axis=1, keepdims=True)
    o_ref[...] = jnp.exp(a) - jnp.exp(c) + init_ref[...]


def _diff_lambda(lam_vecs, lam_inits):
    depth = lam_vecs.shape[0]
    return pl.pallas_call(
        _lambda_kernel,
        out_shape=jax.ShapeDtypeStruct((depth, 1), F32),
        name="diff_lambda",
    )(*(lam_vecs[:, i, :] for i in range(4)), jnp.asarray(np.asarray(lam_inits, np.float32).reshape(depth, 1)))


def _tiles(seq):
    return dict(tm=min(512, seq), mla_tq=min(1024, seq), mla_tk=256, diff_tq=min(512, seq), diff_tk=256)


def kernel(x, w_in, mla_q_norm, mla_kv_norm, mla_w_uq, mla_w_ukv, diff_lambda, diff_subln, swa_sink,
           rel_bias, w_out, ln_g, ln_b):
    depth = w_in.shape[0]
    seq = x.shape[1]
    cfg = _tiles(seq)
    alpha = (2 * depth) ** 0.25
    tabs = _rope_tables(seq)
    e_min, e_max = _diff_near_offsets(cfg["diff_tq"] // cfg["diff_tk"])
    diff_bias = _diff_bias_tiles(rel_bias, cfg["diff_tq"], cfg["diff_tk"], e_min, e_max)
    swa_bias = _swa_bias_tiles(rel_bias)
    half = REL_BUCKETS // 2
    far = jnp.stack([rel_bias[half - 1, :DIFF_HEADS], rel_bias[2 * half - 1, :DIFF_HEADS]]) * LOG2E
    lam_inits = [0.8 - 0.6 * math.exp(-0.3 * l) for l in range(depth)]
    lam = _diff_lambda(diff_lambda, lam_inits)
    lw = jax.vmap(_layer_weights)(w_in, mla_q_norm, mla_kv_norm, mla_w_uq, mla_w_ukv)
    subln_col = diff_subln[:, :, None]
    w_out_b, ln_g3, ln_b3 = w_out.astype(BF16), ln_g[:, None, :], ln_b[:, None, :]
    prev = None
    for l in range(depth + 1):
        nxt = (lw, l) if l < depth else None
        outs = _mix(x, prev, nxt, tabs, cfg["tm"], alpha)
        if prev is not None:
            x, outs = outs[0], outs[1:]
        if nxt is None:
            return x
        qm, km, vm, qd, kd, vd, sq, sk, sv, g = outs
        om = _mla_attention(qm, km, vm, cfg["mla_tq"], cfg["mla_tk"])
        od = _diff_attention(qd, kd, vd, diff_bias, far, lam, subln_col, l, cfg["diff_tq"], cfg["diff_tk"],
                             lam_inits[l])
        prev = (om, od, g, sq, sk, sv, swa_bias, swa_sink, w_out_b, ln_g3, ln_b3, l)
```

```python
import functools
import math

import numpy as np
import jax
import jax.numpy as jnp
from jax import lax
from jax.experimental import pallas as pl
from jax.experimental.pallas import tpu as pltpu

D_MODEL = 1024
MLA_HEADS, MLA_Q_RANK, MLA_KV_RANK, MLA_NOPE, MLA_ROPE, MLA_V = 6, 192, 128, 64, 32, 64
DIFF_HEADS, DIFF_QK, DIFF_V = 4, 32, 64
SWA_HEADS, SWA_KV_HEADS, SWA_DIM, WINDOW = 6, 2, 64, 128
W_MLA, W_DIFF, W_SWA = MLA_HEADS * MLA_V, DIFF_HEADS * DIFF_V, SWA_HEADS * SWA_DIM
D_MIX = W_MLA + W_DIFF + W_SWA
IN_SIZES = (MLA_Q_RANK, MLA_KV_RANK, MLA_ROPE, DIFF_HEADS * 2 * DIFF_QK, DIFF_HEADS * 2 * DIFF_QK,
            W_DIFF, W_SWA, SWA_KV_HEADS * SWA_DIM, SWA_KV_HEADS * SWA_DIM, D_MIX)
REL_BUCKETS, REL_MAX_DIST = 32, 128
ROPE_THETA = 10000.0
RMS_EPS, LN_EPS = 1e-6, 1e-5

LOG2E = math.log2(math.e)
LANES = 128
ONES_ROWS = 16
VMEM_LIMIT = 56 * 1024 * 1024
OUT_CHUNK = 256

F32, BF16 = jnp.float32, jnp.bfloat16
NEG_INF = float("-inf")


def _cparams(sem):
    return pltpu.CompilerParams(dimension_semantics=sem, vmem_limit_bytes=VMEM_LIMIT)


def _nt(a, b):
    return lax.dot_general(a, b, (((1,), (1,)), ((), ())), preferred_element_type=F32)


def _nn(a, b):
    return jnp.dot(a, b, preferred_element_type=F32)


def _rope_tables_kernel(fcol_ref, frow_ref, cos_t_ref, sin_t_ref, cos_p_ref, sin_p_ref, *, tile):
    p0 = pl.program_id(0) * tile
    pos_l = (p0 + lax.broadcasted_iota(jnp.int32, (MLA_ROPE // 2, tile), 1)).astype(F32)
    ang_t = pos_l * fcol_ref[...]
    cos_t_ref[...] = jnp.cos(ang_t)
    sin_t_ref[...] = jnp.sin(ang_t)
    pos_s = (p0 + lax.broadcasted_iota(jnp.int32, (tile, LANES), 0)).astype(F32)
    ang_p = pos_s * frow_ref[...]
    lane = lax.broadcasted_iota(jnp.int32, (tile, LANES), 1)
    live = (lane >= MLA_NOPE) & (lane < MLA_NOPE + MLA_ROPE)
    cos_p_ref[...] = jnp.where(live, jnp.cos(ang_p), 0.0)
    sin_p_ref[...] = jnp.where(live, jnp.sin(ang_p), 0.0)


def _rope_tables(seq):
    half = MLA_ROPE // 2
    freqs = (ROPE_THETA ** (-np.arange(half, dtype=np.float64) / half)).astype(np.float32)
    fcol = jnp.asarray(freqs.reshape(half, 1))
    frow_np = np.zeros((1, LANES), np.float32)
    frow_np[0, MLA_NOPE:MLA_NOPE + half] = freqs
    frow_np[0, MLA_NOPE + half:MLA_NOPE + MLA_ROPE] = freqs
    frow = jnp.asarray(frow_np)
    tile = min(seq, 1024)
    return pl.pallas_call(
        functools.partial(_rope_tables_kernel, tile=tile),
        grid=(seq // tile,),
        in_specs=[pl.BlockSpec((half, 1), lambda i: (0, 0)), pl.BlockSpec((1, LANES), lambda i: (0, 0))],
        out_specs=[pl.BlockSpec((half, tile), lambda i: (0, i)), pl.BlockSpec((half, tile), lambda i: (0, i)),
                   pl.BlockSpec((tile, LANES), lambda i: (i, 0)), pl.BlockSpec((tile, LANES), lambda i: (i, 0))],
        out_shape=[jax.ShapeDtypeStruct((half, seq), F32), jax.ShapeDtypeStruct((half, seq), F32),
                   jax.ShapeDtypeStruct((seq, LANES), F32), jax.ShapeDtypeStruct((seq, LANES), F32)],
        compiler_params=_cparams(("arbitrary",)),
        name="rope_tables",
    )(fcol, frow)


def _t5_bucket(rel):
    half = REL_BUCKETS // 2
    max_exact = half // 2
    n = jnp.abs(rel)
    large = jnp.full(rel.shape, max_exact, jnp.int32)
    for k in range(1, half - max_exact):
        thr = max_exact * (REL_MAX_DIST / max_exact) ** (k / (half - max_exact))
        thr_i = int(round(thr)) if abs(thr - round(thr)) < 1e-9 else int(math.ceil(thr))
        large = large + (n >= thr_i).astype(jnp.int32)
    return jnp.where(rel > 0, half, 0) + jnp.where(n < max_exact, n, large)


def _bias_lookup(tab_ref, bucket, head):
    out = jnp.zeros(bucket.shape, F32)
    for b in range(REL_BUCKETS):
        out = jnp.where(bucket == b, tab_ref[b, head], out)
    return out


def _diff_bias_kernel(tab_ref, out_ref, *, tq, tk, e_min):
    h = pl.program_id(0)
    d = (pl.program_id(1) + e_min) * tk
    ki = lax.broadcasted_iota(jnp.int32, (tk, tq), 0)
    qj = lax.broadcasted_iota(jnp.int32, (tk, tq), 1)
    out_ref[0, 0] = _bias_lookup(tab_ref, _t5_bucket(d + ki - qj), h) * LOG2E


def _diff_bias_tiles(rel_bias, tq, tk, e_min, e_max):
    n_e = e_max - e_min + 1
    return pl.pallas_call(
        functools.partial(_diff_bias_kernel, tq=tq, tk=tk, e_min=e_min),
        grid=(DIFF_HEADS, n_e),
        in_specs=[pl.BlockSpec(memory_space=pltpu.SMEM)],
        out_specs=pl.BlockSpec((1, 1, tk, tq), lambda h, d: (h, d, 0, 0)),
        out_shape=jax.ShapeDtypeStruct((DIFF_HEADS, n_e, tk, tq), F32),
        compiler_params=_cparams(("arbitrary", "arbitrary")),
        name="diff_bias_tiles",
    )(rel_bias)


def _swa_bias_kernel(tab_ref, out_ref):
    variant = pl.program_id(0)
    h = pl.program_id(1)
    ki = lax.broadcasted_iota(jnp.int32, (3 * WINDOW, WINDOW), 0)
    qj = lax.broadcasted_iota(jnp.int32, (3 * WINDOW, WINDOW), 1)
    rel = ki - WINDOW - qj
    bias = _bias_lookup(tab_ref, _t5_bucket(rel), DIFF_HEADS + h) * LOG2E
    lo = jnp.where(variant == 1, WINDOW, 0)
    hi = jnp.where(variant == 2, 2 * WINDOW, 3 * WINDOW)
    live = (jnp.abs(rel) <= WINDOW) & (ki >= lo) & (ki < hi)
    out_ref[0, 0] = jnp.where(live, bias, NEG_INF)


def _swa_bias_tiles(rel_bias):
    return pl.pallas_call(
        _swa_bias_kernel,
        grid=(3, SWA_HEADS),
        in_specs=[pl.BlockSpec(memory_space=pltpu.SMEM)],
        out_specs=pl.BlockSpec((1, 1, 3 * WINDOW, WINDOW), lambda v, h: (v, h, 0, 0)),
        out_shape=jax.ShapeDtypeStruct((3, SWA_HEADS, 3 * WINDOW, WINDOW), F32),
        compiler_params=_cparams(("arbitrary", "arbitrary")),
        name="swa_bias_tiles",
    )(rel_bias)


_T_CQ, _T_CKV, _T_DQ, _T_DV, _T_SQ, _T_SV, _T_GATE = 0, 192, 320, 576, 832, 1216, 1344
_T_ROWS = 2368
_R_CKV, _R_KR, _R_KRROT, _R_DK, _R_SK = 0, 128, 256, 384, 896
_R_COLS = 1024


def _rms_t(v, g_col):
    ms = jnp.mean(v * v, axis=0, keepdims=True)
    return v * lax.rsqrt(ms + RMS_EPS) * g_col


N_IN_OPERANDS, N_IN_OUTPUTS = 13, 10


def _in_proj_body(xb, wt_ref, wr_ref, wuq_ref, wkp_ref, wvt_ref, gq_ref, gkv_col_ref, gkv_row_ref,
                  cos_t_ref, sin_t_ref, cos_p_ref, sin_p_ref, ones_ref,
                  qm_ref, km_ref, vm_ref, qd_ref, kd_ref, vd_ref, sq_ref, sk_ref, sv_ref, g_ref):
    def proj_t(lo, hi):
        return _nt(wt_ref[lo:hi, :], xb)

    def gate_rows(lo, hi):
        gate = proj_t(_T_GATE + lo, _T_GATE + hi)
        g_ref[0, lo:hi, :] = (gate * jax.nn.sigmoid(gate)).astype(BF16)

    cq_t = proj_t(_T_CQ, _T_CKV)
    ckv_t = proj_t(_T_CKV, _T_DQ)
    r = _nn(xb, wr_ref[...])
    gate_rows(0, D_MIX // 2)

    cqn = _rms_t(cq_t, gq_ref[...]).astype(BF16)
    q_all = _nn(wuq_ref[...], cqn)
    ckvn_t = _rms_t(ckv_t, gkv_col_ref[...]).astype(BF16)
    v_all = _nn(wvt_ref[...], ckvn_t).astype(BF16)
    ckv = r[:, _R_CKV:_R_KR]
    ms = jnp.mean(ckv * ckv, axis=1, keepdims=True)
    ckvn = (ckv * lax.rsqrt(ms + RMS_EPS) * gkv_row_ref[...]).astype(BF16)
    k_nope = _nn(ckvn, wkp_ref[...])
    gate_rows(D_MIX // 2, D_MIX)

    cos_t, sin_t = cos_t_ref[...], sin_t_ref[...]
    q_scale = LOG2E / math.sqrt(MLA_NOPE + MLA_ROPE)
    half = MLA_ROPE // 2
    for h in range(MLA_HEADS):
        blk = q_all[h * LANES:(h + 1) * LANES]
        x1 = blk[MLA_NOPE:MLA_NOPE + half]
        x2 = blk[MLA_NOPE + half:MLA_NOPE + MLA_ROPE]
        roped = jnp.concatenate([blk[:MLA_NOPE], x1 * cos_t - x2 * sin_t, x2 * cos_t + x1 * sin_t,
                                 blk[MLA_NOPE + MLA_ROPE:]], axis=0)
        qm_ref[0, h] = (roped * q_scale).astype(BF16)
    for h in range(MLA_HEADS):
        vm_ref[0, h] = v_all[h * MLA_V:(h + 1) * MLA_V]
    qd_ref[0] = (proj_t(_T_DQ, _T_DV) * (LOG2E / math.sqrt(DIFF_QK))).astype(BF16)

    k_rope = r[:, _R_KR:_R_KRROT] * cos_p_ref[...] + r[:, _R_KRROT:_R_DK] * sin_p_ref[...]
    for h in range(MLA_HEADS):
        km_ref[0, h] = (k_nope[:, h * LANES:(h + 1) * LANES] + k_rope).astype(BF16)
    ones_row = ones_ref[...]
    for h in range(DIFF_HEADS):
        kd_ref[0, h] = (r[:, _R_DK + h * LANES:_R_DK + (h + 1) * LANES] + ones_row).astype(BF16)
    sk_ref[0] = r[:, _R_SK:_R_COLS].astype(BF16)

    vd_ref[0] = proj_t(_T_DV, _T_SQ).astype(BF16)
    sq_ref[0] = (proj_t(_T_SQ, _T_SV) * (LOG2E / math.sqrt(SWA_DIM))).astype(BF16)
    sv_ref[0] = proj_t(_T_SV, _T_GATE).astype(BF16)


def _layer_spec(a, layer):
    return pl.BlockSpec((None,) + a.shape[1:], lambda *_: (layer,) + (0,) * (a.ndim - 1))


def _in_proj_operands(lw, layer, tabs, tm):
    cos_t, sin_t, cos_p, sin_p = tabs
    ones_np = np.zeros((1, LANES), np.float32)
    ones_np[0, 2 * DIFF_QK:2 * DIFF_QK + 2] = 1.0
    ones_row = jnp.asarray(ones_np)
    half = MLA_ROPE // 2
    weights = [lw["w_t"], lw["w_r"], lw["w_uq_t"], lw["w_kp"], lw["w_v_t"], lw["gq_col"], lw["gkv_col"],
               lw["gkv_row"]]
    ins = weights + [cos_t, sin_t, cos_p, sin_p, ones_row]
    in_specs = [_layer_spec(a, layer) for a in weights]
    in_specs += [pl.BlockSpec((half, tm), lambda bi, i: (0, i)), pl.BlockSpec((half, tm), lambda bi, i: (0, i)),
                 pl.BlockSpec((tm, LANES), lambda bi, i: (i, 0)), pl.BlockSpec((tm, LANES), lambda bi, i: (i, 0)),
                 pl.BlockSpec(ones_row.shape, lambda bi, i: (0, 0))]
    assert len(ins) == N_IN_OPERANDS
    return ins, in_specs


def _in_proj_outputs(b, s, tm):
    sds = jax.ShapeDtypeStruct
    out_shape = [
        sds((b, MLA_HEADS, LANES, s), BF16),
        sds((b, MLA_HEADS, s, LANES), BF16),
        sds((b, MLA_HEADS, MLA_V, s), BF16),
        sds((b, DIFF_HEADS * 2 * DIFF_QK, s), BF16),
        sds((b, DIFF_HEADS, s, LANES), BF16),
        sds((b, W_DIFF, s), BF16),
        sds((b, W_SWA, s), BF16),
        sds((b, s, LANES), BF16),
        sds((b, SWA_KV_HEADS * SWA_DIM, s), BF16),
        sds((b, D_MIX, s), BF16),
    ]
    out_specs = [
        pl.BlockSpec((1, MLA_HEADS, LANES, tm), lambda bi, i: (bi, 0, 0, i)),
        pl.BlockSpec((1, MLA_HEADS, tm, LANES), lambda bi, i: (bi, 0, i, 0)),
        pl.BlockSpec((1, MLA_HEADS, MLA_V, tm), lambda bi, i: (bi, 0, 0, i)),
        pl.BlockSpec((1, DIFF_HEADS * 2 * DIFF_QK, tm), lambda bi, i: (bi, 0, i)),
        pl.BlockSpec((1, DIFF_HEADS, tm, LANES), lambda bi, i: (bi, 0, i, 0)),
        pl.BlockSpec((1, W_DIFF, tm), lambda bi, i: (bi, 0, i)),
        pl.BlockSpec((1, W_SWA, tm), lambda bi, i: (bi, 0, i)),
        pl.BlockSpec((1, tm, LANES), lambda bi, i: (bi, i, 0)),
        pl.BlockSpec((1, SWA_KV_HEADS * SWA_DIM, tm), lambda bi, i: (bi, 0, i)),
        pl.BlockSpec((1, D_MIX, tm), lambda bi, i: (bi, 0, i)),
    ]
    assert len(out_shape) == N_IN_OUTPUTS
    return out_shape, out_specs


def _n_out_operands(tm):
    return 8 + 2 * (tm // WINDOW + 2) + 1


def _swa_chunk(sink_ref, q_ref, kb, vb, bias_ref, variant, lo, layer):
    sbs = range(lo // WINDOW, (lo + OUT_CHUNK) // WINDOW)
    k_win = {sb: jnp.concatenate(kb[sb:sb + 3], axis=0) for sb in sbs}
    v_win = {sb: jnp.concatenate(vb[sb:sb + 3], axis=1) for sb in sbs}
    grp = SWA_HEADS // SWA_KV_HEADS
    zeros = jnp.zeros((SWA_DIM, WINDOW), BF16)
    chains = [(hq, sb) for hq in range(SWA_HEADS) for sb in sbs]
    s_t, p_t, den, o_t = {}, {}, {}, {}
    for hq, sb in chains:
        q_t = q_ref[0, hq * SWA_DIM:(hq + 1) * SWA_DIM, sb * WINDOW:(sb + 1) * WINDOW]
        q_pad = jnp.concatenate([q_t, zeros] if hq // grp == 0 else [zeros, q_t], axis=0)
        s_t[hq, sb] = _nn(k_win[sb], q_pad) + bias_ref[variant[sb], hq]
    for hq, sb in chains:
        sink = sink_ref[layer, hq] * LOG2E
        m = jnp.maximum(jnp.max(s_t[hq, sb], axis=0, keepdims=True), sink)
        p = jnp.exp2(s_t[hq, sb] - m)
        den[hq, sb] = jnp.sum(p, axis=0, keepdims=True) + jnp.exp2(sink - m)
        p_t[hq, sb] = p.astype(BF16)
    for hq, sb in chains:
        g = hq // grp
        o_t[hq, sb] = (_nn(v_win[sb][g * SWA_DIM:(g + 1) * SWA_DIM], p_t[hq, sb]) / den[hq, sb]).astype(BF16)
    return jnp.concatenate([jnp.concatenate([o_t[hq, sb] for sb in sbs], axis=1) for hq in range(SWA_HEADS)],
                           axis=0)


def _out_proj_chunks(x_ref, om_ref, od_ref, g_ref, w_ref, lng_ref, lnb_ref, sink_ref, sq_ref, *swa_refs,
                     alpha, layer, n_tiles):
    tm = x_ref.shape[1]
    n_kb = tm // WINDOW + 2
    kb = [r[0] for r in swa_refs[:n_kb]]
    vb = [r[0] for r in swa_refs[n_kb:2 * n_kb]]
    bias_ref = swa_refs[2 * n_kb]
    tile = pl.program_id(1)
    variant = [0] * (tm // WINDOW)
    variant[0] = jnp.where(tile == 0, 1, 0)
    variant[-1] = jnp.where(tile == n_tiles - 1, 2, 0)
    for lo in range(0, tm, OUT_CHUNK):
        tok = slice(lo, lo + OUT_CHUNK)
        os_t = _swa_chunk(sink_ref, sq_ref, kb, vb, bias_ref, variant, lo, layer)
        o_t = jnp.concatenate([om_ref[0, :, tok], od_ref[0, :, tok], os_t], axis=0)
        og_t = o_t * g_ref[0, :, tok]
        y = lax.dot_general(og_t, w_ref[...], (((0,), (0,)), ((), ())), preferred_element_type=F32)
        z = alpha * x_ref[0, tok, :] + y
        mu = jnp.mean(z, axis=1, keepdims=True)
        zc = z - mu
        var = jnp.mean(zc * zc, axis=1, keepdims=True)
        yield tok, zc * lax.rsqrt(var + LN_EPS) * lng_ref[...] + lnb_ref[...]


def _mix_kernel(x_ref, *refs, out_layer, has_in, alpha, n_tiles):
    has_out = out_layer is not None
    n_out = _n_out_operands(x_ref.shape[1]) if has_out else 0
    n_in = n_out + (N_IN_OPERANDS if has_in else 0)
    ins, outs = refs[:n_in], refs[n_in:]
    if has_out:
        out_ops, ins = ins[:n_out], ins[n_out:]
        xo_ref, outs = outs[0], outs[1:]
        rows = []
        for tok, x_new in _out_proj_chunks(x_ref, *out_ops, alpha=alpha, layer=out_layer, n_tiles=n_tiles):
            xo_ref[0, tok, :] = x_new
            rows.append(x_new.astype(BF16))
        xb = jnp.concatenate(rows, axis=0)
    else:
        xb = x_ref[0].astype(BF16)
    if has_in:
        _in_proj_body(xb, *ins, *outs)


def _mix(x, prev, nxt, tabs, tm, alpha):
    b, s, d = x.shape
    ins, in_specs = [x], [pl.BlockSpec((1, tm, d), lambda bi, i: (bi, i, 0))]
    out_shape, out_specs = [], []
    out_layer = None
    if prev is not None:
        om, od, g, sq, sk, sv, swa_bias, swa_sink, w_out, ln_g, ln_b, out_layer = prev
        assert tm % OUT_CHUNK == 0 and tm >= 2 * OUT_CHUNK and OUT_CHUNK % WINDOW == 0
        tspec = lambda rows: pl.BlockSpec((1, rows, tm), lambda bi, i: (bi, 0, i))
        r, nblk = tm // WINDOW, s // WINDOW
        n_kb = r + 2
        kspec = lambda c: pl.BlockSpec((1, WINDOW, LANES), lambda bi, i: (bi, jnp.clip(i * r - 1 + c, 0, nblk - 1), 0))
        vspec = lambda c: pl.BlockSpec((1, SWA_KV_HEADS * SWA_DIM, WINDOW),
                                       lambda bi, i: (bi, 0, jnp.clip(i * r - 1 + c, 0, nblk - 1)))
        ins += [om, od, g, w_out, ln_g, ln_b, swa_sink, sq] + [sk] * n_kb + [sv] * n_kb + [swa_bias]
        in_specs += [tspec(W_MLA), tspec(W_DIFF), tspec(D_MIX),
                     _layer_spec(w_out, out_layer), _layer_spec(ln_g, out_layer), _layer_spec(ln_b, out_layer),
                     pl.BlockSpec(memory_space=pltpu.SMEM), tspec(W_SWA)]
        in_specs += [kspec(c) for c in range(n_kb)] + [vspec(c) for c in range(n_kb)]
        in_specs += [pl.BlockSpec(swa_bias.shape, lambda bi, i: (0, 0, 0, 0))]
        assert len(ins) - 1 == _n_out_operands(tm)
        out_shape.append(jax.ShapeDtypeStruct((b, s, d), F32))
        out_specs.append(pl.BlockSpec((1, tm, d), lambda bi, i: (bi, i, 0)))
    if nxt is not None:
        lw, layer = nxt
        more_ins, more_specs = _in_proj_operands(lw, layer, tabs, tm)
        ins, in_specs = ins + more_ins, in_specs + more_specs
        shapes, specs = _in_proj_outputs(b, s, tm)
        out_shape, out_specs = out_shape + shapes, out_specs + specs
    return pl.pallas_call(
        functools.partial(_mix_kernel, out_layer=out_layer, has_in=nxt is not None, alpha=alpha, n_tiles=s // tm),
        grid=(b, s // tm),
        in_specs=in_specs,
        out_specs=out_specs,
        out_shape=out_shape,
        compiler_params=_cparams(("parallel", "parallel")),
        name="mix_" + ("out" if prev is not None else "") + ("in" if nxt is not None else ""),
    )(*ins)


def _with_ones(v_t):
    tk = v_t.shape[1]
    row = lax.broadcasted_iota(jnp.int32, (ONES_ROWS, tk), 0)
    return jnp.concatenate([v_t, jnp.where(row == 0, 1.0, 0.0).astype(v_t.dtype)], axis=0)


def _online_step(s_t, tile_max, v_ext, m, acc):
    m_new = jnp.maximum(m, tile_max)
    p = jnp.exp2(s_t - m_new).astype(BF16)
    alpha = jnp.exp2(m - m_new)
    return m_new, alpha * acc + _nn(v_ext, p)


def _tile_off(j, t):
    return j * t if isinstance(j, int) else pl.multiple_of(j * t, t)


PIPE_TILES = 2
COL_BLOCK = 256
MAX_UNROLLED_TILES = 64


def _pipelined_tiles(n_tiles, n_sub, produce_ctx, produce, consume_ctx, consume, carry):
    u_n = PIPE_TILES
    assert n_tiles % u_n == 0 and n_tiles <= MAX_UNROLLED_TILES
    carry = list(carry)
    for u in range(u_n):
        ctx = produce_ctx(u)
        for c in range(n_sub):
            produce(0, u, c, ctx)
    for half in range(n_tiles // u_n):
        c_slot, v_c = half % 2, half * u_n
        v_p = v_c + u_n
        for u in range(u_n):
            pctx = produce_ctx(v_p + u) if v_p < n_tiles else None
            cctx = consume_ctx(v_c + u)
            for c in range(n_sub):
                carry[c] = consume(c_slot, u, c, cctx, carry[c])
                if pctx is not None:
                    produce(1 - c_slot, u, c, pctx)
    return tuple(carry)


def _score_scratch(n_sub, tk):
    n = 2 * PIPE_TILES * n_sub
    return [pltpu.VMEM((1, tk, COL_BLOCK), F32)] * n + [pltpu.VMEM((1, 1, COL_BLOCK), F32)] * n


def _score_refs(bufs, n_sub):
    n = 2 * PIPE_TILES * n_sub
    z = jnp.minimum(pl.program_id(0), 0)
    at = lambda slot, u, c: (slot * PIPE_TILES + u) * n_sub + c
    return (lambda slot, u, c: bufs[at(slot, u, c)].at[z]), (lambda slot, u, c: bufs[n + at(slot, u, c)].at[z])


def _mla_attn_kernel(q_ref, k_ref, v_ref, o_ref, *bufs, tq, tk, nk):
    n_sub = tq // COL_BLOCK
    s_buf, mx_buf = _score_refs(bufs, n_sub)

    def query_tile(qt, _):
        cols = lambda c: pl.ds(pl.multiple_of(qt * tq + c * COL_BLOCK, COL_BLOCK), COL_BLOCK)
        q_blk = [q_ref[0, 0, :, cols(c)] for c in range(n_sub)]

        def produce_ctx(j):
            return k_ref[0, 0, pl.ds(_tile_off(j, tk), tk), :]

        def produce(slot, u, c, k):
            s_t = _nn(k, q_blk[c])
            s_buf(slot, u, c)[...] = s_t
            mx_buf(slot, u, c)[...] = jnp.max(s_t, axis=0, keepdims=True)

        def consume_ctx(j):
            return _with_ones(v_ref[0, 0, :, pl.ds(_tile_off(j, tk), tk)])

        def consume(slot, u, c, v_ext, carry):
            return _online_step(s_buf(slot, u, c)[...], mx_buf(slot, u, c)[...], v_ext, *carry)

        m0 = jnp.full((1, COL_BLOCK), NEG_INF, F32)
        acc0 = jnp.zeros((MLA_V + ONES_ROWS, COL_BLOCK), F32)
        carry = _pipelined_tiles(nk, n_sub, produce_ctx, produce, consume_ctx, consume, ((m0, acc0),) * n_sub)
        for c, (_, acc) in enumerate(carry):
            o_ref[0, :, cols(c)] = (acc[:MLA_V] / acc[MLA_V:MLA_V + 1]).astype(o_ref.dtype)
        return 0

    lax.fori_loop(0, q_ref.shape[3] // tq, query_tile, 0)


def _mla_attention(qm, km, vm, tq, tk):
    b, h, _, s = qm.shape
    return pl.pallas_call(
        functools.partial(_mla_attn_kernel, tq=tq, tk=tk, nk=s // tk),
        grid=(b, h),
        in_specs=[pl.BlockSpec((1, 1, LANES, s), lambda bi, hi: (bi, hi, 0, 0)),
                  pl.BlockSpec((1, 1, s, LANES), lambda bi, hi: (bi, hi, 0, 0)),
                  pl.BlockSpec((1, 1, MLA_V, s), lambda bi, hi: (bi, hi, 0, 0))],
        out_specs=pl.BlockSpec((1, MLA_V, s), lambda bi, hi: (bi, hi, 0)),
        out_shape=jax.ShapeDtypeStruct((b, W_MLA, s), BF16),
        scratch_shapes=_score_scratch(tq // COL_BLOCK, tk),
        compiler_params=_cparams(("parallel", "parallel")),
        name="mla_attention",
    )(qm, km, vm)


NEAR_TILES = 4


def _diff_near_offsets(r):
    return min(-1, r - NEAR_TILES), NEAR_TILES - 1


def _diff_attn_kernel(far_ref, lam_ref, q_ref, k_ref, v_ref, bias_ref, subln_ref, o_ref,
                      qfar_buf, *bufs, tq, tk, nk, layer, lam_init):
    s_buf, mx_buf = _score_refs(bufs, 2 * (tq // COL_BLOCK))
    h = pl.program_id(1)
    qi = pl.program_id(2)
    t = tq
    r = tq // tk
    q_t = q_ref[0]
    r16 = lax.broadcasted_iota(jnp.int32, (ONES_ROWS, t), 0)
    zero_map = jnp.zeros((DIFF_QK, t), BF16)
    zero_pad = jnp.zeros((LANES - 2 * DIFF_QK - ONES_ROWS, t), BF16)

    def q_aug(mp, side):
        own = q_t[mp * DIFF_QK:(mp + 1) * DIFF_QK]
        own = [own, zero_map] if mp == 0 else [zero_map, own]
        if side is None:
            aug = jnp.zeros((ONES_ROWS, t), BF16)
        else:
            c = jnp.full((ONES_ROWS, t), far_ref[side, h], F32)
            c_hi = c.astype(BF16).astype(F32)
            aug = jnp.where(r16 == 0, c_hi, jnp.where(r16 == 1, c - c_hi, 0.0)).astype(BF16)
        return jnp.concatenate(own + [aug, zero_pad], axis=0)

    near_lo = jnp.clip(r * qi - 1, 0, nk - NEAR_TILES)
    e_min, _ = _diff_near_offsets(r)
    q_near = [q_aug(0, None), q_aug(1, None)]
    for side in range(2):
        for mp in range(2):
            qfar_buf[side, mp] = q_aug(mp, side)

    def key_tile(v):
        if v < NEAR_TILES:
            return near_lo + v, True, None
        idx = v - NEAR_TILES
        side = (idx >= near_lo).astype(jnp.int32)
        return idx + NEAR_TILES * side, False, side

    n_cb = tq // COL_BLOCK
    cols = lambda cb: slice(cb * COL_BLOCK, (cb + 1) * COL_BLOCK)

    def produce_ctx(v):
        j, near, side = key_tile(v)
        return k_ref[0, 0, pl.ds(_tile_off(j, tk), tk), :], near, side, j

    def produce(slot, u, c, ctx):
        k, near, side, j = ctx
        mp, cb = divmod(c, n_cb)
        if near:
            s_t = _nn(k, q_near[mp][:, cols(cb)]) + bias_ref[0, j - r * qi - e_min, :, cols(cb)]
        else:
            s_t = _nn(k, qfar_buf[side, mp, :, cols(cb)])
        s_buf(slot, u, c)[...] = s_t
        mx_buf(slot, u, c)[...] = jnp.max(s_t, axis=0, keepdims=True)

    def consume_ctx(v):
        j, _, _ = key_tile(v)
        return _with_ones(v_ref[0, :, pl.ds(_tile_off(j, tk), tk)])

    def consume(slot, u, c, v_ext, carry):
        return _online_step(s_buf(slot, u, c)[...], mx_buf(slot, u, c)[...], v_ext, *carry)

    m0 = jnp.full((1, COL_BLOCK), NEG_INF, F32)
    acc0 = jnp.zeros((DIFF_V + ONES_ROWS, COL_BLOCK), F32)
    carry = _pipelined_tiles(nk, 2 * n_cb, produce_ctx, produce, consume_ctx, consume,
                             ((m0, acc0),) * (2 * n_cb))

    lam = lam_ref[layer, 0]
    for cb in range(n_cb):
        a0, a1 = carry[cb][1], carry[n_cb + cb][1]
        o = a0[:DIFF_V] / a0[DIFF_V:DIFF_V + 1] - lam * (a1[:DIFF_V] / a1[DIFF_V:DIFF_V + 1])
        ms = jnp.mean(o * o, axis=0, keepdims=True)
        o = o * lax.rsqrt(ms + RMS_EPS) * subln_ref[...] * (1.0 - lam_init)
        o_ref[0, :, cols(cb)] = o.astype(o_ref.dtype)


def _diff_attention(qd, kd, vd, bias_tiles, far, lam, subln_col, layer, tq, tk, lam_init):
    b, _, s = qd.shape
    n_e = bias_tiles.shape[1]
    assert tq % tk == 0 and tq // tk + 2 <= NEAR_TILES and tk >= REL_MAX_DIST
    smem = pl.BlockSpec(memory_space=pltpu.SMEM)
    return pl.pallas_call(
        functools.partial(_diff_attn_kernel, tq=tq, tk=tk, nk=s // tk, layer=layer, lam_init=lam_init),
        grid=(b, DIFF_HEADS, s // tq),
        in_specs=[smem, smem,
                  pl.BlockSpec((1, 2 * DIFF_QK, tq), lambda bi, hi, qi: (bi, hi, qi)),
                  pl.BlockSpec((1, 1, s, LANES), lambda bi, hi, qi: (bi, hi, 0, 0)),
                  pl.BlockSpec((1, DIFF_V, s), lambda bi, hi, qi: (bi, hi, 0)),
                  pl.BlockSpec((1, n_e, tk, tq), lambda bi, hi, qi: (hi, 0, 0, 0)),
                  _layer_spec(subln_col, layer)],
        out_specs=pl.BlockSpec((1, DIFF_V, tq), lambda bi, hi, qi: (bi, hi, qi)),
        out_shape=jax.ShapeDtypeStruct((b, W_DIFF, s), BF16),
        scratch_shapes=[pltpu.VMEM((2, 2, LANES, tq), BF16)] + _score_scratch(2 * (tq // COL_BLOCK), tk),
        compiler_params=_cparams(("parallel", "parallel", "arbitrary")),
        name="diff_attention",
    )(far, lam, qd, kd, vd, bias_tiles, subln_col)


def _layer_weights(w_in, q_norm, kv_norm, w_uq, w_ukv):
    cq, ckv, kr, dq, dk, dv, sq, sk, sv, gate = jnp.split(w_in, np.cumsum(IN_SIZES)[:-1].tolist(), axis=1)
    d = w_in.shape[0]
    half = MLA_ROPE // 2
    w_t = jnp.concatenate([cq, ckv, dq, dv, sq, sv, gate], axis=1).T.astype(BF16)
    pad = lambda a, lo, width: jnp.pad(a, ((0, 0), (lo, width - lo - a.shape[1])))
    kr_p = pad(kr, MLA_NOPE, LANES)
    kr_rot_p = pad(jnp.concatenate([-kr[:, half:], kr[:, :half]], axis=1), MLA_NOPE, LANES)
    dk_p = jnp.concatenate([pad(dk[:, h * 2 * DIFF_QK:(h + 1) * 2 * DIFF_QK], 0, LANES)
                            for h in range(DIFF_HEADS)], axis=1)
    w_r = jnp.concatenate([ckv, kr_p, kr_rot_p, dk_p, sk], axis=1).astype(BF16)
    assert w_t.shape == (_T_ROWS, d) and w_r.shape == (d, _R_COLS)
    qk = MLA_NOPE + MLA_ROPE
    w_uq_t = jnp.concatenate([jnp.pad(w_uq[:, h * qk:(h + 1) * qk].T, ((0, LANES - qk), (0, 0)))
                              for h in range(MLA_HEADS)], axis=0).astype(BF16)
    kv = MLA_NOPE + MLA_V
    w_kp = jnp.concatenate([pad(w_ukv[:, h * kv:h * kv + MLA_NOPE], 0, LANES)
                            for h in range(MLA_HEADS)], axis=1).astype(BF16)
    w_v_t = jnp.concatenate([w_ukv[:, h * kv + MLA_NOPE:(h + 1) * kv].T
                             for h in range(MLA_HEADS)], axis=0).astype(BF16)
    return dict(w_t=w_t, w_r=w_r, w_uq_t=w_uq_t, w_kp=w_kp, w_v_t=w_v_t,
                gq_col=q_norm.reshape(-1, 1), gkv_col=kv_norm.reshape(-1, 1), gkv_row=kv_norm.reshape(1, -1))


def _lambda_kernel(q1_ref, k1_ref, q2_ref, k2_ref, init_ref, o_ref):
    a = jnp.sum(q1_ref[...] * k1_ref[...], axis=1, keepdims=True)
    c = jnp.sum(q2_ref[...] * k2_ref[...], axis=1, keepdims=True)
    o_ref[...] = jnp.exp(a) - jnp.exp(c) + init_ref[...]


def _diff_lambda(lam_vecs, lam_inits):
    depth = lam_vecs.shape[0]
    return pl.pallas_call(
        _lambda_kernel,
        out_shape=jax.ShapeDtypeStruct((depth, 1), F32),
        name="diff_lambda",
    )(*(lam_vecs[:, i, :] for i in range(4)), jnp.asarray(np.asarray(lam_inits, np.float32).reshape(depth, 1)))


def _tiles(seq):
    return dict(tm=min(512, seq), mla_tq=min(1024, seq), mla_tk=256, diff_tq=min(512, seq), diff_tk=256)


def kernel(x, w_in, mla_q_norm, mla_kv_norm, mla_w_uq, mla_w_ukv, diff_lambda, diff_subln, swa_sink,
           rel_bias, w_out, ln_g, ln_b):
    depth = w_in.shape[0]
    seq = x.shape[1]
    cfg = _tiles(seq)
    alpha = (2 * depth) ** 0.25
    tabs = _rope_tables(seq)
    e_min, e_max = _diff_near_offsets(cfg["diff_tq"] // cfg["diff_tk"])
    diff_bias = _diff_bias_tiles(rel_bias, cfg["diff_tq"], cfg["diff_tk"], e_min, e_max)
    swa_bias = _swa_bias_tiles(rel_bias)
    half = REL_BUCKETS // 2
    far = jnp.stack([rel_bias[half - 1, :DIFF_HEADS], rel_bias[2 * half - 1, :DIFF_HEADS]]) * LOG2E
    lam_inits = [0.8 - 0.6 * math.exp(-0.3 * l) for l in range(depth)]
    lam = _diff_lambda(diff_lambda, lam_inits)
    lw = jax.vmap(_layer_weights)(w_in, mla_q_norm, mla_kv_norm, mla_w_uq, mla_w_ukv)
    subln_col = diff_subln[:, :, None]
    w_out_b, ln_g3, ln_b3 = w_out.astype(BF16), ln_g[:, None, :], ln_b[:, None, :]
    prev = None
    for l in range(depth + 1):
        nxt = (lw, l) if l < depth else None
        outs = _mix(x, prev, nxt, tabs, cfg["tm"], alpha)
        if prev is not None:
            x, outs = outs[0], outs[1:]
        if nxt is None:
            return x
        qm, km, vm, qd, kd, vd, sq, sk, sv, g = outs
        om = _mla_attention(qm, km, vm, cfg["mla_tq"], cfg["mla_tk"])
        od = _diff_attention(qd, kd, vd, diff_bias, far, lam, subln_col, l, cfg["diff_tq"], cfg["diff_tk"],
                             lam_inits[l])
        prev = (om, od, g, sq, sk, sv, swa_bias, swa_sink, w_out_b, ln_g3, ln_b3, l)
```

```python
import functools
import math

import numpy as np
import jax
import jax.numpy as jnp
from jax import lax
from jax.experimental import pallas as pl
from jax.experimental.pallas import tpu as pltpu

D_MODEL = 1024
MLA_HEADS, MLA_Q_RANK, MLA_KV_RANK, MLA_NOPE, MLA_ROPE, MLA_V = 6, 192, 128, 64, 32, 64
DIFF_HEADS, DIFF_QK, DIFF_V = 4, 32, 64
SWA_HEADS, SWA_KV_HEADS, SWA_DIM, WINDOW = 6, 2, 64, 128
W_MLA, W_DIFF, W_SWA = MLA_HEADS * MLA_V, DIFF_HEADS * DIFF_V, SWA_HEADS * SWA_DIM
D_MIX = W_MLA + W_DIFF + W_SWA
IN_SIZES = (MLA_Q_RANK, MLA_KV_RANK, MLA_ROPE, DIFF_HEADS * 2 * DIFF_QK, DIFF_HEADS * 2 * DIFF_QK,
            W_DIFF, W_SWA, SWA_KV_HEADS * SWA_DIM, SWA_KV_HEADS * SWA_DIM, D_MIX)
REL_BUCKETS, REL_MAX_DIST = 32, 128
ROPE_THETA = 10000.0
RMS_EPS, LN_EPS = 1e-6, 1e-5

LOG2E = math.log2(math.e)
LANES = 128
ONES_ROWS = 16
VMEM_LIMIT = 56 * 1024 * 1024
OUT_CHUNK = 256

F32, BF16 = jnp.float32, jnp.bfloat16
NEG_INF = float("-inf")


def _cparams(sem):
    return pltpu.CompilerParams(dimension_semantics=sem, vmem_limit_bytes=VMEM_LIMIT)


def _nt(a, b):
    return lax.dot_general(a, b, (((1,), (1,)), ((), ())), preferred_element_type=F32)


def _nn(a, b):
    return jnp.dot(a, b, preferred_element_type=F32)


def _rope_tables_kernel(fcol_ref, frow_ref, cos_t_ref, sin_t_ref, cos_p_ref, sin_p_ref, *, tile):
    p0 = pl.program_id(0) * tile
    pos_l = (p0 + lax.broadcasted_iota(jnp.int32, (MLA_ROPE // 2, tile), 1)).astype(F32)
    ang_t = pos_l * fcol_ref[...]
    cos_t_ref[...] = jnp.cos(ang_t)
    sin_t_ref[...] = jnp.sin(ang_t)
    pos_s = (p0 + lax.broadcasted_iota(jnp.int32, (tile, LANES), 0)).astype(F32)
    ang_p = pos_s * frow_ref[...]
    lane = lax.broadcasted_iota(jnp.int32, (tile, LANES), 1)
    live = (lane >= MLA_NOPE) & (lane < MLA_NOPE + MLA_ROPE)
    cos_p_ref[...] = jnp.where(live, jnp.cos(ang_p), 0.0)
    sin_p_ref[...] = jnp.where(live, jnp.sin(ang_p), 0.0)


def _rope_tables(seq):
    half = MLA_ROPE // 2
    freqs = (ROPE_THETA ** (-np.arange(half, dtype=np.float64) / half)).astype(np.float32)
    fcol = jnp.asarray(freqs.reshape(half, 1))
    frow_np = np.zeros((1, LANES), np.float32)
    frow_np[0, MLA_NOPE:MLA_NOPE + half] = freqs
    frow_np[0, MLA_NOPE + half:MLA_NOPE + MLA_ROPE] = freqs
    frow = jnp.asarray(frow_np)
    tile = min(seq, 1024)
    return pl.pallas_call(
        functools.partial(_rope_tables_kernel, tile=tile),
        grid=(seq // tile,),
        in_specs=[pl.BlockSpec((half, 1), lambda i: (0, 0)), pl.BlockSpec((1, LANES), lambda i: (0, 0))],
        out_specs=[pl.BlockSpec((half, tile), lambda i: (0, i)), pl.BlockSpec((half, tile), lambda i: (0, i)),
                   pl.BlockSpec((tile, LANES), lambda i: (i, 0)), pl.BlockSpec((tile, LANES), lambda i: (i, 0))],
        out_shape=[jax.ShapeDtypeStruct((half, seq), F32), jax.ShapeDtypeStruct((half, seq), F32),
                   jax.ShapeDtypeStruct((seq, LANES), F32), jax.ShapeDtypeStruct((seq, LANES), F32)],
        compiler_params=_cparams(("arbitrary",)),
        name="rope_tables",
    )(fcol, frow)


def _t5_bucket(rel):
    half = REL_BUCKETS // 2
    max_exact = half // 2
    n = jnp.abs(rel)
    large = jnp.full(rel.shape, max_exact, jnp.int32)
    for k in range(1, half - max_exact):
        thr = max_exact * (REL_MAX_DIST / max_exact) ** (k / (half - max_exact))
        thr_i = int(round(thr)) if abs(thr - round(thr)) < 1e-9 else int(math.ceil(thr))
        large = large + (n >= thr_i).astype(jnp.int32)
    return jnp.where(rel > 0, half, 0) + jnp.where(n < max_exact, n, large)


def _bias_lookup(tab_ref, bucket, head):
    out = jnp.zeros(bucket.shape, F32)
    for b in range(REL_BUCKETS):
        out = jnp.where(bucket == b, tab_ref[b, head], out)
    return out


def _diff_bias_kernel(tab_ref, out_ref, *, tq, tk, e_min):
    h = pl.program_id(0)
    d = (pl.program_id(1) + e_min) * tk
    ki = lax.broadcasted_iota(jnp.int32, (tk, tq), 0)
    qj = lax.broadcasted_iota(jnp.int32, (tk, tq), 1)
    out_ref[0, 0] = _bias_lookup(tab_ref, _t5_bucket(d + ki - qj), h) * LOG2E


def _diff_bias_tiles(rel_bias, tq, tk, e_min, e_max):
    n_e = e_max - e_min + 1
    return pl.pallas_call(
        functools.partial(_diff_bias_kernel, tq=tq, tk=tk, e_min=e_min),
        grid=(DIFF_HEADS, n_e),
        in_specs=[pl.BlockSpec(memory_space=pltpu.SMEM)],
        out_specs=pl.BlockSpec((1, 1, tk, tq), lambda h, d: (h, d, 0, 0)),
        out_shape=jax.ShapeDtypeStruct((DIFF_HEADS, n_e, tk, tq), F32),
        compiler_params=_cparams(("arbitrary", "arbitrary")),
        name="diff_bias_tiles",
    )(rel_bias)


def _swa_bias_kernel(tab_ref, out_ref):
    variant = pl.program_id(0)
    h = pl.program_id(1)
    ki = lax.broadcasted_iota(jnp.int32, (3 * WINDOW, WINDOW), 0)
    qj = lax.broadcasted_iota(jnp.int32, (3 * WINDOW, WINDOW), 1)
    rel = ki - WINDOW - qj
    bias = _bias_lookup(tab_ref, _t5_bucket(rel), DIFF_HEADS + h) * LOG2E
    lo = jnp.where(variant == 1, WINDOW, 0)
    hi = jnp.where(variant == 2, 2 * WINDOW, 3 * WINDOW)
    live = (jnp.abs(rel) <= WINDOW) & (ki >= lo) & (ki < hi)
    out_ref[0, 0] = jnp.where(live, bias, NEG_INF)


def _swa_bias_tiles(rel_bias):
    return pl.pallas_call(
        _swa_bias_kernel,
        grid=(3, SWA_HEADS),
        in_specs=[pl.BlockSpec(memory_space=pltpu.SMEM)],
        out_specs=pl.BlockSpec((1, 1, 3 * WINDOW, WINDOW), lambda v, h: (v, h, 0, 0)),
        out_shape=jax.ShapeDtypeStruct((3, SWA_HEADS, 3 * WINDOW, WINDOW), F32),
        compiler_params=_cparams(("arbitrary", "arbitrary")),
        name="swa_bias_tiles",
    )(rel_bias)


_T_CQ, _T_CKV, _T_DQ, _T_DV, _T_SQ, _T_SV, _T_GATE = 0, 192, 320, 576, 832, 1216, 1344
_T_ROWS = 2368
_R_CKV, _R_KR, _R_KRROT, _R_DK, _R_SK = 0, 128, 256, 384, 896
_R_COLS = 1024


def _rms_t(v, g_col):
    ms = jnp.mean(v * v, axis=0, keepdims=True)
    return v * lax.rsqrt(ms + RMS_EPS) * g_col


N_IN_OPERANDS, N_IN_OUTPUTS = 13, 10


def _in_proj_body(xb, wt_ref, wr_ref, wuq_ref, wkp_ref, wvt_ref, gq_ref, gkv_col_ref, gkv_row_ref,
                  cos_t_ref, sin_t_ref, cos_p_ref, sin_p_ref, ones_ref,
                  qm_ref, km_ref, vm_ref, qd_ref, kd_ref, vd_ref, sq_ref, sk_ref, sv_ref, g_ref):
    def proj_t(lo, hi):
        return _nt(wt_ref[lo:hi, :], xb)

    def gate_rows(lo, hi):
        gate = proj_t(_T_GATE + lo, _T_GATE + hi)
        g_ref[0, lo:hi, :] = (gate * jax.nn.sigmoid(gate)).astype(BF16)

    cq_t = proj_t(_T_CQ, _T_CKV)
    ckv_t = proj_t(_T_CKV, _T_DQ)
    r = _nn(xb, wr_ref[...])
    gate_rows(0, D_MIX // 2)

    cqn = _rms_t(cq_t, gq_ref[...]).astype(BF16)
    q_all = _nn(wuq_ref[...], cqn)
    ckvn_t = _rms_t(ckv_t, gkv_col_ref[...]).astype(BF16)
    v_all = _nn(wvt_ref[...], ckvn_t).astype(BF16)
    ckv = r[:, _R_CKV:_R_KR]
    ms = jnp.mean(ckv * ckv, axis=1, keepdims=True)
    ckvn = (ckv * lax.rsqrt(ms + RMS_EPS) * gkv_row_ref[...]).astype(BF16)
    k_nope = _nn(ckvn, wkp_ref[...])
    gate_rows(D_MIX // 2, D_MIX)

    cos_t, sin_t = cos_t_ref[...], sin_t_ref[...]
    q_scale = LOG2E / math.sqrt(MLA_NOPE + MLA_ROPE)
    half = MLA_ROPE // 2
    for h in range(MLA_HEADS):
        blk = q_all[h * LANES:(h + 1) * LANES]
        x1 = blk[MLA_NOPE:MLA_NOPE + half]
        x2 = blk[MLA_NOPE + half:MLA_NOPE + MLA_ROPE]
        roped = jnp.concatenate([blk[:MLA_NOPE], x1 * cos_t - x2 * sin_t, x2 * cos_t + x1 * sin_t,
                                 blk[MLA_NOPE + MLA_ROPE:]], axis=0)
        qm_ref[0, h] = (roped * q_scale).astype(BF16)
    for h in range(MLA_HEADS):
        vm_ref[0, h] = v_all[h * MLA_V:(h + 1) * MLA_V]
    qd_ref[0] = (proj_t(_T_DQ, _T_DV) * (LOG2E / math.sqrt(DIFF_QK))).astype(BF16)

    k_rope = r[:, _R_KR:_R_KRROT] * cos_p_ref[...] + r[:, _R_KRROT:_R_DK] * sin_p_ref[...]
    for h in range(MLA_HEADS):
        km_ref[0, h] = (k_nope[:, h * LANES:(h + 1) * LANES] + k_rope).astype(BF16)
    ones_row = ones_ref[...]
    for h in range(DIFF_HEADS):
        kd_ref[0, h] = (r[:, _R_DK + h * LANES:_R_DK + (h + 1) * LANES] + ones_row).astype(BF16)
    sk_ref[0] = r[:, _R_SK:_R_COLS].astype(BF16)

    vd_ref[0] = proj_t(_T_DV, _T_SQ).astype(BF16)
    sq_ref[0] = (proj_t(_T_SQ, _T_SV) * (LOG2E / math.sqrt(SWA_DIM))).astype(BF16)
    sv_ref[0] = proj_t(_T_SV, _T_GATE).astype(BF16)


def _layer_spec(a, layer):
    return pl.BlockSpec((None,) + a.shape[1:], lambda *_: (layer,) + (0,) * (a.ndim - 1))


def _in_proj_operands(lw, layer, tabs, tm):
    cos_t, sin_t, cos_p, sin_p = tabs
    ones_np = np.zeros((1, LANES), np.float32)
    ones_np[0, 2 * DIFF_QK:2 * DIFF_QK + 2] = 1.0
    ones_row = jnp.asarray(ones_np)
    half = MLA_ROPE // 2
    weights = [lw["w_t"], lw["w_r"], lw["w_uq_t"], lw["w_kp"], lw["w_v_t"], lw["gq_col"], lw["gkv_col"],
               lw["gkv_row"]]
    ins = weights + [cos_t, sin_t, cos_p, sin_p, ones_row]
    in_specs = [_layer_spec(a, layer) for a in weights]
    in_specs += [pl.BlockSpec((half, tm), lambda bi, i: (0, i)), pl.BlockSpec((half, tm), lambda bi, i: (0, i)),
                 pl.BlockSpec((tm, LANES), lambda bi, i: (i, 0)), pl.BlockSpec((tm, LANES), lambda bi, i: (i, 0)),
                 pl.BlockSpec(ones_row.shape, lambda bi, i: (0, 0))]
    assert len(ins) == N_IN_OPERANDS
    return ins, in_specs


def _in_proj_outputs(b, s, tm):
    sds = jax.ShapeDtypeStruct
    out_shape = [
        sds((b, MLA_HEADS, LANES, s), BF16),
        sds((b, MLA_HEADS, s, LANES), BF16),
        sds((b, MLA_HEADS, MLA_V, s), BF16),
        sds((b, DIFF_HEADS * 2 * DIFF_QK, s), BF16),
        sds((b, DIFF_HEADS, s, LANES), BF16),
        sds((b, W_DIFF, s), BF16),
        sds((b, W_SWA, s), BF16),
        sds((b, s, LANES), BF16),
        sds((b, SWA_KV_HEADS * SWA_DIM, s), BF16),
        sds((b, D_MIX, s), BF16),
    ]
    out_specs = [
        pl.BlockSpec((1, MLA_HEADS, LANES, tm), lambda bi, i: (bi, 0, 0, i)),
        pl.BlockSpec((1, MLA_HEADS, tm, LANES), lambda bi, i: (bi, 0, i, 0)),
        pl.BlockSpec((1, MLA_HEADS, MLA_V, tm), lambda bi, i: (bi, 0, 0, i)),
        pl.BlockSpec((1, DIFF_HEADS * 2 * DIFF_QK, tm), lambda bi, i: (bi, 0, i)),
        pl.BlockSpec((1, DIFF_HEADS, tm, LANES), lambda bi, i: (bi, 0, i, 0)),
        pl.BlockSpec((1, W_DIFF, tm), lambda bi, i: (bi, 0, i)),
        pl.BlockSpec((1, W_SWA, tm), lambda bi, i: (bi, 0, i)),
        pl.BlockSpec((1, tm, LANES), lambda bi, i: (bi, i, 0)),
        pl.BlockSpec((1, SWA_KV_HEADS * SWA_DIM, tm), lambda bi, i: (bi, 0, i)),
        pl.BlockSpec((1, D_MIX, tm), lambda bi, i: (bi, 0, i)),
    ]
    assert len(out_shape) == N_IN_OUTPUTS
    return out_shape, out_specs


def _n_out_operands(tm):
    return 8 + 2 * (tm // WINDOW + 2) + 1


def _swa_chunk(sink_ref, q_ref, kb, vb, bias_ref, variant, lo, layer):
    sbs = range(lo // WINDOW, (lo + OUT_CHUNK) // WINDOW)
    k_win = {sb: jnp.concatenate(kb[sb:sb + 3], axis=0) for sb in sbs}
    v_win = {sb: jnp.concatenate(vb[sb:sb + 3], axis=1) for sb in sbs}
    grp = SWA_HEADS // SWA_KV_HEADS
    zeros = jnp.zeros((SWA_DIM, WINDOW), BF16)
    chains = [(hq, sb) for hq in range(SWA_HEADS) for sb in sbs]
    s_t, p_t, den, o_t = {}, {}, {}, {}
    for hq, sb in chains:
        q_t = q_ref[0, hq * SWA_DIM:(hq + 1) * SWA_DIM, sb * WINDOW:(sb + 1) * WINDOW]
        q_pad = jnp.concatenate([q_t, zeros] if hq // grp == 0 else [zeros, q_t], axis=0)
        s_t[hq, sb] = _nn(k_win[sb], q_pad) + bias_ref[variant[sb], hq]
    for hq, sb in chains:
        sink = sink_ref[layer, hq] * LOG2E
        m = jnp.maximum(jnp.max(s_t[hq, sb], axis=0, keepdims=True), sink)
        p = jnp.exp2(s_t[hq, sb] - m)
        den[hq, sb] = jnp.sum(p, axis=0, keepdims=True) + jnp.exp2(sink - m)
        p_t[hq, sb] = p.astype(BF16)
    for hq, sb in chains:
        g = hq // grp
        o_t[hq, sb] = (_nn(v_win[sb][g * SWA_DIM:(g + 1) * SWA_DIM], p_t[hq, sb]) / den[hq, sb]).astype(BF16)
    return jnp.concatenate([jnp.concatenate([o_t[hq, sb] for sb in sbs], axis=1) for hq in range(SWA_HEADS)],
                           axis=0)


def _out_proj_chunks(x_ref, om_ref, od_ref, g_ref, w_ref, lng_ref, lnb_ref, sink_ref, sq_ref, *swa_refs,
                     alpha, layer, n_tiles):
    tm = x_ref.shape[1]
    n_kb = tm // WINDOW + 2
    kb = [r[0] for r in swa_refs[:n_kb]]
    vb = [r[0] for r in swa_refs[n_kb:2 * n_kb]]
    bias_ref = swa_refs[2 * n_kb]
    tile = pl.program_id(1)
    variant = [0] * (tm // WINDOW)
    variant[0] = jnp.where(tile == 0, 1, 0)
    variant[-1] = jnp.where(tile == n_tiles - 1, 2, 0)
    for lo in range(0, tm, OUT_CHUNK):
        tok = slice(lo, lo + OUT_CHUNK)
        os_t = _swa_chunk(sink_ref, sq_ref, kb, vb, bias_ref, variant, lo, layer)
        o_t = jnp.concatenate([om_ref[0, :, tok], od_ref[0, :, tok], os_t], axis=0)
        og_t = o_t * g_ref[0, :, tok]
        y = lax.dot_general(og_t, w_ref[...], (((0,), (0,)), ((), ())), preferred_element_type=F32)
        z = alpha * x_ref[0, tok, :] + y
        mu = jnp.mean(z, axis=1, keepdims=True)
        zc = z - mu
        var = jnp.mean(zc * zc, axis=1, keepdims=True)
        yield tok, zc * lax.rsqrt(var + LN_EPS) * lng_ref[...] + lnb_ref[...]


def _mix_kernel(x_ref, *refs, out_layer, has_in, alpha, n_tiles):
    has_out = out_layer is not None
    n_out = _n_out_operands(x_ref.shape[1]) if has_out else 0
    n_in = n_out + (N_IN_OPERANDS if has_in else 0)
    ins, outs = refs[:n_in], refs[n_in:]
    if has_out:
        out_ops, ins = ins[:n_out], ins[n_out:]
        xo_ref, outs = outs[0], outs[1:]
        rows = []
        for tok, x_new in _out_proj_chunks(x_ref, *out_ops, alpha=alpha, layer=out_layer, n_tiles=n_tiles):
            xo_ref[0, tok, :] = x_new
            rows.append(x_new.astype(BF16))
        xb = jnp.concatenate(rows, axis=0)
    else:
        xb = x_ref[0].astype(BF16)
    if has_in:
        _in_proj_body(xb, *ins, *outs)


def _mix(x, prev, nxt, tabs, tm, alpha):
    b, s, d = x.shape
    ins, in_specs = [x], [pl.BlockSpec((1, tm, d), lambda bi, i: (bi, i, 0))]
    out_shape, out_specs = [], []
    out_layer = None
    if prev is not None:
        om, od, g, sq, sk, sv, swa_bias, swa_sink, w_out, ln_g, ln_b, out_layer = prev
        assert tm % OUT_CHUNK == 0 and tm >= 2 * OUT_CHUNK and OUT_CHUNK % WINDOW == 0
        tspec = lambda rows: pl.BlockSpec((1, rows, tm), lambda bi, i: (bi, 0, i))
        r, nblk = tm // WINDOW, s // WINDOW
        n_kb = r + 2
        kspec = lambda c: pl.BlockSpec((1, WINDOW, LANES), lambda bi, i: (bi, jnp.clip(i * r - 1 + c, 0, nblk - 1), 0))
        vspec = lambda c: pl.BlockSpec((1, SWA_KV_HEADS * SWA_DIM, WINDOW),
                                       lambda bi, i: (bi, 0, jnp.clip(i * r - 1 + c, 0, nblk - 1)))
        ins += [om, od, g, w_out, ln_g, ln_b, swa_sink, sq] + [sk] * n_kb + [sv] * n_kb + [swa_bias]
        in_specs += [tspec(W_MLA), tspec(W_DIFF), tspec(D_MIX),
                     _layer_spec(w_out, out_layer), _layer_spec(ln_g, out_layer), _layer_spec(ln_b, out_layer),
                     pl.BlockSpec(memory_space=pltpu.SMEM), tspec(W_SWA)]
        in_specs += [kspec(c) for c in range(n_kb)] + [vspec(c) for c in range(n_kb)]
        in_specs += [pl.BlockSpec(swa_bias.shape, lambda bi, i: (0, 0, 0, 0))]
        assert len(ins) - 1 == _n_out_operands(tm)
        out_shape.append(jax.ShapeDtypeStruct((b, s, d), F32))
        out_specs.append(pl.BlockSpec((1, tm, d), lambda bi, i: (bi, i, 0)))
    if nxt is not None:
        lw, layer = nxt
        more_ins, more_specs = _in_proj_operands(lw, layer, tabs, tm)
        ins, in_specs = ins + more_ins, in_specs + more_specs
        shapes, specs = _in_proj_outputs(b, s, tm)
        out_shape, out_specs = out_shape + shapes, out_specs + specs
    return pl.pallas_call(
        functools.partial(_mix_kernel, out_layer=out_layer, has_in=nxt is not None, alpha=alpha, n_tiles=s // tm),
        grid=(b, s // tm),
        in_specs=in_specs,
        out_specs=out_specs,
        out_shape=out_shape,
        compiler_params=_cparams(("parallel", "parallel")),
        name="mix_" + ("out" if prev is not None else "") + ("in" if nxt is not None else ""),
    )(*ins)


def _with_ones(v_t):
    tk = v_t.shape[1]
    row = lax.broadcasted_iota(jnp.int32, (ONES_ROWS, tk), 0)
    return jnp.concatenate([v_t, jnp.where(row == 0, 1.0, 0.0).astype(v_t.dtype)], axis=0)


def _online_step(s_t, tile_max, v_ext, m, acc):
    m_new = jnp.maximum(m, tile_max)
    p = jnp.exp2(s_t - m_new).astype(BF16)
    alpha = jnp.exp2(m - m_new)
    return m_new, alpha * acc + _nn(v_ext, p)


def _tile_off(j, t):
    return j * t if isinstance(j, int) else pl.multiple_of(j * t, t)


PIPE_TILES = 2
COL_BLOCK = 256
MAX_UNROLLED_TILES = 64


def _pipelined_tiles(n_tiles, n_sub, produce_ctx, produce, consume_ctx, consume, carry):
    u_n = PIPE_TILES
    assert n_tiles % u_n == 0 and n_tiles <= MAX_UNROLLED_TILES
    carry = list(carry)
    for u in range(u_n):
        ctx = produce_ctx(u)
        for c in range(n_sub):
            produce(0, u, c, ctx)
    for half in range(n_tiles // u_n):
        c_slot, v_c = half % 2, half * u_n
        v_p = v_c + u_n
        for u in range(u_n):
            pctx = produce_ctx(v_p + u) if v_p < n_tiles else None
            cctx = consume_ctx(v_c + u)
            for c in range(n_sub):
                carry[c] = consume(c_slot, u, c, cctx, carry[c])
                if pctx is not None:
                    produce(1 - c_slot, u, c, pctx)
    return tuple(carry)


def _score_scratch(n_sub, tk):
    n = 2 * PIPE_TILES * n_sub
    return [pltpu.VMEM((1, tk, COL_BLOCK), F32)] * n + [pltpu.VMEM((1, 1, COL_BLOCK), F32)] * n


def _score_refs(bufs, n_sub):
    n = 2 * PIPE_TILES * n_sub
    z = jnp.minimum(pl.program_id(0), 0)
    at = lambda slot, u, c: (slot * PIPE_TILES + u) * n_sub + c
    return (lambda slot, u, c: bufs[at(slot, u, c)].at[z]), (lambda slot, u, c: bufs[n + at(slot, u, c)].at[z])


def _mla_attn_kernel(q_ref, k_ref, v_ref, o_ref, *bufs, tk, nk):
    tq = q_ref.shape[3]
    n_sub = tq // COL_BLOCK
    s_buf, mx_buf = _score_refs(bufs, n_sub)
    q_blk = [q_ref[0, 0, :, c * COL_BLOCK:(c + 1) * COL_BLOCK] for c in range(n_sub)]

    def produce_ctx(j):
        return k_ref[0, 0, pl.ds(_tile_off(j, tk), tk), :]

    def produce(slot, u, c, k):
        s_t = _nn(k, q_blk[c])
        s_buf(slot, u, c)[...] = s_t
        mx_buf(slot, u, c)[...] = jnp.max(s_t, axis=0, keepdims=True)

    def consume_ctx(j):
        return _with_ones(v_ref[0, 0, :, pl.ds(_tile_off(j, tk), tk)])

    def consume(slot, u, c, v_ext, carry):
        return _online_step(s_buf(slot, u, c)[...], mx_buf(slot, u, c)[...], v_ext, *carry)

    m0 = jnp.full((1, COL_BLOCK), NEG_INF, F32)
    acc0 = jnp.zeros((MLA_V + ONES_ROWS, COL_BLOCK), F32)
    carry = _pipelined_tiles(nk, n_sub, produce_ctx, produce, consume_ctx, consume, ((m0, acc0),) * n_sub)
    for c, (_, acc) in enumerate(carry):
        o_ref[0, :, c * COL_BLOCK:(c + 1) * COL_BLOCK] = (acc[:MLA_V] / acc[MLA_V:MLA_V + 1]).astype(o_ref.dtype)


def _mla_attention(qm, km, vm, tq, tk):
    b, h, _, s = qm.shape
    return pl.pallas_call(
        functools.partial(_mla_attn_kernel, tk=tk, nk=s // tk),
        grid=(b, h, s // tq),
        in_specs=[pl.BlockSpec((1, 1, LANES, tq), lambda bi, hi, qi: (bi, hi, 0, qi)),
                  pl.BlockSpec((1, 1, s, LANES), lambda bi, hi, qi: (bi, hi, 0, 0)),
                  pl.BlockSpec((1, 1, MLA_V, s), lambda bi, hi, qi: (bi, hi, 0, 0))],
        out_specs=pl.BlockSpec((1, MLA_V, tq), lambda bi, hi, qi: (bi, hi, qi)),
        out_shape=jax.ShapeDtypeStruct((b, W_MLA, s), BF16),
        scratch_shapes=_score_scratch(tq // COL_BLOCK, tk),
        compiler_params=_cparams(("parallel", "parallel", "arbitrary")),
        name="mla_attention",
    )(qm, km, vm)


NEAR_TILES = 4
Q_NEAR = 2


def _diff_near_offsets(r):
    return min(-1, r - NEAR_TILES), NEAR_TILES - 1


def _diff_attn_kernel(far_ref, lam_ref, q_ref, k_ref, v_ref, bias_ref, subln_ref, o_ref,
                      q_buf, *bufs, tq, tk, nk, layer, lam_init):
    s_buf, mx_buf = _score_refs(bufs, 2 * (tq // COL_BLOCK))
    h = pl.program_id(1)
    qi = pl.program_id(2)
    t = tq
    r = tq // tk
    q_t = q_ref[0]
    r16 = lax.broadcasted_iota(jnp.int32, (ONES_ROWS, t), 0)
    zero_map = jnp.zeros((DIFF_QK, t), BF16)
    zero_pad = jnp.zeros((LANES - 2 * DIFF_QK - ONES_ROWS, t), BF16)

    def q_aug(mp, side):
        own = q_t[mp * DIFF_QK:(mp + 1) * DIFF_QK]
        own = [own, zero_map] if mp == 0 else [zero_map, own]
        if side is None:
            aug = jnp.zeros((ONES_ROWS, t), BF16)
        else:
            c = jnp.full((ONES_ROWS, t), far_ref[side, h], F32)
            c_hi = c.astype(BF16).astype(F32)
            aug = jnp.where(r16 == 0, c_hi, jnp.where(r16 == 1, c - c_hi, 0.0)).astype(BF16)
        return jnp.concatenate(own + [aug, zero_pad], axis=0)

    near_lo = jnp.clip(r * qi - 1, 0, nk - NEAR_TILES)
    e_min, _ = _diff_near_offsets(r)
    for mp in range(2):
        for side in range(2):
            q_buf[side, mp] = q_aug(mp, side)
        q_buf[Q_NEAR, mp] = q_aug(mp, None)

    def key_tile(v):
        if v < NEAR_TILES:
            return near_lo + v, True, None
        idx = v - NEAR_TILES
        side = (idx >= near_lo).astype(jnp.int32)
        return idx + NEAR_TILES * side, False, side

    n_cb = tq // COL_BLOCK
    cols = lambda cb: slice(cb * COL_BLOCK, (cb + 1) * COL_BLOCK)

    def produce_ctx(v):
        j, near, side = key_tile(v)
        return k_ref[0, 0, pl.ds(_tile_off(j, tk), tk), :], near, side, j

    def produce(slot, u, c, ctx):
        k, near, side, j = ctx
        mp, cb = divmod(c, n_cb)
        if near:
            s_t = _nn(k, q_buf[Q_NEAR, mp, :, cols(cb)]) + bias_ref[0, j - r * qi - e_min, :, cols(cb)]
        else:
            s_t = _nn(k, q_buf[side, mp, :, cols(cb)])
        s_buf(slot, u, c)[...] = s_t
        mx_buf(slot, u, c)[...] = jnp.max(s_t, axis=0, keepdims=True)

    def consume_ctx(v):
        j, _, _ = key_tile(v)
        return _with_ones(v_ref[0, :, pl.ds(_tile_off(j, tk), tk)])

    def consume(slot, u, c, v_ext, carry):
        return _online_step(s_buf(slot, u, c)[...], mx_buf(slot, u, c)[...], v_ext, *carry)

    m0 = jnp.full((1, COL_BLOCK), NEG_INF, F32)
    acc0 = jnp.zeros((DIFF_V + ONES_ROWS, COL_BLOCK), F32)
    carry = _pipelined_tiles(nk, 2 * n_cb, produce_ctx, produce, consume_ctx, consume,
                             ((m0, acc0),) * (2 * n_cb))

    lam = lam_ref[layer, 0]
    for cb in range(n_cb):
        a0, a1 = carry[cb][1], carry[n_cb + cb][1]
        o = a0[:DIFF_V] / a0[DIFF_V:DIFF_V + 1] - lam * (a1[:DIFF_V] / a1[DIFF_V:DIFF_V + 1])
        ms = jnp.mean(o * o, axis=0, keepdims=True)
        o = o * lax.rsqrt(ms + RMS_EPS) * subln_ref[...] * (1.0 - lam_init)
        o_ref[0, :, cols(cb)] = o.astype(o_ref.dtype)


def _diff_attention(qd, kd, vd, bias_tiles, far, lam, subln_col, layer, tq, tk, lam_init):
    b, _, s = qd.shape
    n_e = bias_tiles.shape[1]
    assert tq % tk == 0 and tq // tk + 2 <= NEAR_TILES and tk >= REL_MAX_DIST
    smem = pl.BlockSpec(memory_space=pltpu.SMEM)
    return pl.pallas_call(
        functools.partial(_diff_attn_kernel, tq=tq, tk=tk, nk=s // tk, layer=layer, lam_init=lam_init),
        grid=(b, DIFF_HEADS, s // tq),
        in_specs=[smem, smem,
                  pl.BlockSpec((1, 2 * DIFF_QK, tq), lambda bi, hi, qi: (bi, hi, qi)),
                  pl.BlockSpec((1, 1, s, LANES), lambda bi, hi, qi: (bi, hi, 0, 0)),
                  pl.BlockSpec((1, DIFF_V, s), lambda bi, hi, qi: (bi, hi, 0)),
                  pl.BlockSpec((1, n_e, tk, tq), lambda bi, hi, qi: (hi, 0, 0, 0)),
                  _layer_spec(subln_col, layer)],
        out_specs=pl.BlockSpec((1, DIFF_V, tq), lambda bi, hi, qi: (bi, hi, qi)),
        out_shape=jax.ShapeDtypeStruct((b, W_DIFF, s), BF16),
        scratch_shapes=[pltpu.VMEM((3, 2, LANES, tq), BF16)] + _score_scratch(2 * (tq // COL_BLOCK), tk),
        compiler_params=_cparams(("parallel", "parallel", "arbitrary")),
        name="diff_attention",
    )(far, lam, qd, kd, vd, bias_tiles, subln_col)


def _layer_weights(w_in, q_norm, kv_norm, w_uq, w_ukv):
    cq, ckv, kr, dq, dk, dv, sq, sk, sv, gate = jnp.split(w_in, np.cumsum(IN_SIZES)[:-1].tolist(), axis=1)
    d = w_in.shape[0]
    half = MLA_ROPE // 2
    w_t = jnp.concatenate([cq, ckv, dq, dv, sq, sv, gate], axis=1).T.astype(BF16)
    pad = lambda a, lo, width: jnp.pad(a, ((0, 0), (lo, width - lo - a.shape[1])))
    kr_p = pad(kr, MLA_NOPE, LANES)
    kr_rot_p = pad(jnp.concatenate([-kr[:, half:], kr[:, :half]], axis=1), MLA_NOPE, LANES)
    dk_p = jnp.concatenate([pad(dk[:, h * 2 * DIFF_QK:(h + 1) * 2 * DIFF_QK], 0, LANES)
                            for h in range(DIFF_HEADS)], axis=1)
    w_r = jnp.concatenate([ckv, kr_p, kr_rot_p, dk_p, sk], axis=1).astype(BF16)
    assert w_t.shape == (_T_ROWS, d) and w_r.shape == (d, _R_COLS)
    qk = MLA_NOPE + MLA_ROPE
    w_uq_t = jnp.concatenate([jnp.pad(w_uq[:, h * qk:(h + 1) * qk].T, ((0, LANES - qk), (0, 0)))
                              for h in range(MLA_HEADS)], axis=0).astype(BF16)
    kv = MLA_NOPE + MLA_V
    w_kp = jnp.concatenate([pad(w_ukv[:, h * kv:h * kv + MLA_NOPE], 0, LANES)
                            for h in range(MLA_HEADS)], axis=1).astype(BF16)
    w_v_t = jnp.concatenate([w_ukv[:, h * kv + MLA_NOPE:(h + 1) * kv].T
                             for h in range(MLA_HEADS)], axis=0).astype(BF16)
    return dict(w_t=w_t, w_r=w_r, w_uq_t=w_uq_t, w_kp=w_kp, w_v_t=w_v_t,
                gq_col=q_norm.reshape(-1, 1), gkv_col=kv_norm.reshape(-1, 1), gkv_row=kv_norm.reshape(1, -1))


def _lambda_kernel(q1_ref, k1_ref, q2_ref, k2_ref, init_ref, o_ref):
    a = jnp.sum(q1_ref[...] * k1_ref[...], axis=1, keepdims=True)
    c = jnp.sum(q2_ref[...] * k2_ref[...], axis=1, keepdims=True)
    o_ref[...] = jnp.exp(a) - jnp.exp(c) + init_ref[...]


def _diff_lambda(lam_vecs, lam_inits):
    depth = lam_vecs.shape[0]
    return pl.pallas_call(
        _lambda_kernel,
        out_shape=jax.ShapeDtypeStruct((depth, 1), F32),
        name="diff_lambda",
    )(*(lam_vecs[:, i, :] for i in range(4)), jnp.asarray(np.asarray(lam_inits, np.float32).reshape(depth, 1)))


def _tiles(seq):
    return dict(tm=min(512, seq), mla_tq=min(1024, seq), mla_tk=256, diff_tq=min(512, seq), diff_tk=256)


def kernel(x, w_in, mla_q_norm, mla_kv_norm, mla_w_uq, mla_w_ukv, diff_lambda, diff_subln, swa_sink,
           rel_bias, w_out, ln_g, ln_b):
    depth = w_in.shape[0]
    seq = x.shape[1]
    cfg = _tiles(seq)
    alpha = (2 * depth) ** 0.25
    tabs = _rope_tables(seq)
    e_min, e_max = _diff_near_offsets(cfg["diff_tq"] // cfg["diff_tk"])
    diff_bias = _diff_bias_tiles(rel_bias, cfg["diff_tq"], cfg["diff_tk"], e_min, e_max)
    swa_bias = _swa_bias_tiles(rel_bias)
    half = REL_BUCKETS // 2
    far = jnp.stack([rel_bias[half - 1, :DIFF_HEADS], rel_bias[2 * half - 1, :DIFF_HEADS]]) * LOG2E
    lam_inits = [0.8 - 0.6 * math.exp(-0.3 * l) for l in range(depth)]
    lam = _diff_lambda(diff_lambda, lam_inits)
    lw = jax.vmap(_layer_weights)(w_in, mla_q_norm, mla_kv_norm, mla_w_uq, mla_w_ukv)
    subln_col = diff_subln[:, :, None]
    w_out_b, ln_g3, ln_b3 = w_out.astype(BF16), ln_g[:, None, :], ln_b[:, None, :]
    prev = None
    for l in range(depth + 1):
        nxt = (lw, l) if l < depth else None
        outs = _mix(x, prev, nxt, tabs, cfg["tm"], alpha)
        if prev is not None:
            x, outs = outs[0], outs[1:]
        if nxt is None:
            return x
        qm, km, vm, qd, kd, vd, sq, sk, sv, g = outs
        om = _mla_attention(qm, km, vm, cfg["mla_tq"], cfg["mla_tk"])
        od = _diff_attention(qd, kd, vd, diff_bias, far, lam, subln_col, l, cfg["diff_tq"], cfg["diff_tk"],
                             lam_inits[l])
        prev = (om, od, g, sq, sk, sv, swa_bias, swa_sink, w_out_b, ln_g3, ln_b3, l)
```

```python
import functools
import math

import numpy as np
import jax
import jax.numpy as jnp
from jax import lax
from jax.experimental import pallas as pl
from jax.experimental.pallas import tpu as pltpu

D_MODEL = 1024
MLA_HEADS, MLA_Q_RANK, MLA_KV_RANK, MLA_NOPE, MLA_ROPE, MLA_V = 6, 192, 128, 64, 32, 64
DIFF_HEADS, DIFF_QK, DIFF_V = 4, 32, 64
SWA_HEADS, SWA_KV_HEADS, SWA_DIM, WINDOW = 6, 2, 64, 128
W_MLA, W_DIFF, W_SWA = MLA_HEADS * MLA_V, DIFF_HEADS * DIFF_V, SWA_HEADS * SWA_DIM
D_MIX = W_MLA + W_DIFF + W_SWA
IN_SIZES = (MLA_Q_RANK, MLA_KV_RANK, MLA_ROPE, DIFF_HEADS * 2 * DIFF_QK, DIFF_HEADS * 2 * DIFF_QK,
            W_DIFF, W_SWA, SWA_KV_HEADS * SWA_DIM, SWA_KV_HEADS * SWA_DIM, D_MIX)
REL_BUCKETS, REL_MAX_DIST = 32, 128
ROPE_THETA = 10000.0
RMS_EPS, LN_EPS = 1e-6, 1e-5

LOG2E = math.log2(math.e)
LANES = 128
ONES_ROWS = 16
V_ONES_ROWS = 8
VMEM_LIMIT = 56 * 1024 * 1024
OUT_CHUNK = 256

F32, BF16 = jnp.float32, jnp.bfloat16
NEG_INF = float("-inf")


def _cparams(sem):
    return pltpu.CompilerParams(dimension_semantics=sem, vmem_limit_bytes=VMEM_LIMIT)


def _nt(a, b):
    return lax.dot_general(a, b, (((1,), (1,)), ((), ())), preferred_element_type=F32)


def _nn(a, b):
    return jnp.dot(a, b, preferred_element_type=F32)


def _rope_tables_kernel(fcol_ref, frow_ref, cos_t_ref, sin_t_ref, cos_p_ref, sin_p_ref, *, tile):
    p0 = pl.program_id(0) * tile
    pos_l = (p0 + lax.broadcasted_iota(jnp.int32, (MLA_ROPE // 2, tile), 1)).astype(F32)
    ang_t = pos_l * fcol_ref[...]
    cos_t_ref[...] = jnp.cos(ang_t)
    sin_t_ref[...] = jnp.sin(ang_t)
    pos_s = (p0 + lax.broadcasted_iota(jnp.int32, (tile, LANES), 0)).astype(F32)
    ang_p = pos_s * frow_ref[...]
    lane = lax.broadcasted_iota(jnp.int32, (tile, LANES), 1)
    live = (lane >= MLA_NOPE) & (lane < MLA_NOPE + MLA_ROPE)
    cos_p_ref[...] = jnp.where(live, jnp.cos(ang_p), 0.0)
    sin_p_ref[...] = jnp.where(live, jnp.sin(ang_p), 0.0)


def _rope_tables(seq):
    half = MLA_ROPE // 2
    freqs = (ROPE_THETA ** (-np.arange(half, dtype=np.float64) / half)).astype(np.float32)
    fcol = jnp.asarray(freqs.reshape(half, 1))
    frow_np = np.zeros((1, LANES), np.float32)
    frow_np[0, MLA_NOPE:MLA_NOPE + half] = freqs
    frow_np[0, MLA_NOPE + half:MLA_NOPE + MLA_ROPE] = freqs
    frow = jnp.asarray(frow_np)
    tile = min(seq, 1024)
    return pl.pallas_call(
        functools.partial(_rope_tables_kernel, tile=tile),
        grid=(seq // tile,),
        in_specs=[pl.BlockSpec((half, 1), lambda i: (0, 0)), pl.BlockSpec((1, LANES), lambda i: (0, 0))],
        out_specs=[pl.BlockSpec((half, tile), lambda i: (0, i)), pl.BlockSpec((half, tile), lambda i: (0, i)),
                   pl.BlockSpec((tile, LANES), lambda i: (i, 0)), pl.BlockSpec((tile, LANES), lambda i: (i, 0))],
        out_shape=[jax.ShapeDtypeStruct((half, seq), F32), jax.ShapeDtypeStruct((half, seq), F32),
                   jax.ShapeDtypeStruct((seq, LANES), F32), jax.ShapeDtypeStruct((seq, LANES), F32)],
        compiler_params=_cparams(("arbitrary",)),
        name="rope_tables",
    )(fcol, frow)


def _t5_bucket(rel):
    half = REL_BUCKETS // 2
    max_exact = half // 2
    n = jnp.abs(rel)
    large = jnp.full(rel.shape, max_exact, jnp.int32)
    for k in range(1, half - max_exact):
        thr = max_exact * (REL_MAX_DIST / max_exact) ** (k / (half - max_exact))
        thr_i = int(round(thr)) if abs(thr - round(thr)) < 1e-9 else int(math.ceil(thr))
        large = large + (n >= thr_i).astype(jnp.int32)
    return jnp.where(rel > 0, half, 0) + jnp.where(n < max_exact, n, large)


def _bias_lookup(tab_ref, bucket, head):
    out = jnp.zeros(bucket.shape, F32)
    for b in range(REL_BUCKETS):
        out = jnp.where(bucket == b, tab_ref[b, head], out)
    return out


def _diff_bias_kernel(tab_ref, out_ref, *, tq, tk, e_min):
    h = pl.program_id(0)
    d = (pl.program_id(1) + e_min) * tk
    ki = lax.broadcasted_iota(jnp.int32, (tk, tq), 0)
    qj = lax.broadcasted_iota(jnp.int32, (tk, tq), 1)
    out_ref[0, 0] = _bias_lookup(tab_ref, _t5_bucket(d + ki - qj), h) * LOG2E


def _diff_bias_tiles(rel_bias, tq, tk, e_min, e_max):
    n_e = e_max - e_min + 1
    return pl.pallas_call(
        functools.partial(_diff_bias_kernel, tq=tq, tk=tk, e_min=e_min),
        grid=(DIFF_HEADS, n_e),
        in_specs=[pl.BlockSpec(memory_space=pltpu.SMEM)],
        out_specs=pl.BlockSpec((1, 1, tk, tq), lambda h, d: (h, d, 0, 0)),
        out_shape=jax.ShapeDtypeStruct((DIFF_HEADS, n_e, tk, tq), F32),
        compiler_params=_cparams(("arbitrary", "arbitrary")),
        name="diff_bias_tiles",
    )(rel_bias)


def _swa_bias_kernel(tab_ref, out_ref):
    variant = pl.program_id(0)
    h = pl.program_id(1)
    ki = lax.broadcasted_iota(jnp.int32, (3 * WINDOW, WINDOW), 0)
    qj = lax.broadcasted_iota(jnp.int32, (3 * WINDOW, WINDOW), 1)
    rel = ki - WINDOW - qj
    bias = _bias_lookup(tab_ref, _t5_bucket(rel), DIFF_HEADS + h) * LOG2E
    lo = jnp.where(variant == 1, WINDOW, 0)
    hi = jnp.where(variant == 2, 2 * WINDOW, 3 * WINDOW)
    live = (jnp.abs(rel) <= WINDOW) & (ki >= lo) & (ki < hi)
    out_ref[0, 0] = jnp.where(live, bias, NEG_INF)


def _swa_bias_tiles(rel_bias):
    return pl.pallas_call(
        _swa_bias_kernel,
        grid=(3, SWA_HEADS),
        in_specs=[pl.BlockSpec(memory_space=pltpu.SMEM)],
        out_specs=pl.BlockSpec((1, 1, 3 * WINDOW, WINDOW), lambda v, h: (v, h, 0, 0)),
        out_shape=jax.ShapeDtypeStruct((3, SWA_HEADS, 3 * WINDOW, WINDOW), F32),
        compiler_params=_cparams(("arbitrary", "arbitrary")),
        name="swa_bias_tiles",
    )(rel_bias)


_T_CQ, _T_CKV, _T_DQ, _T_DV, _T_SQ, _T_SV, _T_GATE = 0, 192, 320, 576, 832, 1216, 1344
_T_ROWS = 2368
_R_CKV, _R_KR, _R_KRROT, _R_DK, _R_SK = 0, 128, 256, 384, 896
_R_COLS = 1024


def _rms_t(v, g_col):
    ms = jnp.mean(v * v, axis=0, keepdims=True)
    return v * lax.rsqrt(ms + RMS_EPS) * g_col


N_IN_OPERANDS, N_IN_OUTPUTS = 13, 10


def _in_proj_body(xb, wt_ref, wr_ref, wuq_ref, wkp_ref, wvt_ref, gq_ref, gkv_col_ref, gkv_row_ref,
                  cos_t_ref, sin_t_ref, cos_p_ref, sin_p_ref, ones_ref,
                  qm_ref, km_ref, vm_ref, qd_ref, kd_ref, vd_ref, sq_ref, sk_ref, sv_ref, g_ref):
    def proj_t(lo, hi):
        return _nt(wt_ref[lo:hi, :], xb)

    def gate_rows(lo, hi):
        gate = proj_t(_T_GATE + lo, _T_GATE + hi)
        g_ref[0, lo:hi, :] = (gate * jax.nn.sigmoid(gate)).astype(BF16)

    cq_t = proj_t(_T_CQ, _T_CKV)
    ckv_t = proj_t(_T_CKV, _T_DQ)
    r = _nn(xb, wr_ref[...])
    gate_rows(0, D_MIX // 2)

    cqn = _rms_t(cq_t, gq_ref[...]).astype(BF16)
    q_all = _nn(wuq_ref[...], cqn)
    ckvn_t = _rms_t(ckv_t, gkv_col_ref[...]).astype(BF16)
    v_all = _nn(wvt_ref[...], ckvn_t).astype(BF16)
    ckv = r[:, _R_CKV:_R_KR]
    ms = jnp.mean(ckv * ckv, axis=1, keepdims=True)
    ckvn = (ckv * lax.rsqrt(ms + RMS_EPS) * gkv_row_ref[...]).astype(BF16)
    k_nope = _nn(ckvn, wkp_ref[...])
    gate_rows(D_MIX // 2, D_MIX)

    cos_t, sin_t = cos_t_ref[...], sin_t_ref[...]
    q_scale = LOG2E / math.sqrt(MLA_NOPE + MLA_ROPE)
    half = MLA_ROPE // 2
    for h in range(MLA_HEADS):
        blk = q_all[h * LANES:(h + 1) * LANES]
        x1 = blk[MLA_NOPE:MLA_NOPE + half]
        x2 = blk[MLA_NOPE + half:MLA_NOPE + MLA_ROPE]
        roped = jnp.concatenate([blk[:MLA_NOPE], x1 * cos_t - x2 * sin_t, x2 * cos_t + x1 * sin_t,
                                 blk[MLA_NOPE + MLA_ROPE:]], axis=0)
        qm_ref[0, h] = (roped * q_scale).astype(BF16)
    for h in range(MLA_HEADS):
        vm_ref[0, h] = v_all[h * MLA_V:(h + 1) * MLA_V]
    qd_ref[0] = (proj_t(_T_DQ, _T_DV) * (LOG2E / math.sqrt(DIFF_QK))).astype(BF16)

    k_rope = r[:, _R_KR:_R_KRROT] * cos_p_ref[...] + r[:, _R_KRROT:_R_DK] * sin_p_ref[...]
    for h in range(MLA_HEADS):
        km_ref[0, h] = (k_nope[:, h * LANES:(h + 1) * LANES] + k_rope).astype(BF16)
    ones_row = ones_ref[...]
    for h in range(DIFF_HEADS):
        kd_ref[0, h] = (r[:, _R_DK + h * LANES:_R_DK + (h + 1) * LANES] + ones_row).astype(BF16)
    sk_ref[0] = r[:, _R_SK:_R_COLS].astype(BF16)

    vd_ref[0] = proj_t(_T_DV, _T_SQ).astype(BF16)
    sq_ref[0] = (proj_t(_T_SQ, _T_SV) * (LOG2E / math.sqrt(SWA_DIM))).astype(BF16)
    sv_ref[0] = proj_t(_T_SV, _T_GATE).astype(BF16)


def _layer_spec(a, layer):
    return pl.BlockSpec((None,) + a.shape[1:], lambda *_: (layer,) + (0,) * (a.ndim - 1))


def _in_proj_operands(lw, layer, tabs, tm):
    cos_t, sin_t, cos_p, sin_p = tabs
    ones_np = np.zeros((1, LANES), np.float32)
    ones_np[0, 2 * DIFF_QK:2 * DIFF_QK + 2] = 1.0
    ones_row = jnp.asarray(ones_np)
    half = MLA_ROPE // 2
    weights = [lw["w_t"], lw["w_r"], lw["w_uq_t"], lw["w_kp"], lw["w_v_t"], lw["gq_col"], lw["gkv_col"],
               lw["gkv_row"]]
    ins = weights + [cos_t, sin_t, cos_p, sin_p, ones_row]
    in_specs = [_layer_spec(a, layer) for a in weights]
    in_specs += [pl.BlockSpec((half, tm), lambda bi, i: (0, i)), pl.BlockSpec((half, tm), lambda bi, i: (0, i)),
                 pl.BlockSpec((tm, LANES), lambda bi, i: (i, 0)), pl.BlockSpec((tm, LANES), lambda bi, i: (i, 0)),
                 pl.BlockSpec(ones_row.shape, lambda bi, i: (0, 0))]
    assert len(ins) == N_IN_OPERANDS
    return ins, in_specs


def _in_proj_outputs(b, s, tm):
    sds = jax.ShapeDtypeStruct
    out_shape = [
        sds((b, MLA_HEADS, LANES, s), BF16),
        sds((b, MLA_HEADS, s, LANES), BF16),
        sds((b, MLA_HEADS, MLA_V, s), BF16),
        sds((b, DIFF_HEADS * 2 * DIFF_QK, s), BF16),
        sds((b, DIFF_HEADS, s, LANES), BF16),
        sds((b, W_DIFF, s), BF16),
        sds((b, W_SWA, s), BF16),
        sds((b, s, LANES), BF16),
        sds((b, SWA_KV_HEADS * SWA_DIM, s), BF16),
        sds((b, D_MIX, s), BF16),
    ]
    out_specs = [
        pl.BlockSpec((1, MLA_HEADS, LANES, tm), lambda bi, i: (bi, 0, 0, i)),
        pl.BlockSpec((1, MLA_HEADS, tm, LANES), lambda bi, i: (bi, 0, i, 0)),
        pl.BlockSpec((1, MLA_HEADS, MLA_V, tm), lambda bi, i: (bi, 0, 0, i)),
        pl.BlockSpec((1, DIFF_HEADS * 2 * DIFF_QK, tm), lambda bi, i: (bi, 0, i)),
        pl.BlockSpec((1, DIFF_HEADS, tm, LANES), lambda bi, i: (bi, 0, i, 0)),
        pl.BlockSpec((1, W_DIFF, tm), lambda bi, i: (bi, 0, i)),
        pl.BlockSpec((1, W_SWA, tm), lambda bi, i: (bi, 0, i)),
        pl.BlockSpec((1, tm, LANES), lambda bi, i: (bi, i, 0)),
        pl.BlockSpec((1, SWA_KV_HEADS * SWA_DIM, tm), lambda bi, i: (bi, 0, i)),
        pl.BlockSpec((1, D_MIX, tm), lambda bi, i: (bi, 0, i)),
    ]
    assert len(out_shape) == N_IN_OUTPUTS
    return out_shape, out_specs


def _n_out_operands(tm):
    return 8 + 2 * (tm // WINDOW + 2) + 1


def _swa_chunk(sink_ref, q_ref, kb, vb, bias_ref, variant, lo, layer):
    sbs = range(lo // WINDOW, (lo + OUT_CHUNK) // WINDOW)
    k_win = {sb: jnp.concatenate(kb[sb:sb + 3], axis=0) for sb in sbs}
    v_win = {sb: jnp.concatenate(vb[sb:sb + 3], axis=1) for sb in sbs}
    grp = SWA_HEADS // SWA_KV_HEADS
    zeros = jnp.zeros((SWA_DIM, WINDOW), BF16)
    chains = [(hq, sb) for hq in range(SWA_HEADS) for sb in sbs]
    s_t, p_t, den, o_t = {}, {}, {}, {}
    for hq, sb in chains:
        q_t = q_ref[0, hq * SWA_DIM:(hq + 1) * SWA_DIM, sb * WINDOW:(sb + 1) * WINDOW]
        q_pad = jnp.concatenate([q_t, zeros] if hq // grp == 0 else [zeros, q_t], axis=0)
        s_t[hq, sb] = _nn(k_win[sb], q_pad) + bias_ref[variant[sb], hq]
    for hq, sb in chains:
        sink = sink_ref[layer, hq] * LOG2E
        m = jnp.maximum(jnp.max(s_t[hq, sb], axis=0, keepdims=True), sink)
        p = jnp.exp2(s_t[hq, sb] - m)
        den[hq, sb] = jnp.sum(p, axis=0, keepdims=True) + jnp.exp2(sink - m)
        p_t[hq, sb] = p.astype(BF16)
    for hq, sb in chains:
        g = hq // grp
        o_t[hq, sb] = (_nn(v_win[sb][g * SWA_DIM:(g + 1) * SWA_DIM], p_t[hq, sb]) / den[hq, sb]).astype(BF16)
    return jnp.concatenate([jnp.concatenate([o_t[hq, sb] for sb in sbs], axis=1) for hq in range(SWA_HEADS)],
                           axis=0)


def _out_proj_chunks(x_ref, om_ref, od_ref, g_ref, w_ref, lng_ref, lnb_ref, sink_ref, sq_ref, *swa_refs,
                     alpha, layer, n_tiles):
    tm = x_ref.shape[1]
    n_kb = tm // WINDOW + 2
    kb = [r[0] for r in swa_refs[:n_kb]]
    vb = [r[0] for r in swa_refs[n_kb:2 * n_kb]]
    bias_ref = swa_refs[2 * n_kb]
    tile = pl.program_id(1)
    variant = [0] * (tm // WINDOW)
    variant[0] = jnp.where(tile == 0, 1, 0)
    variant[-1] = jnp.where(tile == n_tiles - 1, 2, 0)
    for lo in range(0, tm, OUT_CHUNK):
        tok = slice(lo, lo + OUT_CHUNK)
        os_t = _swa_chunk(sink_ref, sq_ref, kb, vb, bias_ref, variant, lo, layer)
        o_t = jnp.concatenate([om_ref[0, :, tok], od_ref[0, :, tok], os_t], axis=0)
        og_t = o_t * g_ref[0, :, tok]
        y = lax.dot_general(og_t, w_ref[...], (((0,), (0,)), ((), ())), preferred_element_type=F32)
        z = alpha * x_ref[0, tok, :] + y
        mu = jnp.mean(z, axis=1, keepdims=True)
        zc = z - mu
        var = jnp.mean(zc * zc, axis=1, keepdims=True)
        yield tok, zc * lax.rsqrt(var + LN_EPS) * lng_ref[...] + lnb_ref[...]


def _mix_kernel(x_ref, *refs, out_layer, has_in, alpha, n_tiles):
    has_out = out_layer is not None
    n_out = _n_out_operands(x_ref.shape[1]) if has_out else 0
    n_in = n_out + (N_IN_OPERANDS if has_in else 0)
    ins, outs = refs[:n_in], refs[n_in:]
    if has_out:
        out_ops, ins = ins[:n_out], ins[n_out:]
        xo_ref, outs = outs[0], outs[1:]
        rows = []
        for tok, x_new in _out_proj_chunks(x_ref, *out_ops, alpha=alpha, layer=out_layer, n_tiles=n_tiles):
            xo_ref[0, tok, :] = x_new
            rows.append(x_new.astype(BF16))
        xb = jnp.concatenate(rows, axis=0)
    else:
        xb = x_ref[0].astype(BF16)
    if has_in:
        _in_proj_body(xb, *ins, *outs)


def _mix(x, prev, nxt, tabs, tm, alpha):
    b, s, d = x.shape
    ins, in_specs = [x], [pl.BlockSpec((1, tm, d), lambda bi, i: (bi, i, 0))]
    out_shape, out_specs = [], []
    out_layer = None
    if prev is not None:
        om, od, g, sq, sk, sv, swa_bias, swa_sink, w_out, ln_g, ln_b, out_layer = prev
        assert tm % OUT_CHUNK == 0 and tm >= 2 * OUT_CHUNK and OUT_CHUNK % WINDOW == 0
        tspec = lambda rows: pl.BlockSpec((1, rows, tm), lambda bi, i: (bi, 0, i))
        r, nblk = tm // WINDOW, s // WINDOW
        n_kb = r + 2
        kspec = lambda c: pl.BlockSpec((1, WINDOW, LANES), lambda bi, i: (bi, jnp.clip(i * r - 1 + c, 0, nblk - 1), 0))
        vspec = lambda c: pl.BlockSpec((1, SWA_KV_HEADS * SWA_DIM, WINDOW),
                                       lambda bi, i: (bi, 0, jnp.clip(i * r - 1 + c, 0, nblk - 1)))
        ins += [om, od, g, w_out, ln_g, ln_b, swa_sink, sq] + [sk] * n_kb + [sv] * n_kb + [swa_bias]
        in_specs += [tspec(W_MLA), tspec(W_DIFF), tspec(D_MIX),
                     _layer_spec(w_out, out_layer), _layer_spec(ln_g, out_layer), _layer_spec(ln_b, out_layer),
                     pl.BlockSpec(memory_space=pltpu.SMEM), tspec(W_SWA)]
        in_specs += [kspec(c) for c in range(n_kb)] + [vspec(c) for c in range(n_kb)]
        in_specs += [pl.BlockSpec(swa_bias.shape, lambda bi, i: (0, 0, 0, 0))]
        assert len(ins) - 1 == _n_out_operands(tm)
        out_shape.append(jax.ShapeDtypeStruct((b, s, d), F32))
        out_specs.append(pl.BlockSpec((1, tm, d), lambda bi, i: (bi, i, 0)))
    if nxt is not None:
        lw, layer = nxt
        more_ins, more_specs = _in_proj_operands(lw, layer, tabs, tm)
        ins, in_specs = ins + more_ins, in_specs + more_specs
        shapes, specs = _in_proj_outputs(b, s, tm)
        out_shape, out_specs = out_shape + shapes, out_specs + specs
    return pl.pallas_call(
        functools.partial(_mix_kernel, out_layer=out_layer, has_in=nxt is not None, alpha=alpha, n_tiles=s // tm),
        grid=(b, s // tm),
        in_specs=in_specs,
        out_specs=out_specs,
        out_shape=out_shape,
        compiler_params=_cparams(("parallel", "parallel")),
        name="mix_" + ("out" if prev is not None else "") + ("in" if nxt is not None else ""),
    )(*ins)


def _with_ones(v_t):
    tk = v_t.shape[1]
    row = lax.broadcasted_iota(jnp.int32, (V_ONES_ROWS, tk), 0)
    return jnp.concatenate([v_t.astype(F32), jnp.where(row == 0, 1.0, 0.0)], axis=0)


def _online_step(s_t, tile_max, v_ext, m, acc):
    m_new = jnp.maximum(m, tile_max)
    p = jnp.exp2(s_t - m_new).astype(BF16)
    alpha = jnp.exp2(m - m_new)
    return m_new, alpha * acc + _nn(v_ext, p)


def _tile_off(j, t):
    return j * t if isinstance(j, int) else pl.multiple_of(j * t, t)


PIPE_TILES = 2
COL_BLOCK = 256
MAX_UNROLLED_TILES = 64


def _pipelined_tiles(n_tiles, n_sub, produce_ctx, produce, consume_ctx, consume, carry):
    u_n = PIPE_TILES
    assert n_tiles % u_n == 0 and n_tiles <= MAX_UNROLLED_TILES
    carry = list(carry)
    for u in range(u_n):
        ctx = produce_ctx(u)
        for c in range(n_sub):
            produce(0, u, c, ctx)
    for half in range(n_tiles // u_n):
        c_slot, v_c = half % 2, half * u_n
        v_p = v_c + u_n
        for u in range(u_n):
            pctx = produce_ctx(v_p + u) if v_p < n_tiles else None
            cctx = consume_ctx(v_c + u)
            for c in range(n_sub):
                carry[c] = consume(c_slot, u, c, cctx, carry[c])
                if pctx is not None:
                    produce(1 - c_slot, u, c, pctx)
    return tuple(carry)


def _score_scratch(n_sub, tk):
    n = 2 * PIPE_TILES * n_sub
    return [pltpu.VMEM((1, tk, COL_BLOCK), F32)] * n + [pltpu.VMEM((1, 1, COL_BLOCK), F32)] * n


def _score_refs(bufs, n_sub):
    n = 2 * PIPE_TILES * n_sub
    z = jnp.minimum(pl.program_id(0), 0)
    at = lambda slot, u, c: (slot * PIPE_TILES + u) * n_sub + c
    return (lambda slot, u, c: bufs[at(slot, u, c)].at[z]), (lambda slot, u, c: bufs[n + at(slot, u, c)].at[z])


def _mla_attn_kernel(q_ref, k_ref, v_ref, o_ref, *bufs, tk, nk):
    tq = q_ref.shape[3]
    n_sub = tq // COL_BLOCK
    s_buf, mx_buf = _score_refs(bufs, n_sub)
    q_blk = [q_ref[0, 0, :, c * COL_BLOCK:(c + 1) * COL_BLOCK] for c in range(n_sub)]

    def produce_ctx(j):
        return j

    def produce(slot, u, c, j):
        k = k_ref[0, 0, pl.ds(_tile_off(j, tk), tk), :]
        s_t = _nn(k, q_blk[c])
        s_buf(slot, u, c)[...] = s_t
        mx_buf(slot, u, c)[...] = jnp.max(s_t, axis=0, keepdims=True)

    def consume_ctx(j):
        return _with_ones(v_ref[0, 0, :, pl.ds(_tile_off(j, tk), tk)])

    def consume(slot, u, c, v_ext, carry):
        return _online_step(s_buf(slot, u, c)[...], mx_buf(slot, u, c)[...], v_ext, *carry)

    m0 = jnp.full((1, COL_BLOCK), NEG_INF, F32)
    acc0 = jnp.zeros((MLA_V + V_ONES_ROWS, COL_BLOCK), F32)
    carry = _pipelined_tiles(nk, n_sub, produce_ctx, produce, consume_ctx, consume, ((m0, acc0),) * n_sub)
    for c, (_, acc) in enumerate(carry):
        o_ref[0, :, c * COL_BLOCK:(c + 1) * COL_BLOCK] = (acc[:MLA_V] / acc[MLA_V:MLA_V + 1]).astype(o_ref.dtype)


def _mla_attention(qm, km, vm, tq, tk):
    b, h, _, s = qm.shape
    return pl.pallas_call(
        functools.partial(_mla_attn_kernel, tk=tk, nk=s // tk),
        grid=(b, h, s // tq),
        in_specs=[pl.BlockSpec((1, 1, LANES, tq), lambda bi, hi, qi: (bi, hi, 0, qi)),
                  pl.BlockSpec((1, 1, s, LANES), lambda bi, hi, qi: (bi, hi, 0, 0)),
                  pl.BlockSpec((1, 1, MLA_V, s), lambda bi, hi, qi: (bi, hi, 0, 0))],
        out_specs=pl.BlockSpec((1, MLA_V, tq), lambda bi, hi, qi: (bi, hi, qi)),
        out_shape=jax.ShapeDtypeStruct((b, W_MLA, s), BF16),
        scratch_shapes=_score_scratch(tq // COL_BLOCK, tk),
        compiler_params=_cparams(("parallel", "parallel", "arbitrary")),
        name="mla_attention",
    )(qm, km, vm)


NEAR_TILES = 4
Q_NEAR = 2


def _diff_near_offsets(r):
    return min(-1, r - NEAR_TILES), NEAR_TILES - 1


def _diff_attn_kernel(far_ref, lam_ref, q_ref, k_ref, v_ref, bias_ref, subln_ref, o_ref,
                      q_buf, *bufs, tq, tk, nk, layer, lam_init):
    s_buf, mx_buf = _score_refs(bufs, 2 * (tq // COL_BLOCK))
    h = pl.program_id(1)
    qi = pl.program_id(2)
    t = tq
    r = tq // tk
    q_t = q_ref[0]
    r16 = lax.broadcasted_iota(jnp.int32, (ONES_ROWS, t), 0)
    zero_map = jnp.zeros((DIFF_QK, t), BF16)
    zero_pad = jnp.zeros((LANES - 2 * DIFF_QK - ONES_ROWS, t), BF16)

    def q_aug(mp, side):
        own = q_t[mp * DIFF_QK:(mp + 1) * DIFF_QK]
        own = [own, zero_map] if mp == 0 else [zero_map, own]
        if side is None:
            aug = jnp.zeros((ONES_ROWS, t), BF16)
        else:
            c = jnp.full((ONES_ROWS, t), far_ref[side, h], F32)
            c_hi = c.astype(BF16).astype(F32)
            aug = jnp.where(r16 == 0, c_hi, jnp.where(r16 == 1, c - c_hi, 0.0)).astype(BF16)
        return jnp.concatenate(own + [aug, zero_pad], axis=0)

    near_lo = jnp.clip(r * qi - 1, 0, nk - NEAR_TILES)
    e_min, _ = _diff_near_offsets(r)
    for mp in range(2):
        for side in range(2):
            q_buf[side, mp] = q_aug(mp, side)
        q_buf[Q_NEAR, mp] = q_aug(mp, None)

    def key_tile(v):
        if v < NEAR_TILES:
            return near_lo + v, True, None
        idx = v - NEAR_TILES
        side = (idx >= near_lo).astype(jnp.int32)
        return idx + NEAR_TILES * side, False, side

    n_cb = tq // COL_BLOCK
    cols = lambda cb: slice(cb * COL_BLOCK, (cb + 1) * COL_BLOCK)

    def produce_ctx(v):
        j, near, side = key_tile(v)
        return near, side, j

    def produce(slot, u, c, ctx):
        near, side, j = ctx
        k = k_ref[0, 0, pl.ds(_tile_off(j, tk), tk), :]
        mp, cb = divmod(c, n_cb)
        if near:
            s_t = _nn(k, q_buf[Q_NEAR, mp, :, cols(cb)]) + bias_ref[0, j - r * qi - e_min, :, cols(cb)]
        else:
            s_t = _nn(k, q_buf[side, mp, :, cols(cb)])
        s_buf(slot, u, c)[...] = s_t
        mx_buf(slot, u, c)[...] = jnp.max(s_t, axis=0, keepdims=True)

    def consume_ctx(v):
        j, _, _ = key_tile(v)
        return _with_ones(v_ref[0, :, pl.ds(_tile_off(j, tk), tk)])

    def consume(slot, u, c, v_ext, carry):
        return _online_step(s_buf(slot, u, c)[...], mx_buf(slot, u, c)[...], v_ext, *carry)

    m0 = jnp.full((1, COL_BLOCK), NEG_INF, F32)
    acc0 = jnp.zeros((DIFF_V + V_ONES_ROWS, COL_BLOCK), F32)
    carry = _pipelined_tiles(nk, 2 * n_cb, produce_ctx, produce, consume_ctx, consume,
                             ((m0, acc0),) * (2 * n_cb))

    lam = lam_ref[layer, 0]
    for cb in range(n_cb):
        a0, a1 = carry[cb][1], carry[n_cb + cb][1]
        o = a0[:DIFF_V] / a0[DIFF_V:DIFF_V + 1] - lam * (a1[:DIFF_V] / a1[DIFF_V:DIFF_V + 1])
        ms = jnp.mean(o * o, axis=0, keepdims=True)
        o = o * lax.rsqrt(ms + RMS_EPS) * subln_ref[...] * (1.0 - lam_init)
        o_ref[0, :, cols(cb)] = o.astype(o_ref.dtype)


def _diff_attention(qd, kd, vd, bias_tiles, far, lam, subln_col, layer, tq, tk, lam_init):
    b, _, s = qd.shape
    n_e = bias_tiles.shape[1]
    assert tq % tk == 0 and tq // tk + 2 <= NEAR_TILES and tk >= REL_MAX_DIST
    smem = pl.BlockSpec(memory_space=pltpu.SMEM)
    return pl.pallas_call(
        functools.partial(_diff_attn_kernel, tq=tq, tk=tk, nk=s // tk, layer=layer, lam_init=lam_init),
        grid=(b, DIFF_HEADS, s // tq),
        in_specs=[smem, smem,
                  pl.BlockSpec((1, 2 * DIFF_QK, tq), lambda bi, hi, qi: (bi, hi, qi)),
                  pl.BlockSpec((1, 1, s, LANES), lambda bi, hi, qi: (bi, hi, 0, 0)),
                  pl.BlockSpec((1, DIFF_V, s), lambda bi, hi, qi: (bi, hi, 0)),
                  pl.BlockSpec((1, n_e, tk, tq), lambda bi, hi, qi: (hi, 0, 0, 0)),
                  _layer_spec(subln_col, layer)],
        out_specs=pl.BlockSpec((1, DIFF_V, tq), lambda bi, hi, qi: (bi, hi, qi)),
        out_shape=jax.ShapeDtypeStruct((b, W_DIFF, s), BF16),
        scratch_shapes=[pltpu.VMEM((3, 2, LANES, tq), BF16)] + _score_scratch(2 * (tq // COL_BLOCK), tk),
        compiler_params=_cparams(("parallel", "parallel", "arbitrary")),
        name="diff_attention",
    )(far, lam, qd, kd, vd, bias_tiles, subln_col)


def _layer_weights(w_in, q_norm, kv_norm, w_uq, w_ukv):
    cq, ckv, kr, dq, dk, dv, sq, sk, sv, gate = jnp.split(w_in, np.cumsum(IN_SIZES)[:-1].tolist(), axis=1)
    d = w_in.shape[0]
    half = MLA_ROPE // 2
    w_t = jnp.concatenate([cq, ckv, dq, dv, sq, sv, gate], axis=1).T.astype(BF16)
    pad = lambda a, lo, width: jnp.pad(a, ((0, 0), (lo, width - lo - a.shape[1])))
    kr_p = pad(kr, MLA_NOPE, LANES)
    kr_rot_p = pad(jnp.concatenate([-kr[:, half:], kr[:, :half]], axis=1), MLA_NOPE, LANES)
    dk_p = jnp.concatenate([pad(dk[:, h * 2 * DIFF_QK:(h + 1) * 2 * DIFF_QK], 0, LANES)
                            for h in range(DIFF_HEADS)], axis=1)
    w_r = jnp.concatenate([ckv, kr_p, kr_rot_p, dk_p, sk], axis=1).astype(BF16)
    assert w_t.shape == (_T_ROWS, d) and w_r.shape == (d, _R_COLS)
    qk = MLA_NOPE + MLA_ROPE
    w_uq_t = jnp.concatenate([jnp.pad(w_uq[:, h * qk:(h + 1) * qk].T, ((0, LANES - qk), (0, 0)))
                              for h in range(MLA_HEADS)], axis=0).astype(BF16)
    kv = MLA_NOPE + MLA_V
    w_kp = jnp.concatenate([pad(w_ukv[:, h * kv:h * kv + MLA_NOPE], 0, LANES)
                            for h in range(MLA_HEADS)], axis=1).astype(BF16)
    w_v_t = jnp.concatenate([w_ukv[:, h * kv + MLA_NOPE:(h + 1) * kv].T
                             for h in range(MLA_HEADS)], axis=0).astype(BF16)
    return dict(w_t=w_t, w_r=w_r, w_uq_t=w_uq_t, w_kp=w_kp, w_v_t=w_v_t,
                gq_col=q_norm.reshape(-1, 1), gkv_col=kv_norm.reshape(-1, 1), gkv_row=kv_norm.reshape(1, -1))


def _lambda_kernel(q1_ref, k1_ref, q2_ref, k2_ref, init_ref, o_ref):
    a = jnp.sum(q1_ref[...] * k1_ref[...], axis=1, keepdims=True)
    c = jnp.sum(q2_ref[...] * k2_ref[...], axis=1, keepdims=True)
    o_ref[...] = jnp.exp(a) - jnp.exp(c) + init_ref[...]


def _diff_lambda(lam_vecs, lam_inits):
    depth = lam_vecs.shape[0]
    return pl.pallas_call(
        _lambda_kernel,
        out_shape=jax.ShapeDtypeStruct((depth, 1), F32),
        name="diff_lambda",
    )(*(lam_vecs[:, i, :] for i in range(4)), jnp.asarray(np.asarray(lam_inits, np.float32).reshape(depth, 1)))


def _tiles(seq):
    return dict(tm=min(512, seq), mla_tq=min(1024, seq), mla_tk=256, diff_tq=min(512, seq), diff_tk=256)


def kernel(x, w_in, mla_q_norm, mla_kv_norm, mla_w_uq, mla_w_ukv, diff_lambda, diff_subln, swa_sink,
           rel_bias, w_out, ln_g, ln_b):
    depth = w_in.shape[0]
    seq = x.shape[1]
    cfg = _tiles(seq)
    alpha = (2 * depth) ** 0.25
    tabs = _rope_tables(seq)
    e_min, e_max = _diff_near_offsets(cfg["diff_tq"] // cfg["diff_tk"])
    diff_bias = _diff_bias_tiles(rel_bias, cfg["diff_tq"], cfg["diff_tk"], e_min, e_max)
    swa_bias = _swa_bias_tiles(rel_bias)
    half = REL_BUCKETS // 2
    far = jnp.stack([rel_bias[half - 1, :DIFF_HEADS], rel_bias[2 * half - 1, :DIFF_HEADS]]) * LOG2E
    lam_inits = [0.8 - 0.6 * math.exp(-0.3 * l) for l in range(depth)]
    lam = _diff_lambda(diff_lambda, lam_inits)
    lw = jax.vmap(_layer_weights)(w_in, mla_q_norm, mla_kv_norm, mla_w_uq, mla_w_ukv)
    subln_col = diff_subln[:, :, None]
    w_out_b, ln_g3, ln_b3 = w_out.astype(BF16), ln_g[:, None, :], ln_b[:, None, :]
    prev = None
    for l in range(depth + 1):
        nxt = (lw, l) if l < depth else None
        outs = _mix(x, prev, nxt, tabs, cfg["tm"], alpha)
        if prev is not None:
            x, outs = outs[0], outs[1:]
        if nxt is None:
            return x
        qm, km, vm, qd, kd, vd, sq, sk, sv, g = outs
        om = _mla_attention(qm, km, vm, cfg["mla_tq"], cfg["mla_tk"])
        od = _diff_attention(qd, kd, vd, diff_bias, far, lam, subln_col, l, cfg["diff_tq"], cfg["diff_tk"],
                             lam_inits[l])
        prev = (om, od, g, sq, sk, sv, swa_bias, swa_sink, w_out_b, ln_g3, ln_b3, l)
```
